```python
import jax, jax.numpy as jnp
from jax import lax
import numpy as np

D_MODEL = 1024
BATCH = 16
SEQ = 4096
DEPTH = 1

CHUNK = 64
EPS = 1e-6
N_HEADS = 8
HEAD_DIM = 64
ATTN_WIDTH = N_HEADS * HEAD_DIM
ROT_DIM = HEAD_DIM // 4
ROPE_THETA = 500000.0
IDX_HEADS = 8
IDX_DIM = 64
TOPK_MAX = 256
POOL_WINDOWS = (2, 4, 8, 16)
POOL_GROUPS = len(POOL_WINDOWS)
POOL_WIDTH = 512
POOL_GROUP_DIM = POOL_WIDTH // POOL_GROUPS
N_BRANCHES = 2
PEER_HEADS = 8
PEER_KEYS = 128
PEER_EXPERTS = PEER_KEYS * PEER_KEYS
PEER_KEY_DIM = 64
PEER_TOPK = 16
TOKEN_BLOCK = 128

SPLIT_SIZES = (ATTN_WIDTH, ATTN_WIDTH, ATTN_WIDTH, IDX_HEADS * IDX_DIM, IDX_DIM, IDX_HEADS, POOL_WIDTH, N_BRANCHES * D_MODEL)
SPLIT_POINTS = tuple(int(v) for v in np.cumsum(SPLIT_SIZES)[:-1])
IN_WIDTH = int(sum(SPLIT_SIZES))

kernel_name = "hybrid_dsa_pool_peer_block"


def rms_norm(x, g):
    xf = x.astype(jnp.float32)
    y = xf * lax.rsqrt(jnp.mean(xf * xf, axis=-1, keepdims=True) + EPS)
    return (y * g.astype(jnp.float32)).astype(x.dtype)


def partial_rope(x, pos):
    half = ROT_DIM // 2
    inv_freq = ROPE_THETA ** (-jnp.arange(half, dtype=jnp.float32) / half)
    ang = pos.astype(jnp.float32)[:, None] * inv_freq[None, :]
    cos = jnp.cos(ang)[None, :, None, :]
    sin = jnp.sin(ang)[None, :, None, :]
    xr = x[..., :ROT_DIM].astype(jnp.float32)
    x1, x2 = xr[..., :half], xr[..., half:]
    rot = jnp.concatenate([x1 * cos - x2 * sin, x2 * cos + x1 * sin], axis=-1)
    return jnp.concatenate([rot.astype(x.dtype), x[..., ROT_DIM:]], axis=-1)


def dsa_attention(q, k, v, q_i, k_i, w_i):
    B, S = q.shape[0], q.shape[1]
    n_blocks = S // CHUNK
    topk = min(TOPK_MAX, S // 4)
    key_pos = jnp.arange(S)

    def to_blocks(a):
        return jnp.moveaxis(a.reshape((B, n_blocks, CHUNK) + a.shape[2:]), 1, 0)

    def one_block(args):
        c, qb, qib, wib = args
        limit = (c + 1) * CHUNK
        admissible = key_pos < limit
        logits = jnp.einsum('bthd,bsd->bths', qib, k_i, preferred_element_type=jnp.float32) * (IDX_DIM ** -0.5)
        score = jnp.einsum('bth,bths->bts', wib.astype(jnp.float32), jax.nn.relu(logits))
        score = jnp.where(admissible[None, None, :], score, -jnp.inf)
        _, idx = lax.top_k(score, topk)
        valid = idx < limit
        k_sel = jax.vmap(lambda kb, ib: kb[ib])(k, idx)
        v_sel = jax.vmap(lambda vb, ib: vb[ib])(v, idx)
        s = jnp.einsum('bthd,btkhd->bthk', qb, k_sel, preferred_element_type=jnp.float32) * (HEAD_DIM ** -0.5)
        s = jnp.where(valid[:, :, None, :], s, -jnp.inf)
        p = jax.nn.softmax(s, axis=-1)
        return jnp.einsum('bthk,btkhd->bthd', p.astype(v.dtype), v_sel)

    out = lax.map(one_block, (jnp.arange(n_blocks), to_blocks(q), to_blocks(q_i), to_blocks(w_i)))
    return jnp.moveaxis(out, 0, 1).reshape(B, S, ATTN_WIDTH)


def multiscale_pool(p, pool_w, pool_scale):
    B, S, _ = p.shape
    pf = p.astype(jnp.float32).reshape(B, S, POOL_GROUPS, POOL_GROUP_DIM)
    csum = jnp.cumsum(pf, axis=1)
    t1 = jnp.arange(1, S + 1, dtype=jnp.float32)
    groups = []
    for g, w in enumerate(POOL_WINDOWS):
        c = csum[:, :, g]
        lower = jnp.concatenate([jnp.zeros((B, w, POOL_GROUP_DIM), jnp.float32), c[:, :S - w]], axis=1)
        count = jnp.minimum(t1, float(w))[None, :, None]
        groups.append((c - lower) / count - pf[:, :, g])
    pooled = jnp.stack(groups, axis=2).astype(p.dtype)
    mixed = jnp.einsum('bsgc,gcd->bsgd', pooled, pool_w)
    return mixed.reshape(B, S, POOL_WIDTH) * pool_scale


def peer_ffn(h, wq, subkeys, u, v):
    B, S, D = h.shape
    hb = h.reshape((B * S) // TOKEN_BLOCK, TOKEN_BLOCK, D)

    def one_block(xb):
        tb = xb.shape[0]
        q = (xb @ wq).reshape(tb, PEER_HEADS, 2, PEER_KEY_DIM)
        sub = jnp.einsum('thpd,hpnd->thpn', q, subkeys, preferred_element_type=jnp.float32)
        s_half, i_half = lax.top_k(sub, PEER_TOPK)
        cand = (s_half[:, :, 0, :, None] + s_half[:, :, 1, None, :]).reshape(tb, PEER_HEADS, PEER_TOPK * PEER_TOPK)
        cand_idx = (i_half[:, :, 0, :, None] * PEER_KEYS + i_half[:, :, 1, None, :]).reshape(tb, PEER_HEADS, PEER_TOPK * PEER_TOPK)
        top_s, pos = lax.top_k(cand, PEER_TOPK)
        expert = jnp.take_along_axis(cand_idx, pos, axis=-1)
        g = jax.nn.softmax(top_s, axis=-1)
        u_sel = u[expert]
        v_sel = v[expert]
        a = jax.nn.gelu(jnp.einsum('td,thkd->thk', xb, u_sel, preferred_element_type=jnp.float32), approximate=False)
        return jnp.einsum('thk,thkd->td', (g * a).astype(v.dtype), v_sel)

    return lax.map(one_block, hb).reshape(B, S, D)


def setup_inputs(seed: int = 0) -> dict:
    key = jax.random.key(seed)
    ks = jax.random.split(key, 16)
    f32 = jnp.float32
    L = DEPTH
    nrm = lambda k, shape, scale: jax.random.normal(k, shape, f32) * scale
    return {
        "x": jax.random.normal(ks[0], (BATCH, SEQ, D_MODEL), f32),
        "norm1_g": 1.0 + nrm(ks[1], (L, D_MODEL), 0.02),
        "w_in": nrm(ks[2], (L, D_MODEL, IN_WIDTH), D_MODEL ** -0.5),
        "q_norm_g": 1.0 + nrm(ks[3], (L, HEAD_DIM), 0.02),
        "k_norm_g": 1.0 + nrm(ks[4], (L, HEAD_DIM), 0.02),
        "pool_w": nrm(ks[5], (L, POOL_GROUPS, POOL_GROUP_DIM, POOL_GROUP_DIM), POOL_GROUP_DIM ** -0.5),
        "pool_scale": 1.0 + nrm(ks[6], (L, POOL_WIDTH), 0.02),
        "w_branch_attn": nrm(ks[7], (L, ATTN_WIDTH, D_MODEL), ATTN_WIDTH ** -0.5),
        "w_branch_pool": nrm(ks[8], (L, POOL_WIDTH, D_MODEL), POOL_WIDTH ** -0.5),
        "w_out": nrm(ks[9], (L, D_MODEL, D_MODEL), D_MODEL ** -0.5),
        "norm2_g": 1.0 + nrm(ks[10], (L, D_MODEL), 0.02),
        "peer_wq": nrm(ks[11], (L, D_MODEL, PEER_HEADS * 2 * PEER_KEY_DIM), D_MODEL ** -0.5),
        "peer_subkeys": nrm(ks[12], (L, PEER_HEADS, 2, PEER_KEYS, PEER_KEY_DIM), PEER_KEY_DIM ** -0.5),
        "peer_u": nrm(ks[13], (L, PEER_EXPERTS, D_MODEL), D_MODEL ** -0.5),
        "peer_v": nrm(ks[14], (L, PEER_EXPERTS, D_MODEL), PEER_HEADS ** -0.5),
    }


def reference(x, norm1_g, w_in, q_norm_g, k_norm_g, pool_w, pool_scale, w_branch_attn, w_branch_pool, w_out, norm2_g, peer_wq, peer_subkeys, peer_u, peer_v):
    B, S, _ = x.shape
    pos = jnp.arange(S)
    for l in range(DEPTH):
        xn = rms_norm(x, norm1_g[l])
        proj = xn @ w_in[l]
        q, k, v, qi, ki, wi, p, gl = jnp.split(proj, SPLIT_POINTS, axis=-1)
        q = partial_rope(rms_norm(q.reshape(B, S, N_HEADS, HEAD_DIM), q_norm_g[l]), pos)
        k = partial_rope(rms_norm(k.reshape(B, S, N_HEADS, HEAD_DIM), k_norm_g[l]), pos)
        v = v.reshape(B, S, N_HEADS, HEAD_DIM)
        qi = partial_rope(qi.reshape(B, S, IDX_HEADS, IDX_DIM), pos)
        ki = partial_rope(ki.reshape(B, S, 1, IDX_DIM), pos)[:, :, 0]
        wi = wi * (IDX_HEADS ** -0.5)
        y_attn = dsa_attention(q, k, v, qi, ki, wi) @ w_branch_attn[l]
        y_pool = multiscale_pool(p, pool_w[l], pool_scale[l]) @ w_branch_pool[l]
        g_attn, g_pool = jnp.split(jax.nn.sigmoid(gl), 2, axis=-1)
        x = x + (g_attn * y_attn + g_pool * y_pool) @ w_out[l]
        x = x + peer_ffn(rms_norm(x, norm2_g[l]), peer_wq[l], peer_subkeys[l], peer_u[l], peer_v[l])
    return x
```

```python
import functools

import jax
import jax.numpy as jnp
import numpy as np
from jax import lax
from jax.experimental import pallas as pl
from jax.experimental.pallas import tpu as pltpu

CHUNK = 64
EPS = 1e-6
N_HEADS = 8
HEAD_DIM = 64
ATTN_WIDTH = N_HEADS * HEAD_DIM
ROT_HALF = HEAD_DIM // 8
ROPE_THETA = 500000.0
IDX_HEADS = 8
IDX_DIM = 64
TOPK_MAX = 256
POOL_WINDOWS = (2, 4, 8, 16)
POOL_WIDTH = 512
POOL_GROUP_DIM = POOL_WIDTH // len(POOL_WINDOWS)
POOL_HALO = 16
PEER_HEADS = 8
PEER_KEYS = 128
PEER_KEY_DIM = 64
PEER_TOPK = 16

LANES = 128
SUBLANES = 8
VMEM_LIMIT = 56 * 1024 * 1024

F32 = jnp.float32
BF16 = jnp.bfloat16
I32 = jnp.int32
INT_MIN = -2147483648
NEG = -1e30
NT_DIMS = (((1,), (1,)), ((), ()))


def _params(sem):
    return pltpu.CompilerParams(dimension_semantics=sem, vmem_limit_bytes=VMEM_LIMIT)


def _rope(t, c, s_lo, s_hi):
    w = t.shape[-1]
    return t * c + pltpu.roll(t, w - ROT_HALF, 1) * s_lo + pltpu.roll(t, ROT_HALF, 1) * s_hi


def _proj_body(x_ref, g1_ref, wqkv_ref, wqi_ref, wki_ref, wwi_ref, wp_ref, wgl_ref, qg_ref, kg_ref, bd_ref,
               c_ref, slo_ref, shi_ref,
               q_ref, k_ref, vt_ref, qi_ref, ki_ref, wi_ref, p_ref, gate_ref):
    x = x_ref[...]
    xn = x * lax.rsqrt(jnp.mean(x * x, axis=-1, keepdims=True) + EPS) * g1_ref[...]
    xb = xn.astype(BF16)
    c, s_lo, s_hi = c_ref[...], slo_ref[...], shi_ref[...]

    def head_norm(t, g):
        ms = jnp.dot((t * t).astype(BF16), bd_ref[...], preferred_element_type=F32)
        return t * lax.rsqrt(ms + EPS) * g

    qkv = jnp.dot(xb, wqkv_ref[...], preferred_element_type=F32)
    w = ATTN_WIDTH
    q = _rope(head_norm(qkv[:, :w], qg_ref[...]), c, s_lo, s_hi) * (HEAD_DIM ** -0.5)
    k = _rope(head_norm(qkv[:, w:2 * w], kg_ref[...]), c, s_lo, s_hi)
    q_ref[...] = q.astype(BF16)
    k_ref[...] = k.astype(BF16)
    vt_ref[0, 0] = qkv[:, 2 * w:].T.astype(BF16)
    qi = jnp.dot(xb, wqi_ref[...], preferred_element_type=F32)
    qi_ref[...] = (_rope(qi, c, s_lo, s_hi) * (IDX_DIM ** -0.5)).astype(BF16)
    ki = jnp.dot(xb, wki_ref[...], preferred_element_type=F32)
    ki = _rope(ki, c[:, :LANES], s_lo[:, :LANES], s_hi[:, :LANES])
    ki_ref[...] = ki[:, :IDX_DIM].astype(BF16)
    wi_ref[...] = jnp.dot(xb, wwi_ref[...], preferred_element_type=F32) * (IDX_HEADS ** -0.5)
    p_ref[...] = jnp.dot(xb, wp_ref[...], preferred_element_type=F32)
    gate_ref[...] = jax.nn.sigmoid(jnp.dot(xb, wgl_ref[...], preferred_element_type=F32)).astype(BF16)


def _rope_tables(seq):
    inv_freq = ROPE_THETA ** (-jnp.arange(ROT_HALF, dtype=F32) / ROT_HALF)
    ang = jnp.arange(seq, dtype=F32)[:, None] * inv_freq[None, :]
    cos, sin = jnp.cos(ang), jnp.sin(ang)
    rest = HEAD_DIM - 2 * ROT_HALF
    ones = jnp.ones((seq, rest), F32)
    zeros = jnp.zeros((seq, rest), F32)
    zh = jnp.zeros((seq, ROT_HALF), F32)
    c = jnp.concatenate([cos, cos, ones], axis=1)
    s_lo = jnp.concatenate([-sin, zh, zeros], axis=1)
    s_hi = jnp.concatenate([zh, sin, zeros], axis=1)
    tile = lambda t: jnp.tile(t, (1, N_HEADS))
    return tile(c), tile(s_lo), tile(s_hi)


def _input_projection(x2, norm1_g, w_in, q_norm_g, k_norm_g, batch, seq, tm, kb):
    n, d = x2.shape
    w = ATTN_WIDTH
    o = np.cumsum([0, w, w, w, IDX_HEADS * IDX_DIM, IDX_DIM, IDX_HEADS, POOL_WIDTH, 2 * d])
    wb = w_in.astype(BF16)
    wqkv = wb[:, o[0]:o[3]]
    wqi = wb[:, o[3]:o[4]]
    wki = jnp.pad(wb[:, o[4]:o[5]], ((0, 0), (0, LANES - IDX_DIM)))
    wwi = jnp.pad(wb[:, o[5]:o[6]], ((0, 0), (0, LANES - IDX_HEADS)))
    wp = wb[:, o[6]:o[7]]
    wgl = wb[:, o[7]:o[8]]
    bd = jnp.kron(jnp.eye(N_HEADS, dtype=F32), jnp.full((HEAD_DIM, HEAD_DIM), 1.0 / HEAD_DIM, F32)).astype(BF16)
    c, s_lo, s_hi = _rope_tables(seq)
    tps = seq // tm
    const = lambda shape: pl.BlockSpec(shape, lambda i: (0,) * len(shape))
    row = lambda width: pl.BlockSpec((tm, width), lambda i: (i, 0))
    tab = pl.BlockSpec((tm, w), lambda i: (i % tps, 0))
    per_kb = kb // tm
    out_shapes = (
        jax.ShapeDtypeStruct((n, w), BF16),
        jax.ShapeDtypeStruct((n, w), BF16),
        jax.ShapeDtypeStruct((batch, seq // kb, w, kb), BF16),
        jax.ShapeDtypeStruct((n, w), BF16),
        jax.ShapeDtypeStruct((n, IDX_DIM), BF16),
        jax.ShapeDtypeStruct((n, LANES), F32),
        jax.ShapeDtypeStruct((n, POOL_WIDTH), F32),
        jax.ShapeDtypeStruct((n, 2 * d), BF16),
    )
    out_specs = (
        row(w), row(w),
        pl.BlockSpec((1, 1, w, tm), lambda i: (i // tps, (i % tps) // per_kb, 0, (i % tps) % per_kb)),
        row(w), row(IDX_DIM), row(LANES), row(POOL_WIDTH), row(2 * d),
    )
    return pl.pallas_call(
        _proj_body,
        grid=(n // tm,),
        in_specs=[row(d), const((1, d)), const(wqkv.shape), const(wqi.shape), const(wki.shape), const(wwi.shape),
                  const(wp.shape), const(wgl.shape), const((1, w)), const((1, w)), const(bd.shape), tab, tab, tab],
        out_specs=out_specs,
        out_shape=out_shapes,
        compiler_params=_params(("parallel",)),
        name="input_projection",
    )(x2, norm1_g.reshape(1, d), wqkv, wqi, wki, wwi, wp, wgl,
      jnp.tile(q_norm_g, N_HEADS).reshape(1, w), jnp.tile(k_norm_g, N_HEADS).reshape(1, w), bd, c, s_lo, s_hi)


def _sortable(v):
    b = lax.bitcast_convert_type(v, I32)
    b = jnp.where(b == INT_MIN, 0, b)
    return jnp.where(b < 0, b ^ 0x7FFFFFFF, b)


def _dsa_body(q_ref, qi_ref, wi_ref, k_ref, vt_ref, ki_ref, o_ref, key_s, bias_s, acc_s, *, seq, tq, kb, topk):
    j = pl.program_id(1)
    nblk = ((j + 1) * tq + kb - 1) // kb
    lane = lax.broadcasted_iota(I32, (1, tq), 1)
    lim = j * tq + (lane // CHUNK + 1) * CHUNK
    row_iota = lax.broadcasted_iota(I32, (kb, tq), 0)
    wi_t = wi_ref[...].T[:IDX_HEADS, :]
    qi = qi_ref[...]

    def rows(i):
        return pl.ds(pl.multiple_of(i * kb, kb), kb)

    def score_block(i, carry):
        kib = ki_ref[rows(i), :]
        acc = jnp.zeros((kb, tq), F32)
        for h in range(IDX_HEADS):
            lg = lax.dot_general(kib, qi[:, h * IDX_DIM:(h + 1) * IDX_DIM], NT_DIMS, preferred_element_type=F32)
            acc = acc + jnp.maximum(lg, 0.0) * wi_t[h:h + 1, :]
        key_s[rows(i), :] = jnp.where(i * kb + row_iota < lim, _sortable(acc), INT_MIN)
        return carry

    lax.fori_loop(0, nblk, score_block, 0)

    def count(pred):
        def body(i, c):
            m = pred(key_s[rows(i), :], i * kb + row_iota)
            return c + jnp.sum(m.astype(I32).reshape(kb // SUBLANES, SUBLANES, tq), axis=0)
        c8 = lax.fori_loop(0, nblk, body, jnp.zeros((SUBLANES, tq), I32))
        return jnp.sum(c8, axis=0, keepdims=True)

    def value_bit(it, tu):
        cand_u = tu | lax.shift_left(jnp.int32(1), 31 - it)
        cand_s = cand_u ^ INT_MIN
        return jnp.where(count(lambda blk, idx: blk >= cand_s) >= topk, cand_u, tu)

    thr = lax.fori_loop(0, 32, value_bit, jnp.zeros((1, tq), I32)) ^ INT_MIN
    need = topk - count(lambda blk, idx: blk > thr)
    idx_bits = int(seq).bit_length()

    def index_bit(it, jj):
        cand = jj | lax.shift_left(jnp.int32(1), idx_bits - 1 - it)
        return jnp.where(count(lambda blk, idx: (blk == thr) & (idx < cand)) <= need, cand, jj)

    tie_end = lax.fori_loop(0, idx_bits, index_bit, jnp.zeros((1, tq), I32))

    def bias_block(i, carry):
        blk = key_s[rows(i), :]
        idx = i * kb + row_iota
        sel = ((blk > thr) | ((blk == thr) & (idx < tie_end))) & (idx < lim)
        bias_s[rows(i), :] = jnp.where(sel, 0.0, NEG)
        return carry

    lax.fori_loop(0, nblk, bias_block, 0)

    q = q_ref[...]
    pair_lane = lax.broadcasted_iota(I32, (tq, 2 * HEAD_DIM), 1)
    qm = []
    for h in range(N_HEADS):
        pair = q[:, (h // 2) * 2 * HEAD_DIM:(h // 2 + 1) * 2 * HEAD_DIM]
        qm.append(jnp.where((pair_lane // HEAD_DIM) == (h % 2), pair, jnp.zeros_like(pair)))
    acc_s[...] = jnp.zeros_like(acc_s)

    def attend(i, carry):
        ms, ls = carry
        bias = bias_s[rows(i), :]
        new_m, new_l = [], []
        for h in range(N_HEADS):
            hs = slice(h * HEAD_DIM, (h + 1) * HEAD_DIM)
            kblk = k_ref[rows(i), (h // 2) * 2 * HEAD_DIM:(h // 2 + 1) * 2 * HEAD_DIM]
            s = lax.dot_general(kblk, qm[h], NT_DIMS, preferred_element_type=F32) + bias
            m_new = jnp.maximum(ms[h], jnp.max(s, axis=0, keepdims=True))
            alpha = jnp.exp(ms[h] - m_new)
            p = jnp.exp(s - m_new)
            new_l.append(alpha * ls[h] + jnp.sum(p, axis=0, keepdims=True))
            new_m.append(m_new)
            pv = jnp.dot(vt_ref[0, i, hs, :], p.astype(BF16), preferred_element_type=F32)
            acc_s[hs, :] = alpha * acc_s[hs, :] + pv
        return tuple(new_m), tuple(new_l)

    m0 = tuple(jnp.full((1, tq), NEG, F32) for _ in range(N_HEADS))
    l0 = tuple(jnp.zeros((1, tq), F32) for _ in range(N_HEADS))
    _, ls = lax.fori_loop(0, nblk, attend, (m0, l0))
    for h in range(N_HEADS):
        hs = slice(h * HEAD_DIM, (h + 1) * HEAD_DIM)
        acc_s[hs, :] = acc_s[hs, :] / ls[h]
    o_ref[...] = acc_s[...].T.astype(BF16)


def _dsa_attention(q, qi, wi, k, vt, ki, batch, seq, tq, kb):
    n, w = q.shape
    topk = min(TOPK_MAX, seq // 4)
    nq = seq // tq
    tile = lambda width: pl.BlockSpec((tq, width), lambda b, j: (b * nq + j, 0))
    whole = lambda width: pl.BlockSpec((seq, width), lambda b, j: (b, 0))
    return pl.pallas_call(
        functools.partial(_dsa_body, seq=seq, tq=tq, kb=kb, topk=topk),
        grid=(batch, nq),
        in_specs=[tile(w), tile(w), tile(LANES), whole(w),
                  pl.BlockSpec((1, seq // kb, w, kb), lambda b, j: (b, 0, 0, 0)), whole(IDX_DIM)],
        out_specs=tile(w),
        out_shape=jax.ShapeDtypeStruct((n, w), BF16),
        scratch_shapes=[pltpu.VMEM((seq, tq), I32), pltpu.VMEM((seq, tq), F32), pltpu.VMEM((w, tq), F32)],
        compiler_params=_params(("parallel", "arbitrary")),
        name="dsa_attention",
    )(q, qi, wi, k, vt, ki)


def _mix_body(attn_ref, p_ref, halo_ref, gate_ref, x_ref, wa_ref, wpb_ref, wo_ref, pw_ref, ps_ref, g2_ref,
              h_ref, hn_ref, ext_s, *, tm, tps):
    st = pl.program_id(0) % tps
    ext_s[0:POOL_HALO, :] = jnp.where(st == 0, 0.0, halo_ref[...])
    ext_s[POOL_HALO:POOL_HALO + tm, :] = p_ref[...]
    t1 = (st * tm + 1 + lax.broadcasted_iota(I32, (tm, POOL_GROUP_DIM), 0)).astype(F32)
    mixed = []
    for g, win in enumerate(POOL_WINDOWS):
        ls = slice(g * POOL_GROUP_DIM, (g + 1) * POOL_GROUP_DIM)
        frame = ext_s[POOL_HALO:POOL_HALO + tm, ls]
        tot = frame
        for dlt in range(1, win):
            tot = tot + ext_s[POOL_HALO - dlt:POOL_HALO - dlt + tm, ls]
        pooled = tot / jnp.minimum(t1, float(win)) - frame
        mixed.append(jnp.dot(pooled.astype(BF16), pw_ref[g], preferred_element_type=F32))
    mixed = jnp.concatenate(mixed, axis=1) * ps_ref[...]
    y_pool = jnp.dot(mixed.astype(BF16), wpb_ref[...], preferred_element_type=F32)
    y_attn = jnp.dot(attn_ref[...], wa_ref[...], preferred_element_type=F32)
    d = y_attn.shape[1]
    gate = gate_ref[...].astype(F32)
    z = gate[:, :d] * y_attn + gate[:, d:] * y_pool
    h = x_ref[...] + jnp.dot(z.astype(BF16), wo_ref[...], preferred_element_type=F32)
    h_ref[...] = h
    hn = h * lax.rsqrt(jnp.mean(h * h, axis=-1, keepdims=True) + EPS) * g2_ref[...]
    hn_ref[...] = hn.astype(BF16)


def _mixer_output(attn, p, gate, x2, w_branch_attn, w_branch_pool, w_out, pool_w, pool_scale, norm2_g, seq, tm):
    n, d = x2.shape
    tps = seq // tm
    hb = tm // POOL_HALO
    const = lambda shape: pl.BlockSpec(shape, lambda i: (0,) * len(shape))
    row = lambda width: pl.BlockSpec((tm, width), lambda i: (i, 0))
    return pl.pallas_call(
        functools.partial(_mix_body, tm=tm, tps=tps),
        grid=(n // tm,),
        in_specs=[row(ATTN_WIDTH), row(POOL_WIDTH),
                  pl.BlockSpec((POOL_HALO, POOL_WIDTH), lambda i: (jnp.maximum(i * hb - 1, 0), 0)),
                  row(2 * d), row(d), const((ATTN_WIDTH, d)), const((POOL_WIDTH, d)), const((d, d)),
                  const(pool_w.shape), const((1, POOL_WIDTH)), const((1, d))],
        out_specs=(row(d), row(d)),
        out_shape=(jax.ShapeDtypeStruct((n, d), F32), jax.ShapeDtypeStruct((n, d), BF16)),
        scratch_shapes=[pltpu.VMEM((POOL_HALO + tm, POOL_WIDTH), F32)],
        compiler_params=_params(("parallel",)),
        name="mixer_output",
    )(attn, p, p, gate, x2, w_branch_attn.astype(BF16), w_branch_pool.astype(BF16), w_out.astype(BF16),
      pool_w.astype(BF16), pool_scale.reshape(1, POOL_WIDTH), norm2_g.reshape(1, d))


def _candidate_pairs():
    return [(a, b) for a in range(PEER_TOPK) for b in range(PEER_TOPK) if (a + 1) * (b + 1) <= PEER_TOPK]


def _route_body(hn_ref, wqt_ref, kbig_ref, a0_ref, l0_ref, b1_ref, r1_ref, vals_s, rank_s, ex_s, topv_s, tope_s,
                *, tr):
    nk, nh = PEER_KEYS, PEER_HEADS
    half_rows = nh * PEER_KEY_DIM
    qt = lax.dot_general(wqt_ref[...], hn_ref[...], NT_DIMS, preferred_element_type=F32).astype(BF16)
    for p in range(2):
        sub = jnp.dot(kbig_ref[p], qt[p * half_rows:(p + 1) * half_rows], preferred_element_type=F32)
        vals_s[p] = sub.reshape(nk, nh, tr)
    rank_s[...] = jnp.full(rank_s.shape, float(PEER_TOPK), F32)
    ex_s[...] = jnp.zeros(ex_s.shape, F32)
    key_iota = lax.broadcasted_iota(I32, (nk, nh, tr), 0)

    def extract(kk, first):
        new_first = []
        for p in range(2):
            v = vals_s[p]
            m = jnp.max(v, axis=0)
            idx = jnp.min(jnp.where(v == m[None], key_iota, nk), axis=0)
            hit = key_iota == idx[None]
            vals_s[p] = jnp.where(hit, -jnp.inf, v)
            top = jnp.where(kk == 0, m, first[p])
            e = jnp.exp(m - top)
            rank_s[p] = jnp.where(hit, kk.astype(F32), rank_s[p])
            ex_s[p] = jnp.where(hit, e[None], ex_s[p])
            topv_s[p, kk] = m
            tope_s[p, kk] = e
            new_first.append(top)
        return tuple(new_first)

    zero = jnp.zeros((nh, tr), F32)
    lax.fori_loop(0, PEER_TOPK, extract, (zero, zero))

    pairs = _candidate_pairs()
    v0 = [topv_s[0, a] for a in range(PEER_TOPK)]
    v1 = [topv_s[1, b] for b in range(PEER_TOPK)]
    cand = [v0[a] + v1[b] for a, b in pairs]
    rank = [jnp.zeros((nh, tr), F32) for _ in pairs]
    for ia, (a0, a1) in enumerate(pairs):
        for ib in range(ia + 1, len(pairs)):
            b0, b1 = pairs[ib]
            if a0 <= b0 and a1 <= b1:
                rank[ib] = rank[ib] + 1.0
            else:
                wins = jnp.where(cand[ia] >= cand[ib], 1.0, 0.0)
                rank[ib] = rank[ib] + wins
                rank[ia] = rank[ia] + (1.0 - wins)
    e0 = [tope_s[0, a] for a in range(PEER_TOPK)]
    e1 = [tope_s[1, b] for b in range(PEER_TOPK)]
    width = [jnp.zeros((nh, tr), F32) for _ in range(PEER_TOPK)]
    z = jnp.zeros((nh, tr), F32)
    for ic, (a, b) in enumerate(pairs):
        sel = jnp.where(rank[ic] < float(PEER_TOPK), 1.0, 0.0)
        width[a] = width[a] + sel
        z = z + sel * (e0[a] * e1[b])
    inv_z = 1.0 / z
    r0 = rank_s[0]
    l0 = jnp.zeros((nk, nh, tr), F32)
    for a in range(PEER_TOPK):
        l0 = jnp.where(r0 == float(a), width[a][None], l0)
    a0_ref[0] = (ex_s[0] * inv_z[None]).reshape(nk * nh, tr)
    l0_ref[0] = l0.reshape(nk * nh, tr)
    b1_ref[0] = ex_s[1].reshape(nk * nh, tr)
    r1_ref[0] = rank_s[1].reshape(nk * nh, tr)


def _peer_routing(hn, peer_wq, peer_subkeys, tr):
    n, d = hn.shape
    nk, nh, kd = PEER_KEYS, PEER_HEADS, PEER_KEY_DIM
    wqt = peer_wq.reshape(d, nh, 2, kd).transpose(2, 1, 3, 0).reshape(2 * nh * kd, d).astype(BF16)
    eye = jnp.eye(nh, dtype=peer_subkeys.dtype)
    kbig = jnp.einsum("hpnd,hg->pnhgd", peer_subkeys, eye).reshape(2, nk * nh, nh * kd).astype(BF16)
    rows = nk * nh
    assert tr == LANES
    out = jax.ShapeDtypeStruct((n // tr, rows, tr), F32)
    spec = pl.BlockSpec((1, rows, tr), lambda i: (i, 0, 0))
    return pl.pallas_call(
        functools.partial(_route_body, tr=tr),
        grid=(n // tr,),
        in_specs=[pl.BlockSpec((tr, d), lambda i: (i, 0)),
                  pl.BlockSpec(wqt.shape, lambda i: (0, 0)),
                  pl.BlockSpec(kbig.shape, lambda i: (0, 0, 0))],
        out_specs=(spec, spec, spec, spec),
        out_shape=(out, out, out, out),
        scratch_shapes=[pltpu.VMEM((2, nk, nh, tr), F32), pltpu.VMEM((2, nk, nh, tr), F32),
                        pltpu.VMEM((2, nk, nh, tr), F32), pltpu.VMEM((2, PEER_TOPK, nh, tr), F32),
                        pltpu.VMEM((2, PEER_TOPK, nh, tr), F32)],
        compiler_params=_params(("parallel",)),
        name="peer_routing",
    )(hn, wqt, kbig)


def _expert_body(hn_ref, h_ref, u_ref, vt_ref, a0_ref, l0_ref, b1_ref, r1_ref, y_ref, acc_s, ga_s, *, tm, te):
    eb = pl.program_id(1)
    nk, nh = PEER_KEYS, PEER_HEADS

    @pl.when(eb == 0)
    def _():
        acc_s[...] = jnp.zeros_like(acc_s)

    act = lax.dot_general(u_ref[...], hn_ref[...], NT_DIMS, preferred_element_type=F32)
    act = 0.5 * act * (1.0 + lax.erf(act * (2.0 ** -0.5)))
    for il in range(te // nk):
        i = eb * (te // nk) + il
        for c in range(tm // LANES):
            g = jnp.zeros((nk, LANES), F32)
            for h in range(nh):
                a_row = a0_ref[c, pl.ds(i * nh + h, 1), :]
                l_row = l0_ref[c, pl.ds(i * nh + h, 1), :]
                b1 = b1_ref[c, pl.ds(h, nk, stride=nh), :]
                r1 = r1_ref[c, pl.ds(h, nk, stride=nh), :]
                g = g + a_row * jnp.where(r1 < l_row, b1, 0.0)
            blk = (slice(il * nk, (il + 1) * nk), slice(c * LANES, (c + 1) * LANES))
            ga_s[blk] = (g * act[blk]).astype(BF16)
    acc_s[...] += jnp.dot(vt_ref[...], ga_s[...], preferred_element_type=F32)

    @pl.when(eb == pl.num_programs(1) - 1)
    def _():
        y_ref[...] = h_ref[...] + acc_s[...].T


def _peer_experts(hn, h, peer_u, peer_v, a0, l0, b1, r1, tm, te):
    n, d = hn.shape
    ne = peer_u.shape[0]
    rows = a0.shape[1]
    tok = pl.BlockSpec((tm // LANES, rows, LANES), lambda t, e: (t, 0, 0))
    return pl.pallas_call(
        functools.partial(_expert_body, tm=tm, te=te),
        grid=(n // tm, ne // te),
        in_specs=[pl.BlockSpec((tm, d), lambda t, e: (t, 0)), pl.BlockSpec((tm, d), lambda t, e: (t, 0)),
                  pl.BlockSpec((te, d), lambda t, e: (e, 0)), pl.BlockSpec((d, te), lambda t, e: (0, e)),
                  tok, tok, tok, tok],
        out_specs=pl.BlockSpec((tm, d), lambda t, e: (t, 0)),
        out_shape=jax.ShapeDtypeStruct((n, d), F32),
        scratch_shapes=[pltpu.VMEM((d, tm), F32), pltpu.VMEM((te, tm), BF16)],
        compiler_params=_params(("parallel", "arbitrary")),
        name="peer_experts",
    )(hn, h, peer_u.astype(BF16), peer_v.T.astype(BF16), a0, l0, b1, r1)


def _tiles(batch, seq):
    return dict(tm=256, tq=128, kb=512, tr=128, te_tm=256, te=1024)


def kernel(x, norm1_g, w_in, q_norm_g, k_norm_g, pool_w, pool_scale, w_branch_attn, w_branch_pool, w_out, norm2_g,
           peer_wq, peer_subkeys, peer_u, peer_v):
    batch, seq, d = x.shape
    t = _tiles(batch, seq)
    x2 = x.reshape(batch * seq, d)
    for l in range(norm1_g.shape[0]):
        q, k, vt, qi, ki, wi, p, gate = _input_projection(
            x2, norm1_g[l], w_in[l], q_norm_g[l], k_norm_g[l], batch, seq, t["tm"], t["kb"])
        attn = _dsa_attention(q, qi, wi, k, vt, ki, batch, seq, t["tq"], t["kb"])
        h, hn = _mixer_output(attn, p, gate, x2, w_branch_attn[l], w_branch_pool[l], w_out[l], pool_w[l],
                              pool_scale[l], norm2_g[l], seq, t["tm"])
        a0, l0, b1, r1 = _peer_routing(hn, peer_wq[l], peer_subkeys[l], t["tr"])
        x2 = _peer_experts(hn, h, peer_u[l], peer_v[l], a0, l0, b1, r1, t["te_tm"], t["te"])
    return x2.reshape(batch, seq, d)
```

```python
import functools

import jax
import jax.numpy as jnp
import numpy as np
from jax import lax
from jax.experimental import pallas as pl
from jax.experimental.pallas import tpu as pltpu

CHUNK = 64
EPS = 1e-6
N_HEADS = 8
HEAD_DIM = 64
ATTN_WIDTH = N_HEADS * HEAD_DIM
ROT_HALF = HEAD_DIM // 8
ROPE_THETA = 500000.0
IDX_HEADS = 8
IDX_DIM = 64
TOPK_MAX = 256
POOL_WINDOWS = (2, 4, 8, 16)
POOL_WIDTH = 512
POOL_GROUP_DIM = POOL_WIDTH // len(POOL_WINDOWS)
POOL_HALO = 16
PEER_HEADS = 8
PEER_KEYS = 128
PEER_KEY_DIM = 64
PEER_TOPK = 16

LANES = 128
SUBLANES = 8
VMEM_LIMIT = 56 * 1024 * 1024

F32 = jnp.float32
BF16 = jnp.bfloat16
I32 = jnp.int32
INT_MIN = -2147483648
NEG = -1e30
LOG2_E = 1.4426950408889634
ATTN_HEAD_GROUP = 4
NT_DIMS = (((1,), (1,)), ((), ()))


def _params(sem):
    return pltpu.CompilerParams(dimension_semantics=sem, vmem_limit_bytes=VMEM_LIMIT)


def _rope(t, c, s_lo, s_hi):
    w = t.shape[-1]
    return t * c + pltpu.roll(t, w - ROT_HALF, 1) * s_lo + pltpu.roll(t, ROT_HALF, 1) * s_hi


def _proj_body(x_ref, g1_ref, wqkv_ref, wqi_ref, wki_ref, wwi_ref, wp_ref, wgl_ref, qg_ref, kg_ref, bd_ref,
               c_ref, slo_ref, shi_ref,
               q_ref, k_ref, vt_ref, qi_ref, ki_ref, wi_ref, p_ref, gate_ref):
    x = x_ref[...]
    xn = x * lax.rsqrt(jnp.mean(x * x, axis=-1, keepdims=True) + EPS) * g1_ref[...]
    xb = xn.astype(BF16)
    c, s_lo, s_hi = c_ref[...], slo_ref[...], shi_ref[...]

    def head_norm(t, g):
        ms = jnp.dot((t * t).astype(BF16), bd_ref[...], preferred_element_type=F32)
        return t * lax.rsqrt(ms + EPS) * g

    qkv = jnp.dot(xb, wqkv_ref[...], preferred_element_type=F32)
    w = ATTN_WIDTH
    q = _rope(head_norm(qkv[:, :w], qg_ref[...]), c, s_lo, s_hi) * (HEAD_DIM ** -0.5 * LOG2_E)
    k = _rope(head_norm(qkv[:, w:2 * w], kg_ref[...]), c, s_lo, s_hi)
    q_ref[...] = q.astype(BF16)
    k_ref[...] = k.astype(BF16)
    vt_ref[0, 0] = qkv[:, 2 * w:].T.astype(BF16)
    qi = jnp.dot(xb, wqi_ref[...], preferred_element_type=F32)
    qi_ref[...] = (_rope(qi, c, s_lo, s_hi) * (IDX_DIM ** -0.5)).astype(BF16)
    ki = jnp.dot(xb, wki_ref[...], preferred_element_type=F32)
    ki = _rope(ki, c[:, :LANES], s_lo[:, :LANES], s_hi[:, :LANES])
    ki_ref[...] = ki[:, :IDX_DIM].astype(BF16)
    wi_ref[...] = jnp.dot(xb, wwi_ref[...], preferred_element_type=F32) * (IDX_HEADS ** -0.5)
    p_ref[...] = jnp.dot(xb, wp_ref[...], preferred_element_type=F32)
    gate_ref[...] = jax.nn.sigmoid(jnp.dot(xb, wgl_ref[...], preferred_element_type=F32)).astype(BF16)


def _rope_tables(seq):
    inv_freq = ROPE_THETA ** (-jnp.arange(ROT_HALF, dtype=F32) / ROT_HALF)
    ang = jnp.arange(seq, dtype=F32)[:, None] * inv_freq[None, :]
    cos, sin = jnp.cos(ang), jnp.sin(ang)
    rest = HEAD_DIM - 2 * ROT_HALF
    ones = jnp.ones((seq, rest), F32)
    zeros = jnp.zeros((seq, rest), F32)
    zh = jnp.zeros((seq, ROT_HALF), F32)
    c = jnp.concatenate([cos, cos, ones], axis=1)
    s_lo = jnp.concatenate([-sin, zh, zeros], axis=1)
    s_hi = jnp.concatenate([zh, sin, zeros], axis=1)
    tile = lambda t: jnp.tile(t, (1, N_HEADS))
    return tile(c), tile(s_lo), tile(s_hi)


def _input_projection(x2, norm1_g, w_in, q_norm_g, k_norm_g, batch, seq, tm, kb):
    n, d = x2.shape
    w = ATTN_WIDTH
    o = np.cumsum([0, w, w, w, IDX_HEADS * IDX_DIM, IDX_DIM, IDX_HEADS, POOL_WIDTH, 2 * d])
    wb = w_in.astype(BF16)
    wqkv = wb[:, o[0]:o[3]]
    wqi = wb[:, o[3]:o[4]]
    wki = jnp.pad(wb[:, o[4]:o[5]], ((0, 0), (0, LANES - IDX_DIM)))
    wwi = jnp.pad(wb[:, o[5]:o[6]], ((0, 0), (0, LANES - IDX_HEADS)))
    wp = wb[:, o[6]:o[7]]
    wgl = wb[:, o[7]:o[8]]
    bd = jnp.kron(jnp.eye(N_HEADS, dtype=F32), jnp.full((HEAD_DIM, HEAD_DIM), 1.0 / HEAD_DIM, F32)).astype(BF16)
    c, s_lo, s_hi = _rope_tables(seq)
    tps = seq // tm
    const = lambda shape: pl.BlockSpec(shape, lambda i: (0,) * len(shape))
    row = lambda width: pl.BlockSpec((tm, width), lambda i: (i, 0))
    tab = pl.BlockSpec((tm, w), lambda i: (i % tps, 0))
    per_kb = kb // tm
    out_shapes = (
        jax.ShapeDtypeStruct((n, w), BF16),
        jax.ShapeDtypeStruct((n, w), BF16),
        jax.ShapeDtypeStruct((batch, seq // kb, w, kb), BF16),
        jax.ShapeDtypeStruct((n, w), BF16),
        jax.ShapeDtypeStruct((n, IDX_DIM), BF16),
        jax.ShapeDtypeStruct((n, LANES), F32),
        jax.ShapeDtypeStruct((n, POOL_WIDTH), F32),
        jax.ShapeDtypeStruct((n, 2 * d), BF16),
    )
    out_specs = (
        row(w), row(w),
        pl.BlockSpec((1, 1, w, tm), lambda i: (i // tps, (i % tps) // per_kb, 0, (i % tps) % per_kb)),
        row(w), row(IDX_DIM), row(LANES), row(POOL_WIDTH), row(2 * d),
    )
    return pl.pallas_call(
        _proj_body,
        grid=(n // tm,),
        in_specs=[row(d), const((1, d)), const(wqkv.shape), const(wqi.shape), const(wki.shape), const(wwi.shape),
                  const(wp.shape), const(wgl.shape), const((1, w)), const((1, w)), const(bd.shape), tab, tab, tab],
        out_specs=out_specs,
        out_shape=out_shapes,
        compiler_params=_params(("parallel",)),
        name="input_projection",
    )(x2, norm1_g.reshape(1, d), wqkv, wqi, wki, wwi, wp, wgl,
      jnp.tile(q_norm_g, N_HEADS).reshape(1, w), jnp.tile(k_norm_g, N_HEADS).reshape(1, w), bd, c, s_lo, s_hi)


def _sortable(v):
    b = lax.bitcast_convert_type(v, I32)
    b = jnp.where(b == INT_MIN, 0, b)
    return jnp.where(b < 0, b ^ 0x7FFFFFFF, b)


def _dsa_body(q_ref, qi_ref, wi_ref, k_ref, vt_ref, ki_ref, o_ref, key_s, bias_s, acc_s, s_s, *, seq, tq, kb, topk):
    j = pl.program_id(1)
    nblk = ((j + 1) * tq + kb - 1) // kb
    lane = lax.broadcasted_iota(I32, (1, tq), 1)
    lim = j * tq + (lane // CHUNK + 1) * CHUNK
    row_iota = lax.broadcasted_iota(I32, (kb, tq), 0)
    wi_t = wi_ref[...].T[:IDX_HEADS, :]
    qi = qi_ref[...]

    def rows(i):
        return pl.ds(pl.multiple_of(i * kb, kb), kb)

    def score_block(i, carry):
        kib = ki_ref[rows(i), :]
        acc = jnp.zeros((kb, tq), F32)
        for h in range(IDX_HEADS):
            lg = lax.dot_general(kib, qi[:, h * IDX_DIM:(h + 1) * IDX_DIM], NT_DIMS, preferred_element_type=F32)
            acc = acc + jnp.maximum(lg, 0.0) * wi_t[h:h + 1, :]
        key_s[rows(i), :] = jnp.where(i * kb + row_iota < lim, _sortable(acc), INT_MIN)
        return carry

    lax.fori_loop(0, nblk, score_block, 0)

    def count(pred):
        def body(i, c):
            m = pred(key_s[rows(i), :], i * kb + row_iota)
            return c + jnp.sum(m.astype(I32).reshape(kb // SUBLANES, SUBLANES, tq), axis=0)
        c8 = lax.fori_loop(0, nblk, body, jnp.zeros((SUBLANES, tq), I32))
        return jnp.sum(c8, axis=0, keepdims=True)

    def value_bit(it, tu):
        cand_u = tu | lax.shift_left(jnp.int32(1), 31 - it)
        cand_s = cand_u ^ INT_MIN
        return jnp.where(count(lambda blk, idx: blk >= cand_s) >= topk, cand_u, tu)

    thr = lax.fori_loop(0, 32, value_bit, jnp.zeros((1, tq), I32)) ^ INT_MIN
    idx_bits = int(seq).bit_length()
    surplus = (count(lambda blk, idx: blk >= thr) != topk) & (thr != INT_MIN)
    has_tie = jnp.max(jnp.where(surplus, 1.0, 0.0)) > 0.5

    def resolve_ties():
        need = topk - count(lambda blk, idx: blk > thr)

        def index_bit(it, jj):
            cand = jj | lax.shift_left(jnp.int32(1), idx_bits - 1 - it)
            return jnp.where(count(lambda blk, idx: (blk == thr) & (idx < cand)) <= need, cand, jj)

        return lax.fori_loop(0, idx_bits, index_bit, jnp.zeros((1, tq), I32))

    tie_end = lax.cond(has_tie, resolve_ties, lambda: jnp.full((1, tq), (1 << idx_bits) - 1, I32))

    def bias_block(i, carry):
        blk = key_s[rows(i), :]
        idx = i * kb + row_iota
        sel = ((blk > thr) | ((blk == thr) & (idx < tie_end))) & (idx < lim)
        bias_s[rows(i), :] = jnp.where(sel, 0.0, NEG)
        return carry

    lax.fori_loop(0, nblk, bias_block, 0)

    q = q_ref[...]
    pair_lane = lax.broadcasted_iota(I32, (tq, 2 * HEAD_DIM), 1)
    qm = []
    for h in range(N_HEADS):
        pair = q[:, (h // 2) * 2 * HEAD_DIM:(h // 2 + 1) * 2 * HEAD_DIM]
        qm.append(jnp.where((pair_lane // HEAD_DIM) == (h % 2), pair, jnp.zeros_like(pair)))
    acc_s[...] = jnp.zeros_like(acc_s)
    group = s_s.shape[0]

    def fold(t):
        return t.reshape(kb // SUBLANES, SUBLANES, tq)

    for g0 in range(0, N_HEADS, group):
        heads = range(g0, g0 + group)

        def score_pass(i, ms):
            bias = bias_s[rows(i), :]
            out = []
            for hh, h in enumerate(heads):
                kblk = k_ref[rows(i), (h // 2) * 2 * HEAD_DIM:(h // 2 + 1) * 2 * HEAD_DIM]
                s = lax.dot_general(kblk, qm[h], NT_DIMS, preferred_element_type=F32) + bias
                s_s[hh, rows(i), :] = s
                out.append(jnp.maximum(ms[hh], jnp.max(fold(s), axis=0)))
            return tuple(out)

        ms = lax.fori_loop(0, nblk, score_pass, tuple(jnp.full((SUBLANES, tq), NEG, F32) for _ in heads))
        mx = [jnp.max(m, axis=0, keepdims=True) for m in ms]

        def value_pass(i, ls):
            out = []
            for hh, h in enumerate(heads):
                hs = slice(h * HEAD_DIM, (h + 1) * HEAD_DIM)
                p = jnp.exp2(s_s[hh, rows(i), :] - mx[hh])
                out.append(ls[hh] + jnp.sum(fold(p), axis=0))
                acc_s[hs, :] += jnp.dot(vt_ref[0, i, hs, :], p.astype(BF16), preferred_element_type=F32)
            return tuple(out)

        ls = lax.fori_loop(0, nblk, value_pass, tuple(jnp.zeros((SUBLANES, tq), F32) for _ in heads))
        for hh, h in enumerate(heads):
            hs = slice(h * HEAD_DIM, (h + 1) * HEAD_DIM)
            acc_s[hs, :] = acc_s[hs, :] / jnp.sum(ls[hh], axis=0, keepdims=True)
    o_ref[...] = acc_s[...].T.astype(BF16)


def _dsa_attention(q, qi, wi, k, vt, ki, batch, seq, tq, kb):
    n, w = q.shape
    topk = min(TOPK_MAX, seq // 4)
    nq = seq // tq
    tile = lambda width: pl.BlockSpec((tq, width), lambda b, j: (b * nq + j, 0))
    whole = lambda width: pl.BlockSpec((seq, width), lambda b, j: (b, 0))
    return pl.pallas_call(
        functools.partial(_dsa_body, seq=seq, tq=tq, kb=kb, topk=topk),
        grid=(batch, nq),
        in_specs=[tile(w), tile(w), tile(LANES), whole(w),
                  pl.BlockSpec((1, seq // kb, w, kb), lambda b, j: (b, 0, 0, 0)), whole(IDX_DIM)],
        out_specs=tile(w),
        out_shape=jax.ShapeDtypeStruct((n, w), BF16),
        scratch_shapes=[pltpu.VMEM((seq, tq), I32), pltpu.VMEM((seq, tq), F32), pltpu.VMEM((w, tq), F32),
                        pltpu.VMEM((ATTN_HEAD_GROUP, seq, tq), F32)],
        compiler_params=_params(("parallel", "arbitrary")),
        name="dsa_attention",
    )(q, qi, wi, k, vt, ki)


def _mix_body(attn_ref, p_ref, halo_ref, gate_ref, x_ref, wa_ref, wpb_ref, wo_ref, pw_ref, ps_ref, g2_ref,
              h_ref, hn_ref, ext_s, *, tm, tps):
    st = pl.program_id(0) % tps
    ext_s[0:POOL_HALO, :] = jnp.where(st == 0, 0.0, halo_ref[...])
    ext_s[POOL_HALO:POOL_HALO + tm, :] = p_ref[...]
    t1 = (st * tm + 1 + lax.broadcasted_iota(I32, (tm, POOL_GROUP_DIM), 0)).astype(F32)
    mixed = []
    for g, win in enumerate(POOL_WINDOWS):
        ls = slice(g * POOL_GROUP_DIM, (g + 1) * POOL_GROUP_DIM)
        frame = ext_s[POOL_HALO:POOL_HALO + tm, ls]
        tot = frame
        for dlt in range(1, win):
            tot = tot + ext_s[POOL_HALO - dlt:POOL_HALO - dlt + tm, ls]
        pooled = tot / jnp.minimum(t1, float(win)) - frame
        mixed.append(jnp.dot(pooled.astype(BF16), pw_ref[g], preferred_element_type=F32))
    mixed = jnp.concatenate(mixed, axis=1) * ps_ref[...]
    y_pool = jnp.dot(mixed.astype(BF16), wpb_ref[...], preferred_element_type=F32)
    y_attn = jnp.dot(attn_ref[...], wa_ref[...], preferred_element_type=F32)
    d = y_attn.shape[1]
    gate = gate_ref[...].astype(F32)
    z = gate[:, :d] * y_attn + gate[:, d:] * y_pool
    h = x_ref[...] + jnp.dot(z.astype(BF16), wo_ref[...], preferred_element_type=F32)
    h_ref[...] = h
    hn = h * lax.rsqrt(jnp.mean(h * h, axis=-1, keepdims=True) + EPS) * g2_ref[...]
    hn_ref[...] = hn.astype(BF16)


def _mixer_output(attn, p, gate, x2, w_branch_attn, w_branch_pool, w_out, pool_w, pool_scale, norm2_g, seq, tm):
    n, d = x2.shape
    tps = seq // tm
    hb = tm // POOL_HALO
    const = lambda shape: pl.BlockSpec(shape, lambda i: (0,) * len(shape))
    row = lambda width: pl.BlockSpec((tm, width), lambda i: (i, 0))
    return pl.pallas_call(
        functools.partial(_mix_body, tm=tm, tps=tps),
        grid=(n // tm,),
        in_specs=[row(ATTN_WIDTH), row(POOL_WIDTH),
                  pl.BlockSpec((POOL_HALO, POOL_WIDTH), lambda i: (jnp.maximum(i * hb - 1, 0), 0)),
                  row(2 * d), row(d), const((ATTN_WIDTH, d)), const((POOL_WIDTH, d)), const((d, d)),
                  const(pool_w.shape), const((1, POOL_WIDTH)), const((1, d))],
        out_specs=(row(d), row(d)),
        out_shape=(jax.ShapeDtypeStruct((n, d), F32), jax.ShapeDtypeStruct((n, d), BF16)),
        scratch_shapes=[pltpu.VMEM((POOL_HALO + tm, POOL_WIDTH), F32)],
        compiler_params=_params(("parallel",)),
        name="mixer_output",
    )(attn, p, p, gate, x2, w_branch_attn.astype(BF16), w_branch_pool.astype(BF16), w_out.astype(BF16),
      pool_w.astype(BF16), pool_scale.reshape(1, POOL_WIDTH), norm2_g.reshape(1, d))


def _candidate_pairs():
    return [(a, b) for a in range(PEER_TOPK) for b in range(PEER_TOPK) if (a + 1) * (b + 1) <= PEER_TOPK]


def _route_body(hn_ref, wqt_ref, kbig_ref, a0_ref, l0_ref, b1_ref, r1_ref, vals_s, rank_s, ex_s, topv_s, tope_s,
                *, tr):
    nk, nh = PEER_KEYS, PEER_HEADS
    half_rows = nh * PEER_KEY_DIM
    qt = lax.dot_general(wqt_ref[...], hn_ref[...], NT_DIMS, preferred_element_type=F32).astype(BF16)
    for p in range(2):
        sub = jnp.dot(kbig_ref[p], qt[p * half_rows:(p + 1) * half_rows], preferred_element_type=F32)
        vals_s[p] = sub.reshape(nk, nh, tr)
    rank_s[...] = jnp.full(rank_s.shape, float(PEER_TOPK), F32)
    ex_s[...] = jnp.zeros(ex_s.shape, F32)
    key_iota = lax.broadcasted_iota(I32, (nk, nh, tr), 0)

    def extract(kk, first):
        new_first = []
        for p in range(2):
            v = vals_s[p]
            m = jnp.max(v, axis=0)
            idx = jnp.min(jnp.where(v == m[None], key_iota, nk), axis=0)
            hit = key_iota == idx[None]
            vals_s[p] = jnp.where(hit, -jnp.inf, v)
            top = jnp.where(kk == 0, m, first[p])
            e = jnp.exp(m - top)
            rank_s[p] = jnp.where(hit, kk.astype(F32), rank_s[p])
            ex_s[p] = jnp.where(hit, e[None], ex_s[p])
            topv_s[p, kk] = m
            tope_s[p, kk] = e
            new_first.append(top)
        return tuple(new_first)

    zero = jnp.zeros((nh, tr), F32)
    lax.fori_loop(0, PEER_TOPK, extract, (zero, zero))

    pairs = _candidate_pairs()
    v0 = [topv_s[0, a] for a in range(PEER_TOPK)]
    v1 = [topv_s[1, b] for b in range(PEER_TOPK)]
    cand = [v0[a] + v1[b] for a, b in pairs]
    rank = [jnp.zeros((nh, tr), F32) for _ in pairs]
    for ia, (a0, a1) in enumerate(pairs):
        for ib in range(ia + 1, len(pairs)):
            b0, b1 = pairs[ib]
            if a0 <= b0 and a1 <= b1:
                rank[ib] = rank[ib] + 1.0
            else:
                wins = jnp.where(cand[ia] >= cand[ib], 1.0, 0.0)
                rank[ib] = rank[ib] + wins
                rank[ia] = rank[ia] + (1.0 - wins)
    e0 = [tope_s[0, a] for a in range(PEER_TOPK)]
    e1 = [tope_s[1, b] for b in range(PEER_TOPK)]
    width = [jnp.zeros((nh, tr), F32) for _ in range(PEER_TOPK)]
    z = jnp.zeros((nh, tr), F32)
    for ic, (a, b) in enumerate(pairs):
        sel = jnp.where(rank[ic] < float(PEER_TOPK), 1.0, 0.0)
        width[a] = width[a] + sel
        z = z + sel * (e0[a] * e1[b])
    inv_z = 1.0 / z
    r0 = rank_s[0]
    l0 = jnp.zeros((nk, nh, tr), F32)
    for a in range(PEER_TOPK):
        l0 = jnp.where(r0 == float(a), width[a][None], l0)
    a0_ref[0] = (ex_s[0] * inv_z[None]).reshape(nk * nh, tr)
    l0_ref[0] = l0.reshape(nk * nh, tr)
    b1_ref[0] = ex_s[1].reshape(nk * nh, tr)
    r1_ref[0] = rank_s[1].reshape(nk * nh, tr)


def _peer_routing(hn, peer_wq, peer_subkeys, tr):
    n, d = hn.shape
    nk, nh, kd = PEER_KEYS, PEER_HEADS, PEER_KEY_DIM
    wqt = peer_wq.reshape(d, nh, 2, kd).transpose(2, 1, 3, 0).reshape(2 * nh * kd, d).astype(BF16)
    eye = jnp.eye(nh, dtype=peer_subkeys.dtype)
    kbig = jnp.einsum("hpnd,hg->pnhgd", peer_subkeys, eye).reshape(2, nk * nh, nh * kd).astype(BF16)
    rows = nk * nh
    assert tr == LANES
    out = jax.ShapeDtypeStruct((n // tr, rows, tr), F32)
    spec = pl.BlockSpec((1, rows, tr), lambda i: (i, 0, 0))
    return pl.pallas_call(
        functools.partial(_route_body, tr=tr),
        grid=(n // tr,),
        in_specs=[pl.BlockSpec((tr, d), lambda i: (i, 0)),
                  pl.BlockSpec(wqt.shape, lambda i: (0, 0)),
                  pl.BlockSpec(kbig.shape, lambda i: (0, 0, 0))],
        out_specs=(spec, spec, spec, spec),
        out_shape=(out, out, out, out),
        scratch_shapes=[pltpu.VMEM((2, nk, nh, tr), F32), pltpu.VMEM((2, nk, nh, tr), F32),
                        pltpu.VMEM((2, nk, nh, tr), F32), pltpu.VMEM((2, PEER_TOPK, nh, tr), F32),
                        pltpu.VMEM((2, PEER_TOPK, nh, tr), F32)],
        compiler_params=_params(("parallel",)),
        name="peer_routing",
    )(hn, wqt, kbig)


def _expert_body(hn_ref, h_ref, u_ref, vt_ref, a0_ref, l0_ref, b1_ref, r1_ref, y_ref,
                 acc_s, g0_s, g1_s, ga0_s, ga1_s, b1_s, r1_s, *, tm, te, n_eb):
    step = pl.program_id(1)
    nk, nh = PEER_KEYS, PEER_HEADS
    gt = (g0_s, g1_s)
    ga = (ga0_s, ga1_s)
    pack = 2 * SUBLANES

    def project():
        return lax.dot_general(u_ref[...], hn_ref[...], NT_DIMS, preferred_element_type=F32)

    def finish(act, src, dst):
        act = 0.5 * act * (1.0 + lax.erf(act * (2.0 ** -0.5)))
        dst[...] = src[...] * act.astype(BF16)

    def gate(dst):
        for il in range(te // nk):
            i = step * (te // nk) + il
            for c in range(tm // LANES):
                g = jnp.zeros((nk, LANES), BF16)
                for h in range(nh):
                    row = pl.ds(i * nh + h, 1)
                    a_row = jnp.broadcast_to(a0_ref[c, row, :], (pack, LANES)).astype(BF16)
                    l_row = jnp.broadcast_to(l0_ref[c, row, :], (pack, LANES)).astype(BF16)
                    a_row = jnp.tile(a_row, (nk // pack, 1))
                    l_row = jnp.tile(l_row, (nk // pack, 1))
                    g = g + a_row * jnp.where(r1_s[c, h] < l_row, b1_s[c, h], jnp.zeros((), BF16))
                blk = (slice(il * nk, (il + 1) * nk), slice(c * LANES, (c + 1) * LANES))
                dst[blk] = g

    def apply(src):
        acc_s[...] += jnp.dot(vt_ref[...], src[...], preferred_element_type=F32)

    @pl.when(step == 0)
    def _():
        acc_s[...] = jnp.zeros_like(acc_s)
        for c in range(tm // LANES):
            for h in range(nh):
                b1_s[c, h] = b1_ref[c, pl.ds(h, nk, stride=nh), :].astype(BF16)
                r1_s[c, h] = r1_ref[c, pl.ds(h, nk, stride=nh), :].astype(BF16)
        gate(gt[0])

    @pl.when(step == 1)
    def _():
        finish(project(), gt[0], ga[0])
        gate(gt[1])

    for p in range(2):
        @pl.when((step >= 2) & (step < n_eb) & (step % 2 == p))
        def _():
            gate(gt[p])
            act = project()
            apply(ga[p])
            finish(act, gt[1 - p], ga[1 - p])

    @pl.when(step == n_eb)
    def _():
        p = n_eb % 2
        act = project()
        apply(ga[p])
        finish(act, gt[1 - p], ga[1 - p])

    @pl.when(step == n_eb + 1)
    def _():
        apply(ga[(n_eb + 1) % 2])
        y_ref[...] = h_ref[...] + acc_s[...].T


def _peer_experts(hn, h, peer_u, peer_v, a0, l0, b1, r1, tm, te):
    n, d = hn.shape
    ne = peer_u.shape[0]
    rows = a0.shape[1]
    tok = pl.BlockSpec((tm // LANES, rows, LANES), lambda t, e: (t, 0, 0))
    n_eb = ne // te
    return pl.pallas_call(
        functools.partial(_expert_body, tm=tm, te=te, n_eb=n_eb),
        grid=(n // tm, n_eb + 2),
        in_specs=[pl.BlockSpec((tm, d), lambda t, s: (t, 0)), pl.BlockSpec((tm, d), lambda t, s: (t, 0)),
                  pl.BlockSpec((te, d), lambda t, s: (jnp.clip(s - 1, 0, n_eb - 1), 0)),
                  pl.BlockSpec((d, te), lambda t, s: (0, jnp.clip(s - 2, 0, n_eb - 1))),
                  tok, tok, tok, tok],
        out_specs=pl.BlockSpec((tm, d), lambda t, s: (t, 0)),
        out_shape=jax.ShapeDtypeStruct((n, d), F32),
        scratch_shapes=[pltpu.VMEM((d, tm), F32), pltpu.VMEM((te, tm), BF16), pltpu.VMEM((te, tm), BF16),
                        pltpu.VMEM((te, tm), BF16), pltpu.VMEM((te, tm), BF16),
                        pltpu.VMEM((tm // LANES, PEER_HEADS, PEER_KEYS, LANES), BF16),
                        pltpu.VMEM((tm // LANES, PEER_HEADS, PEER_KEYS, LANES), BF16)],
        compiler_params=_params(("parallel", "arbitrary")),
        name="peer_experts",
    )(hn, h, peer_u.astype(BF16), peer_v.T.astype(BF16), a0, l0, b1, r1)


def _tiles(batch, seq):
    return dict(tm=256, tq=128, kb=512, tr=128, te_tm=512, te=1024)


def kernel(x, norm1_g, w_in, q_norm_g, k_norm_g, pool_w, pool_scale, w_branch_attn, w_branch_pool, w_out, norm2_g,
           peer_wq, peer_subkeys, peer_u, peer_v):
    batch, seq, d = x.shape
    t = _tiles(batch, seq)
    x2 = x.reshape(batch * seq, d)
    for l in range(norm1_g.shape[0]):
        q, k, vt, qi, ki, wi, p, gate = _input_projection(
            x2, norm1_g[l], w_in[l], q_norm_g[l], k_norm_g[l], batch, seq, t["tm"], t["kb"])
        attn = _dsa_attention(q, qi, wi, k, vt, ki, batch, seq, t["tq"], t["kb"])
        h, hn = _mixer_output(attn, p, gate, x2, w_branch_attn[l], w_branch_pool[l], w_out[l], pool_w[l],
                              pool_scale[l], norm2_g[l], seq, t["tm"])
        a0, l0, b1, r1 = _peer_routing(hn, peer_wq[l], peer_subkeys[l], t["tr"])
        x2 = _peer_experts(hn, h, peer_u[l], peer_v[l], a0, l0, b1, r1, t["te_tm"], t["te"])
    return x2.reshape(batch, seq, d)
```

```python
import functools

import jax
import jax.numpy as jnp
import numpy as np
from jax import lax
from jax.experimental import pallas as pl
from jax.experimental.pallas import tpu as pltpu

CHUNK = 64
EPS = 1e-6
N_HEADS = 8
HEAD_DIM = 64
ATTN_WIDTH = N_HEADS * HEAD_DIM
ROT_HALF = HEAD_DIM // 8
ROPE_THETA = 500000.0
IDX_HEADS = 8
IDX_DIM = 64
TOPK_MAX = 256
POOL_WINDOWS = (2, 4, 8, 16)
POOL_WIDTH = 512
POOL_GROUP_DIM = POOL_WIDTH // len(POOL_WINDOWS)
POOL_HALO = 16
PEER_HEADS = 8
PEER_KEYS = 128
PEER_KEY_DIM = 64
PEER_TOPK = 16

LANES = 128
SUBLANES = 8
VMEM_LIMIT = 56 * 1024 * 1024

F32 = jnp.float32
BF16 = jnp.bfloat16
I32 = jnp.int32
INT_MIN = -2147483648
NEG = -1e30
LOG2_E = 1.4426950408889634
ATTN_HEAD_GROUP = 4
EXPERT_SLICE = 256
NT_DIMS = (((1,), (1,)), ((), ()))


def _params(sem):
    return pltpu.CompilerParams(dimension_semantics=sem, vmem_limit_bytes=VMEM_LIMIT)


def _rope(t, c, s_lo, s_hi):
    w = t.shape[-1]
    return t * c + pltpu.roll(t, w - ROT_HALF, 1) * s_lo + pltpu.roll(t, ROT_HALF, 1) * s_hi


def _proj_body(x_ref, g1_ref, wqkv_ref, wqi_ref, wki_ref, wwi_ref, wp_ref, wgl_ref, qg_ref, kg_ref, bd_ref,
               c_ref, slo_ref, shi_ref,
               q_ref, k_ref, vt_ref, qi_ref, ki_ref, wi_ref, p_ref, gate_ref):
    x = x_ref[...]
    xn = x * lax.rsqrt(jnp.mean(x * x, axis=-1, keepdims=True) + EPS) * g1_ref[...]
    xb = xn.astype(BF16)
    c, s_lo, s_hi = c_ref[...], slo_ref[...], shi_ref[...]

    def head_norm(t, g):
        ms = jnp.dot((t * t).astype(BF16), bd_ref[...], preferred_element_type=F32)
        return t * lax.rsqrt(ms + EPS) * g

    qkv = jnp.dot(xb, wqkv_ref[...], preferred_element_type=F32)
    w = ATTN_WIDTH
    q = _rope(head_norm(qkv[:, :w], qg_ref[...]), c, s_lo, s_hi) * (HEAD_DIM ** -0.5 * LOG2_E)
    k = _rope(head_norm(qkv[:, w:2 * w], kg_ref[...]), c, s_lo, s_hi)
    q_ref[...] = q.astype(BF16)
    k_ref[...] = k.astype(BF16)
    vt_ref[0, 0] = qkv[:, 2 * w:].T.astype(BF16)
    qi = jnp.dot(xb, wqi_ref[...], preferred_element_type=F32)
    qi_ref[...] = (_rope(qi, c, s_lo, s_hi) * (IDX_DIM ** -0.5)).astype(BF16)
    ki = jnp.dot(xb, wki_ref[...], preferred_element_type=F32)
    ki = _rope(ki, c[:, :LANES], s_lo[:, :LANES], s_hi[:, :LANES])
    ki_ref[...] = ki[:, :IDX_DIM].astype(BF16)
    wi_ref[...] = jnp.dot(xb, wwi_ref[...], preferred_element_type=F32) * (IDX_HEADS ** -0.5)
    p_ref[...] = jnp.dot(xb, wp_ref[...], preferred_element_type=F32)
    gate_ref[...] = jax.nn.sigmoid(jnp.dot(xb, wgl_ref[...], preferred_element_type=F32)).astype(BF16)


def _rope_tables(seq):
    inv_freq = ROPE_THETA ** (-jnp.arange(ROT_HALF, dtype=F32) / ROT_HALF)
    ang = jnp.arange(seq, dtype=F32)[:, None] * inv_freq[None, :]
    cos, sin = jnp.cos(ang), jnp.sin(ang)
    rest = HEAD_DIM - 2 * ROT_HALF
    ones = jnp.ones((seq, rest), F32)
    zeros = jnp.zeros((seq, rest), F32)
    zh = jnp.zeros((seq, ROT_HALF), F32)
    c = jnp.concatenate([cos, cos, ones], axis=1)
    s_lo = jnp.concatenate([-sin, zh, zeros], axis=1)
    s_hi = jnp.concatenate([zh, sin, zeros], axis=1)
    tile = lambda t: jnp.tile(t, (1, N_HEADS))
    return tile(c), tile(s_lo), tile(s_hi)


def _input_projection(x2, norm1_g, w_in, q_norm_g, k_norm_g, batch, seq, tm, kb):
    n, d = x2.shape
    w = ATTN_WIDTH
    o = np.cumsum([0, w, w, w, IDX_HEADS * IDX_DIM, IDX_DIM, IDX_HEADS, POOL_WIDTH, 2 * d])
    wb = w_in.astype(BF16)
    wqkv = wb[:, o[0]:o[3]]
    wqi = wb[:, o[3]:o[4]]
    wki = jnp.pad(wb[:, o[4]:o[5]], ((0, 0), (0, LANES - IDX_DIM)))
    wwi = jnp.pad(wb[:, o[5]:o[6]], ((0, 0), (0, LANES - IDX_HEADS)))
    wp = wb[:, o[6]:o[7]]
    wgl = wb[:, o[7]:o[8]]
    bd = jnp.kron(jnp.eye(N_HEADS, dtype=F32), jnp.full((HEAD_DIM, HEAD_DIM), 1.0 / HEAD_DIM, F32)).astype(BF16)
    c, s_lo, s_hi = _rope_tables(seq)
    tps = seq // tm
    const = lambda shape: pl.BlockSpec(shape, lambda i: (0,) * len(shape))
    row = lambda width: pl.BlockSpec((tm, width), lambda i: (i, 0))
    tab = pl.BlockSpec((tm, w), lambda i: (i % tps, 0))
    per_kb = kb // tm
    out_shapes = (
        jax.ShapeDtypeStruct((n, w), BF16),
        jax.ShapeDtypeStruct((n, w), BF16),
        jax.ShapeDtypeStruct((batch, seq // kb, w, kb), BF16),
        jax.ShapeDtypeStruct((n, w), BF16),
        jax.ShapeDtypeStruct((n, IDX_DIM), BF16),
        jax.ShapeDtypeStruct((n, LANES), F32),
        jax.ShapeDtypeStruct((n, POOL_WIDTH), F32),
        jax.ShapeDtypeStruct((n, 2 * d), BF16),
    )
    out_specs = (
        row(w), row(w),
        pl.BlockSpec((1, 1, w, tm), lambda i: (i // tps, (i % tps) // per_kb, 0, (i % tps) % per_kb)),
        row(w), row(IDX_DIM), row(LANES), row(POOL_WIDTH), row(2 * d),
    )
    return pl.pallas_call(
        _proj_body,
        grid=(n // tm,),
        in_specs=[row(d), const((1, d)), const(wqkv.shape), const(wqi.shape), const(wki.shape), const(wwi.shape),
                  const(wp.shape), const(wgl.shape), const((1, w)), const((1, w)), const(bd.shape), tab, tab, tab],
        out_specs=out_specs,
        out_shape=out_shapes,
        compiler_params=_params(("parallel",)),
        name="input_projection",
    )(x2, norm1_g.reshape(1, d), wqkv, wqi, wki, wwi, wp, wgl,
      jnp.tile(q_norm_g, N_HEADS).reshape(1, w), jnp.tile(k_norm_g, N_HEADS).reshape(1, w), bd, c, s_lo, s_hi)


def _sortable(v):
    b = lax.bitcast_convert_type(v, I32)
    b = jnp.where(b == INT_MIN, 0, b)
    return jnp.where(b < 0, b ^ 0x7FFFFFFF, b)


def _dsa_body(q_ref, qi_ref, wi_ref, k_ref, vt_ref, ki_ref, o_ref, key_s, bias_s, acc_s, s_s, *, seq, tq, kb, topk):
    j = pl.program_id(1)
    nblk = ((j + 1) * tq + kb - 1) // kb
    lane = lax.broadcasted_iota(I32, (1, tq), 1)
    lim = j * tq + (lane // CHUNK + 1) * CHUNK
    row_iota = lax.broadcasted_iota(I32, (kb, tq), 0)
    wi_t = wi_ref[...].T[:IDX_HEADS, :]
    qi = qi_ref[...]

    def rows(i):
        return pl.ds(pl.multiple_of(i * kb, kb), kb)

    def score_block(i, carry):
        kib = ki_ref[rows(i), :]
        acc = jnp.zeros((kb, tq), F32)
        for h in range(IDX_HEADS):
            lg = lax.dot_general(kib, qi[:, h * IDX_DIM:(h + 1) * IDX_DIM], NT_DIMS, preferred_element_type=F32)
            acc = acc + jnp.maximum(lg, 0.0) * wi_t[h:h + 1, :]
        key_s[rows(i), :] = jnp.where(i * kb + row_iota < lim, _sortable(acc), INT_MIN)
        return carry

    lax.fori_loop(0, nblk, score_block, 0)

    def count(pred):
        def body(i, c):
            m = pred(key_s[rows(i), :], i * kb + row_iota)
            return c + jnp.sum(m.astype(I32).reshape(kb // SUBLANES, SUBLANES, tq), axis=0)
        c8 = lax.fori_loop(0, nblk, body, jnp.zeros((SUBLANES, tq), I32))
        return jnp.sum(c8, axis=0, keepdims=True)

    def value_bit(it, tu):
        cand_u = tu | lax.shift_left(jnp.int32(1), 31 - it)
        cand_s = cand_u ^ INT_MIN
        return jnp.where(count(lambda blk, idx: blk >= cand_s) >= topk, cand_u, tu)

    thr = lax.fori_loop(0, 32, value_bit, jnp.zeros((1, tq), I32)) ^ INT_MIN
    idx_bits = int(seq).bit_length()
    surplus = (count(lambda blk, idx: blk >= thr) != topk) & (thr != INT_MIN)
    has_tie = jnp.max(jnp.where(surplus, 1.0, 0.0)) > 0.5

    def resolve_ties():
        need = topk - count(lambda blk, idx: blk > thr)

        def index_bit(it, jj):
            cand = jj | lax.shift_left(jnp.int32(1), idx_bits - 1 - it)
            return jnp.where(count(lambda blk, idx: (blk == thr) & (idx < cand)) <= need, cand, jj)

        return lax.fori_loop(0, idx_bits, index_bit, jnp.zeros((1, tq), I32))

    tie_end = lax.cond(has_tie, resolve_ties, lambda: jnp.full((1, tq), (1 << idx_bits) - 1, I32))

    def bias_block(i, carry):
        blk = key_s[rows(i), :]
        idx = i * kb + row_iota
        sel = ((blk > thr) | ((blk == thr) & (idx < tie_end))) & (idx < lim)
        bias_s[rows(i), :] = jnp.where(sel, 0.0, NEG)
        return carry

    lax.fori_loop(0, nblk, bias_block, 0)

    q = q_ref[...]
    pair_lane = lax.broadcasted_iota(I32, (tq, 2 * HEAD_DIM), 1)
    qm = []
    for h in range(N_HEADS):
        pair = q[:, (h // 2) * 2 * HEAD_DIM:(h // 2 + 1) * 2 * HEAD_DIM]
        qm.append(jnp.where((pair_lane // HEAD_DIM) == (h % 2), pair, jnp.zeros_like(pair)))
    acc_s[...] = jnp.zeros_like(acc_s)
    group = s_s.shape[0]

    def fold(t):
        return t.reshape(kb // SUBLANES, SUBLANES, tq)

    for g0 in range(0, N_HEADS, group):
        heads = range(g0, g0 + group)

        def score_pass(i, ms):
            bias = bias_s[rows(i), :]
            out = []
            for hh, h in enumerate(heads):
                kblk = k_ref[rows(i), (h // 2) * 2 * HEAD_DIM:(h // 2 + 1) * 2 * HEAD_DIM]
                s = lax.dot_general(kblk, qm[h], NT_DIMS, preferred_element_type=F32) + bias
                s_s[hh, rows(i), :] = s
                out.append(jnp.maximum(ms[hh], jnp.max(fold(s), axis=0)))
            return tuple(out)

        ms = lax.fori_loop(0, nblk, score_pass, tuple(jnp.full((SUBLANES, tq), NEG, F32) for _ in heads))
        mx = [jnp.max(m, axis=0, keepdims=True) for m in ms]

        def value_pass(i, ls):
            out = []
            for hh, h in enumerate(heads):
                hs = slice(h * HEAD_DIM, (h + 1) * HEAD_DIM)
                p = jnp.exp2(s_s[hh, rows(i), :] - mx[hh])
                out.append(ls[hh] + jnp.sum(fold(p), axis=0))
                acc_s[hs, :] += jnp.dot(vt_ref[0, i, hs, :], p.astype(BF16), preferred_element_type=F32)
            return tuple(out)

        ls = lax.fori_loop(0, nblk, value_pass, tuple(jnp.zeros((SUBLANES, tq), F32) for _ in heads))
        for hh, h in enumerate(heads):
            hs = slice(h * HEAD_DIM, (h + 1) * HEAD_DIM)
            acc_s[hs, :] = acc_s[hs, :] / jnp.sum(ls[hh], axis=0, keepdims=True)
    o_ref[...] = acc_s[...].T.astype(BF16)


def _dsa_attention(q, qi, wi, k, vt, ki, batch, seq, tq, kb):
    n, w = q.shape
    topk = min(TOPK_MAX, seq // 4)
    nq = seq // tq
    tile = lambda width: pl.BlockSpec((tq, width), lambda b, j: (b * nq + j, 0))
    whole = lambda width: pl.BlockSpec((seq, width), lambda b, j: (b, 0))
    return pl.pallas_call(
        functools.partial(_dsa_body, seq=seq, tq=tq, kb=kb, topk=topk),
        grid=(batch, nq),
        in_specs=[tile(w), tile(w), tile(LANES), whole(w),
                  pl.BlockSpec((1, seq // kb, w, kb), lambda b, j: (b, 0, 0, 0)), whole(IDX_DIM)],
        out_specs=tile(w),
        out_shape=jax.ShapeDtypeStruct((n, w), BF16),
        scratch_shapes=[pltpu.VMEM((seq, tq), I32), pltpu.VMEM((seq, tq), F32), pltpu.VMEM((w, tq), F32),
                        pltpu.VMEM((ATTN_HEAD_GROUP, seq, tq), F32)],
        compiler_params=_params(("parallel", "arbitrary")),
        name="dsa_attention",
    )(q, qi, wi, k, vt, ki)


def _mix_body(attn_ref, p_ref, halo_ref, gate_ref, x_ref, wa_ref, wpb_ref, wo_ref, pw_ref, ps_ref, g2_ref,
              h_ref, hn_ref, ext_s, *, tm, tps):
    st = pl.program_id(0) % tps
    ext_s[0:POOL_HALO, :] = jnp.where(st == 0, 0.0, halo_ref[...])
    ext_s[POOL_HALO:POOL_HALO + tm, :] = p_ref[...]
    t1 = (st * tm + 1 + lax.broadcasted_iota(I32, (tm, POOL_GROUP_DIM), 0)).astype(F32)
    mixed = []
    for g, win in enumerate(POOL_WINDOWS):
        ls = slice(g * POOL_GROUP_DIM, (g + 1) * POOL_GROUP_DIM)
        frame = ext_s[POOL_HALO:POOL_HALO + tm, ls]
        tot = frame
        for dlt in range(1, win):
            tot = tot + ext_s[POOL_HALO - dlt:POOL_HALO - dlt + tm, ls]
        pooled = tot / jnp.minimum(t1, float(win)) - frame
        mixed.append(jnp.dot(pooled.astype(BF16), pw_ref[g], preferred_element_type=F32))
    mixed = jnp.concatenate(mixed, axis=1) * ps_ref[...]
    y_pool = jnp.dot(mixed.astype(BF16), wpb_ref[...], preferred_element_type=F32)
    y_attn = jnp.dot(attn_ref[...], wa_ref[...], preferred_element_type=F32)
    d = y_attn.shape[1]
    gate = gate_ref[...].astype(F32)
    z = gate[:, :d] * y_attn + gate[:, d:] * y_pool
    h = x_ref[...] + jnp.dot(z.astype(BF16), wo_ref[...], preferred_element_type=F32)
    h_ref[...] = h
    hn = h * lax.rsqrt(jnp.mean(h * h, axis=-1, keepdims=True) + EPS) * g2_ref[...]
    hn_ref[...] = hn.astype(BF16)


def _mixer_output(attn, p, gate, x2, w_branch_attn, w_branch_pool, w_out, pool_w, pool_scale, norm2_g, seq, tm):
    n, d = x2.shape
    tps = seq // tm
    hb = tm // POOL_HALO
    const = lambda shape: pl.BlockSpec(shape, lambda i: (0,) * len(shape))
    row = lambda width: pl.BlockSpec((tm, width), lambda i: (i, 0))
    return pl.pallas_call(
        functools.partial(_mix_body, tm=tm, tps=tps),
        grid=(n // tm,),
        in_specs=[row(ATTN_WIDTH), row(POOL_WIDTH),
                  pl.BlockSpec((POOL_HALO, POOL_WIDTH), lambda i: (jnp.maximum(i * hb - 1, 0), 0)),
                  row(2 * d), row(d), const((ATTN_WIDTH, d)), const((POOL_WIDTH, d)), const((d, d)),
                  const(pool_w.shape), const((1, POOL_WIDTH)), const((1, d))],
        out_specs=(row(d), row(d)),
        out_shape=(jax.ShapeDtypeStruct((n, d), F32), jax.ShapeDtypeStruct((n, d), BF16)),
        scratch_shapes=[pltpu.VMEM((POOL_HALO + tm, POOL_WIDTH), F32)],
        compiler_params=_params(("parallel",)),
        name="mixer_output",
    )(attn, p, p, gate, x2, w_branch_attn.astype(BF16), w_branch_pool.astype(BF16), w_out.astype(BF16),
      pool_w.astype(BF16), pool_scale.reshape(1, POOL_WIDTH), norm2_g.reshape(1, d))


def _candidate_pairs():
    return [(a, b) for a in range(PEER_TOPK) for b in range(PEER_TOPK) if (a + 1) * (b + 1) <= PEER_TOPK]


def _sort_desc(v):
    v = list(v)
    n = len(v)
    k = 2
    while k <= n:
        j = k // 2
        while j >= 1:
            for i in range(n):
                m = i ^ j
                if m > i:
                    hi, lo = jnp.maximum(v[i], v[m]), jnp.minimum(v[i], v[m])
                    v[i], v[m] = (hi, lo) if (i & k) == 0 else (lo, hi)
            j //= 2
        k *= 2
    return v


def _merge_top(a, b):
    n = len(a)
    c = [jnp.maximum(a[i], b[n - 1 - i]) for i in range(n)]
    j = n // 2
    while j >= 1:
        for i in range(n):
            m = i ^ j
            if m > i:
                c[i], c[m] = jnp.maximum(c[i], c[m]), jnp.minimum(c[i], c[m])
        j //= 2
    return c


def _route_body(hn_ref, wqt_ref, kbig_ref, a0_ref, l0_ref, b1_ref, r1_ref, vals_s, rank_s, ex_s, *, tr):
    nk, nh, kt = PEER_KEYS, PEER_HEADS, PEER_TOPK
    half_rows = nh * PEER_KEY_DIM
    qt = lax.dot_general(wqt_ref[...], hn_ref[...], NT_DIMS, preferred_element_type=F32).astype(BF16)
    for p in range(2):
        sub = jnp.dot(kbig_ref[p], qt[p * half_rows:(p + 1) * half_rows], preferred_element_type=F32)
        vals_s[p] = sub.reshape(nk, nh, tr)

    def best(p, lo, hi):
        if hi - lo == kt:
            return _sort_desc([vals_s[p, i] for i in range(lo, hi)])
        mid = (lo + hi) // 2
        return _merge_top(best(p, lo, mid), best(p, mid, hi))

    tops = [best(p, 0, nk) for p in range(2)]
    v0, v1 = tops
    tied = jnp.zeros((nh, tr), F32)
    for p in range(2):
        for a in range(kt - 1):
            tied = jnp.maximum(tied, jnp.where(tops[p][a] == tops[p][a + 1], 1.0, 0.0))
        above = [jnp.where(vals_s[p, i] >= tops[p][kt - 1], 1.0, 0.0) for i in range(nk)]
        while len(above) > 1:
            above = [above[i] + above[i + 1] for i in range(0, len(above), 2)]
        tied = jnp.maximum(tied, jnp.where(above[0] != float(kt), 1.0, 0.0))
    has_tie = jnp.max(tied) > 0.5

    pairs = _candidate_pairs()
    cand = [v0[a] + v1[b] for a, b in pairs]
    rank = [jnp.zeros((nh, tr), F32) for _ in pairs]
    for ia, (a0, a1) in enumerate(pairs):
        for ib in range(ia + 1, len(pairs)):
            b0, b1 = pairs[ib]
            if a0 <= b0 and a1 <= b1:
                rank[ib] = rank[ib] + 1.0
            else:
                wins = jnp.where(cand[ia] >= cand[ib], 1.0, 0.0)
                rank[ib] = rank[ib] + wins
                rank[ia] = rank[ia] + (1.0 - wins)
    e0 = [jnp.exp(v0[a] - v0[0]) for a in range(kt)]
    e1 = [jnp.exp(v1[b] - v1[0]) for b in range(kt)]
    width = [jnp.zeros((nh, tr), F32) for _ in range(kt)]
    z = jnp.zeros((nh, tr), F32)
    for ic, (a, b) in enumerate(pairs):
        sel = jnp.where(rank[ic] < float(kt), 1.0, 0.0)
        width[a] = width[a] + sel
        z = z + sel * (e0[a] * e1[b])
    inv_z = 1.0 / z

    def key_rows(i):
        return slice(i * nh, (i + 1) * nh)

    @pl.when(jnp.logical_not(has_tie))
    def _():
        for i in range(nk):
            x0, x1 = vals_s[0, i], vals_s[1, i]
            width_i = jnp.zeros((nh, tr), F32)
            for a in range(kt):
                width_i = jnp.where(x0 == v0[a], width[a], width_i)
            a0_ref[0, key_rows(i), :] = jnp.where(x0 >= v0[kt - 1], jnp.exp(x0 - v0[0]) * inv_z, 0.0)
            l0_ref[0, key_rows(i), :] = width_i
            above8 = v1[7] > x1
            piv = jnp.where(above8, v1[11], v1[3])
            above4 = piv > x1
            piv = jnp.where(above8, jnp.where(above4, v1[13], v1[9]), jnp.where(above4, v1[5], v1[1]))
            above2 = piv > x1
            piv = jnp.where(
                above8,
                jnp.where(above4, jnp.where(above2, v1[14], v1[12]), jnp.where(above2, v1[10], v1[8])),
                jnp.where(above4, jnp.where(above2, v1[6], v1[4]), jnp.where(above2, v1[2], v1[0])))
            pos = (jnp.where(above8, 8.0, 0.0) + jnp.where(above4, 4.0, 0.0) + jnp.where(above2, 2.0, 0.0)
                   + jnp.where(piv > x1, 1.0, 0.0))
            chosen = x1 >= v1[kt - 1]
            b1_ref[0, key_rows(i), :] = jnp.where(chosen, jnp.exp(x1 - v1[0]), 0.0)
            r1_ref[0, key_rows(i), :] = jnp.where(chosen, pos, float(kt))

    @pl.when(has_tie)
    def _():
        rank_s[...] = jnp.full(rank_s.shape, float(kt), F32)
        ex_s[...] = jnp.zeros(ex_s.shape, F32)
        key_iota = lax.broadcasted_iota(I32, (nk, nh, tr), 0)

        def extract(kk, carry):
            for p in range(2):
                v = vals_s[p]
                m = jnp.max(v, axis=0)
                idx = jnp.min(jnp.where(v == m[None], key_iota, nk), axis=0)
                hit = key_iota == idx[None]
                vals_s[p] = jnp.where(hit, -jnp.inf, v)
                rank_s[p] = jnp.where(hit, lax.convert_element_type(kk, F32), rank_s[p])
                ex_s[p] = jnp.where(hit, jnp.exp(m - tops[p][0])[None], ex_s[p])
            return carry

        lax.fori_loop(0, kt, extract, 0)
        r0 = rank_s[0]
        l0 = jnp.zeros((nk, nh, tr), F32)
        for a in range(kt):
            l0 = jnp.where(r0 == float(a), width[a][None], l0)
        a0_ref[0] = (ex_s[0] * inv_z[None]).reshape(nk * nh, tr)
        l0_ref[0] = l0.reshape(nk * nh, tr)
        b1_ref[0] = ex_s[1].reshape(nk * nh, tr)
        r1_ref[0] = rank_s[1].reshape(nk * nh, tr)


def _peer_routing(hn, peer_wq, peer_subkeys, tr):
    n, d = hn.shape
    nk, nh, kd = PEER_KEYS, PEER_HEADS, PEER_KEY_DIM
    wqt = peer_wq.reshape(d, nh, 2, kd).transpose(2, 1, 3, 0).reshape(2 * nh * kd, d).astype(BF16)
    eye = jnp.eye(nh, dtype=peer_subkeys.dtype)
    kbig = jnp.einsum("hpnd,hg->pnhgd", peer_subkeys, eye).reshape(2, nk * nh, nh * kd).astype(BF16)
    rows = nk * nh
    assert tr == LANES
    out = jax.ShapeDtypeStruct((n // tr, rows, tr), F32)
    spec = pl.BlockSpec((1, rows, tr), lambda i: (i, 0, 0))
    return pl.pallas_call(
        functools.partial(_route_body, tr=tr),
        grid=(n // tr,),
        in_specs=[pl.BlockSpec((tr, d), lambda i: (i, 0)),
                  pl.BlockSpec(wqt.shape, lambda i: (0, 0)),
                  pl.BlockSpec(kbig.shape, lambda i: (0, 0, 0))],
        out_specs=(spec, spec, spec, spec),
        out_shape=(out, out, out, out),
        scratch_shapes=[pltpu.VMEM((2, nk, nh, tr), F32), pltpu.VMEM((2, nk, nh, tr), F32),
                        pltpu.VMEM((2, nk, nh, tr), F32)],
        compiler_params=_params(("parallel",)),
        name="peer_routing",
    )(hn, wqt, kbig)


def _expert_body(hn_ref, h_ref, u_ref, vt_ref, a0_ref, l0_ref, b1_ref, r1_ref, y_ref,
                 acc_s, g0_s, g1_s, ga0_s, ga1_s, br_s, *, tm, te, n_eb):
    step = pl.program_id(1)
    nk, nh = PEER_KEYS, PEER_HEADS
    gt = (g0_s, g1_s)
    ga = (ga0_s, ga1_s)
    pack = 2 * SUBLANES

    every = slice(0, te)
    n_slices = te // EXPERT_SLICE
    slices = [slice(k * EXPERT_SLICE, (k + 1) * EXPERT_SLICE) for k in range(n_slices)]

    def project(rs=every):
        return lax.dot_general(u_ref[rs, :], hn_ref[...], NT_DIMS, preferred_element_type=F32)

    def finish(act, src, dst, rs=every):
        act = 0.5 * act * (1.0 + lax.erf(act * (2.0 ** -0.5)))
        dst[rs, :] = src[rs, :] * act.astype(BF16)

    def gate(dst, rs=every):
        for il in range(rs.start // nk, rs.stop // nk):
            i = step * (te // nk) + il
            for c in range(tm // LANES):
                g = jnp.zeros((nk, LANES), BF16)
                for h in range(nh):
                    row = pl.ds(i * nh + h, 1)
                    a_row = jnp.broadcast_to(a0_ref[c, row, :], (pack, LANES)).astype(BF16)
                    l_row = jnp.broadcast_to(l0_ref[c, row, :], (pack, LANES)).astype(BF16)
                    a_row = jnp.tile(a_row, (nk // pack, 1))
                    l_row = jnp.tile(l_row, (nk // pack, 1))
                    b1 = br_s[c, h, :, 0].reshape(nk, LANES)
                    r1 = br_s[c, h, :, 1].reshape(nk, LANES)
                    g = g + a_row * jnp.where(r1 < l_row, b1, jnp.zeros((), BF16))
                blk = (slice(il * nk, (il + 1) * nk), slice(c * LANES, (c + 1) * LANES))
                dst[blk] = g

    def apply(src):
        acc_s[...] += jnp.dot(vt_ref[...], src[...], preferred_element_type=F32)

    @pl.when(step == 0)
    def _():
        acc_s[...] = jnp.zeros_like(acc_s)
        for c in range(tm // LANES):
            for h in range(nh):
                b1 = b1_ref[c, pl.ds(h, nk, stride=nh), :].astype(BF16)
                r1 = r1_ref[c, pl.ds(h, nk, stride=nh), :].astype(BF16)
                br_s[c, h, :, 0] = b1.reshape(nk // pack, pack, LANES)
                br_s[c, h, :, 1] = r1.reshape(nk // pack, pack, LANES)
        gate(gt[0])

    @pl.when(step == 1)
    def _():
        finish(project(), gt[0], ga[0])
        gate(gt[1])

    for p in range(2):
        @pl.when((step >= 2) & (step < n_eb) & (step % 2 == p))
        def _():
            total = None
            for rs in slices:
                gate(gt[p], rs)
                act = project(rs)
                part = jnp.dot(vt_ref[:, rs], ga[p][rs, :], preferred_element_type=F32)
                total = part if total is None else total + part
                finish(act, gt[1 - p], ga[1 - p], rs)
            acc_s[...] += total

    @pl.when(step == n_eb)
    def _():
        p = n_eb % 2
        act = project()
        apply(ga[p])
        finish(act, gt[1 - p], ga[1 - p])

    @pl.when(step == n_eb + 1)
    def _():
        apply(ga[(n_eb + 1) % 2])
        y_ref[...] = h_ref[...] + acc_s[...].T


def _peer_experts(hn, h, peer_u, peer_v, a0, l0, b1, r1, tm, te):
    n, d = hn.shape
    ne = peer_u.shape[0]
    rows = a0.shape[1]
    tok = pl.BlockSpec((tm // LANES, rows, LANES), lambda t, e: (t, 0, 0))
    n_eb = ne // te
    return pl.pallas_call(
        functools.partial(_expert_body, tm=tm, te=te, n_eb=n_eb),
        grid=(n // tm, n_eb + 2),
        in_specs=[pl.BlockSpec((tm, d), lambda t, s: (t, 0)), pl.BlockSpec((tm, d), lambda t, s: (t, 0)),
                  pl.BlockSpec((te, d), lambda t, s: (jnp.clip(s - 1, 0, n_eb - 1), 0)),
                  pl.BlockSpec((d, te), lambda t, s: (0, jnp.clip(s - 2, 0, n_eb - 1))),
                  tok, tok, tok, tok],
        out_specs=pl.BlockSpec((tm, d), lambda t, s: (t, 0)),
        out_shape=jax.ShapeDtypeStruct((n, d), F32),
        scratch_shapes=[pltpu.VMEM((d, tm), F32), pltpu.VMEM((te, tm), BF16), pltpu.VMEM((te, tm), BF16),
                        pltpu.VMEM((te, tm), BF16), pltpu.VMEM((te, tm), BF16),
                        pltpu.VMEM((tm // LANES, PEER_HEADS, PEER_KEYS // (2 * SUBLANES), 2, 2 * SUBLANES, LANES),
                                   BF16)],
        compiler_params=_params(("parallel", "arbitrary")),
        name="peer_experts",
    )(hn, h, peer_u.astype(BF16), peer_v.T.astype(BF16), a0, l0, b1, r1)


def _tiles(batch, seq):
    return dict(tm=256, tq=256, kb=512, tr=128, te_tm=512, te=1024)


def kernel(x, norm1_g, w_in, q_norm_g, k_norm_g, pool_w, pool_scale, w_branch_attn, w_branch_pool, w_out, norm2_g,
           peer_wq, peer_subkeys, peer_u, peer_v):
    batch, seq, d = x.shape
    t = _tiles(batch, seq)
    x2 = x.reshape(batch * seq, d)
    for l in range(norm1_g.shape[0]):
        q, k, vt, qi, ki, wi, p, gate = _input_projection(
            x2, norm1_g[l], w_in[l], q_norm_g[l], k_norm_g[l], batch, seq, t["tm"], t["kb"])
        attn = _dsa_attention(q, qi, wi, k, vt, ki, batch, seq, t["tq"], t["kb"])
        h, hn = _mixer_output(attn, p, gate, x2, w_branch_attn[l], w_branch_pool[l], w_out[l], pool_w[l],
                              pool_scale[l], norm2_g[l], seq, t["tm"])
        a0, l0, b1, r1 = _peer_routing(hn, peer_wq[l], peer_subkeys[l], t["tr"])
        x2 = _peer_experts(hn, h, peer_u[l], peer_v[l], a0, l0, b1, r1, t["te_tm"], t["te"])
    return x2.reshape(batch, seq, d)
```

```python
import functools

import jax
import jax.numpy as jnp
import numpy as np
from jax import lax
from jax.experimental import pallas as pl
from jax.experimental.pallas import tpu as pltpu

CHUNK = 64
EPS = 1e-6
N_HEADS = 8
HEAD_DIM = 64
ATTN_WIDTH = N_HEADS * HEAD_DIM
ROT_HALF = HEAD_DIM // 8
ROPE_THETA = 500000.0
IDX_HEADS = 8
IDX_DIM = 64
TOPK_MAX = 256
POOL_WINDOWS = (2, 4, 8, 16)
POOL_WIDTH = 512
POOL_GROUP_DIM = POOL_WIDTH // len(POOL_WINDOWS)
POOL_HALO = 16
PEER_HEADS = 8
PEER_KEYS = 128
PEER_KEY_DIM = 64
PEER_TOPK = 16

LANES = 128
SUBLANES = 8
VMEM_LIMIT = 56 * 1024 * 1024

F32 = jnp.float32
BF16 = jnp.bfloat16
I32 = jnp.int32
I16 = jnp.int16
INT_MIN = -2147483648
HALF_BITS = 16
HALF_MASK = 0xFFFF
HALF_BIAS = 32768
NEG = -1e30
LOG2_E = 1.4426950408889634
ATTN_HEAD_GROUP = 4
EXPERT_SLICE = 256
NT_DIMS = (((1,), (1,)), ((), ()))


def _params(sem):
    return pltpu.CompilerParams(dimension_semantics=sem, vmem_limit_bytes=VMEM_LIMIT)


def _rope(t, c, s_lo, s_hi):
    w = t.shape[-1]
    return t * c + pltpu.roll(t, w - ROT_HALF, 1) * s_lo + pltpu.roll(t, ROT_HALF, 1) * s_hi


def _proj_body(x_ref, g1_ref, wqkv_ref, wqi_ref, wki_ref, wwi_ref, wp_ref, wgl_ref, qg_ref, kg_ref, bd_ref,
               c_ref, slo_ref, shi_ref,
               q_ref, k_ref, vt_ref, qi_ref, ki_ref, wi_ref, p_ref, gate_ref):
    x = x_ref[...]
    xn = x * lax.rsqrt(jnp.mean(x * x, axis=-1, keepdims=True) + EPS) * g1_ref[...]
    xb = xn.astype(BF16)
    c, s_lo, s_hi = c_ref[...], slo_ref[...], shi_ref[...]

    def head_norm(t, g):
        ms = jnp.dot((t * t).astype(BF16), bd_ref[...], preferred_element_type=F32)
        return t * lax.rsqrt(ms + EPS) * g

    qkv = jnp.dot(xb, wqkv_ref[...], preferred_element_type=F32)
    w = ATTN_WIDTH
    q = _rope(head_norm(qkv[:, :w], qg_ref[...]), c, s_lo, s_hi) * (HEAD_DIM ** -0.5 * LOG2_E)
    k = _rope(head_norm(qkv[:, w:2 * w], kg_ref[...]), c, s_lo, s_hi)
    q_ref[...] = q.astype(BF16)
    k_ref[...] = k.astype(BF16)
    vt_ref[0, 0] = qkv[:, 2 * w:].T.astype(BF16)
    qi = jnp.dot(xb, wqi_ref[...], preferred_element_type=F32)
    qi_ref[...] = (_rope(qi, c, s_lo, s_hi) * (IDX_DIM ** -0.5)).astype(BF16)
    ki = jnp.dot(xb, wki_ref[...], preferred_element_type=F32)
    ki = _rope(ki, c[:, :LANES], s_lo[:, :LANES], s_hi[:, :LANES])
    ki_ref[...] = ki[:, :IDX_DIM].astype(BF16)
    wi_ref[...] = jnp.dot(xb, wwi_ref[...], preferred_element_type=F32) * (IDX_HEADS ** -0.5)
    p_ref[...] = jnp.dot(xb, wp_ref[...], preferred_element_type=F32)
    gate_ref[...] = jax.nn.sigmoid(jnp.dot(xb, wgl_ref[...], preferred_element_type=F32)).astype(BF16)


def _rope_tables(seq):
    inv_freq = ROPE_THETA ** (-jnp.arange(ROT_HALF, dtype=F32) / ROT_HALF)
    ang = jnp.arange(seq, dtype=F32)[:, None] * inv_freq[None, :]
    cos, sin = jnp.cos(ang), jnp.sin(ang)
    rest = HEAD_DIM - 2 * ROT_HALF
    ones = jnp.ones((seq, rest), F32)
    zeros = jnp.zeros((seq, rest), F32)
    zh = jnp.zeros((seq, ROT_HALF), F32)
    c = jnp.concatenate([cos, cos, ones], axis=1)
    s_lo = jnp.concatenate([-sin, zh, zeros], axis=1)
    s_hi = jnp.concatenate([zh, sin, zeros], axis=1)
    tile = lambda t: jnp.tile(t, (1, N_HEADS))
    return tile(c), tile(s_lo), tile(s_hi)


def _input_projection(x2, norm1_g, w_in, q_norm_g, k_norm_g, batch, seq, tm, kb):
    n, d = x2.shape
    w = ATTN_WIDTH
    o = np.cumsum([0, w, w, w, IDX_HEADS * IDX_DIM, IDX_DIM, IDX_HEADS, POOL_WIDTH, 2 * d])
    wb = w_in.astype(BF16)
    wqkv = wb[:, o[0]:o[3]]
    wqi = wb[:, o[3]:o[4]]
    wki = jnp.pad(wb[:, o[4]:o[5]], ((0, 0), (0, LANES - IDX_DIM)))
    wwi = jnp.pad(wb[:, o[5]:o[6]], ((0, 0), (0, LANES - IDX_HEADS)))
    wp = wb[:, o[6]:o[7]]
    wgl = wb[:, o[7]:o[8]]
    bd = jnp.kron(jnp.eye(N_HEADS, dtype=F32), jnp.full((HEAD_DIM, HEAD_DIM), 1.0 / HEAD_DIM, F32)).astype(BF16)
    c, s_lo, s_hi = _rope_tables(seq)
    tps = seq // tm
    const = lambda shape: pl.BlockSpec(shape, lambda i: (0,) * len(shape))
    row = lambda width: pl.BlockSpec((tm, width), lambda i: (i, 0))
    tab = pl.BlockSpec((tm, w), lambda i: (i % tps, 0))
    per_kb = kb // tm
    out_shapes = (
        jax.ShapeDtypeStruct((n, w), BF16),
        jax.ShapeDtypeStruct((n, w), BF16),
        jax.ShapeDtypeStruct((batch, seq // kb, w, kb), BF16),
        jax.ShapeDtypeStruct((n, w), BF16),
        jax.ShapeDtypeStruct((n, IDX_DIM), BF16),
        jax.ShapeDtypeStruct((n, LANES), F32),
        jax.ShapeDtypeStruct((n, POOL_WIDTH), F32),
        jax.ShapeDtypeStruct((n, 2 * d), BF16),
    )
    out_specs = (
        row(w), row(w),
        pl.BlockSpec((1, 1, w, tm), lambda i: (i // tps, (i % tps) // per_kb, 0, (i % tps) % per_kb)),
        row(w), row(IDX_DIM), row(LANES), row(POOL_WIDTH), row(2 * d),
    )
    return pl.pallas_call(
        _proj_body,
        grid=(n // tm,),
        in_specs=[row(d), const((1, d)), const(wqkv.shape), const(wqi.shape), const(wki.shape), const(wwi.shape),
                  const(wp.shape), const(wgl.shape), const((1, w)), const((1, w)), const(bd.shape), tab, tab, tab],
        out_specs=out_specs,
        out_shape=out_shapes,
        compiler_params=_params(("parallel",)),
        name="input_projection",
    )(x2, norm1_g.reshape(1, d), wqkv, wqi, wki, wwi, wp, wgl,
      jnp.tile(q_norm_g, N_HEADS).reshape(1, w), jnp.tile(k_norm_g, N_HEADS).reshape(1, w), bd, c, s_lo, s_hi)


def _sortable(v):
    b = lax.bitcast_convert_type(v, I32)
    b = jnp.where(b == INT_MIN, 0, b)
    return jnp.where(b < 0, b ^ 0x7FFFFFFF, b)


def _dsa_body(q_ref, qi_ref, wi_ref, k_ref, vt_ref, ki_ref, o_ref, key_s, bias_s, acc_s, s_s, hi_s, lo_s,
              *, seq, tq, kb, topk):
    j = pl.program_id(1)
    nblk = ((j + 1) * tq + kb - 1) // kb
    lane = lax.broadcasted_iota(I32, (1, tq), 1)
    lim = j * tq + (lane // CHUNK + 1) * CHUNK
    row_iota = lax.broadcasted_iota(I32, (kb, tq), 0)
    wi_t = wi_ref[...].T[:IDX_HEADS, :]
    qi = qi_ref[...]

    def rows(i):
        return pl.ds(pl.multiple_of(i * kb, kb), kb)

    def score_block(i, carry):
        kib = ki_ref[rows(i), :]
        acc = jnp.zeros((kb, tq), F32)
        for h in range(IDX_HEADS):
            lg = lax.dot_general(kib, qi[:, h * IDX_DIM:(h + 1) * IDX_DIM], NT_DIMS, preferred_element_type=F32)
            acc = acc + jnp.maximum(lg, 0.0) * wi_t[h:h + 1, :]
        key = jnp.where(i * kb + row_iota < lim, _sortable(acc), INT_MIN)
        key_s[rows(i), :] = key
        hi_s[rows(i), :] = (key >> HALF_BITS).astype(I16)
        lo_s[rows(i), :] = ((key & HALF_MASK) - HALF_BIAS).astype(I16)
        return carry

    lax.fori_loop(0, nblk, score_block, 0)

    def count(pred):
        def body(i, c):
            m = pred(key_s[rows(i), :], i * kb + row_iota)
            return c + jnp.sum(m.astype(I32).reshape(kb // SUBLANES, SUBLANES, tq), axis=0)
        c8 = lax.fori_loop(0, nblk, body, jnp.zeros((SUBLANES, tq), I32))
        return jnp.sum(c8, axis=0, keepdims=True)

    pack = 2 * SUBLANES
    one16, zero16 = jnp.ones((), I16), jnp.zeros((), I16)

    def spread16(v):
        return jnp.broadcast_to(v.astype(I16), (kb, tq))

    def count16(ref, pred):
        def body(i, c):
            m = pred(ref[rows(i), :])
            hit = jnp.where(m, one16, zero16)
            parts = [hit[r:r + pack, :] for r in range(0, kb, pack)]
            while len(parts) > 1:
                parts = [parts[r] + parts[r + 1] for r in range(0, len(parts), 2)]
            return c + parts[0]
        c16 = lax.fori_loop(0, nblk, body, jnp.zeros((pack, tq), I16))
        return jnp.sum(c16.astype(I32), axis=0, keepdims=True)

    def search16(ref, need):
        def bit(it, tu):
            cand_u = tu | lax.shift_left(jnp.int32(1), HALF_BITS - 1 - it)
            cand = spread16(cand_u - HALF_BIAS)
            return jnp.where(count16(ref, lambda blk: blk >= cand) >= need, cand_u, tu)
        return lax.fori_loop(0, HALF_BITS, bit, jnp.zeros((1, tq), I32))

    hi_u = search16(hi_s, topk)
    thr_hi = spread16(hi_u - HALF_BIAS)
    above = count16(hi_s, lambda blk: blk > thr_hi)

    def mask_low(i, carry):
        lo_s[rows(i), :] = jnp.where(hi_s[rows(i), :] == thr_hi, lo_s[rows(i), :], jnp.full((), -HALF_BIAS, I16))
        return carry

    lax.fori_loop(0, nblk, mask_low, 0)
    lo_u = search16(lo_s, topk - above)
    thr = lax.shift_left(hi_u - HALF_BIAS, HALF_BITS) | lo_u
    idx_bits = int(seq).bit_length()
    surplus = (count(lambda blk, idx: blk >= thr) != topk) & (thr != INT_MIN)
    has_tie = jnp.max(jnp.where(surplus, 1.0, 0.0)) > 0.5

    def resolve_ties():
        need = topk - count(lambda blk, idx: blk > thr)

        def index_bit(it, jj):
            cand = jj | lax.shift_left(jnp.int32(1), idx_bits - 1 - it)
            return jnp.where(count(lambda blk, idx: (blk == thr) & (idx < cand)) <= need, cand, jj)

        return lax.fori_loop(0, idx_bits, index_bit, jnp.zeros((1, tq), I32))

    tie_end = lax.cond(has_tie, resolve_ties, lambda: jnp.full((1, tq), (1 << idx_bits) - 1, I32))

    def bias_block(i, carry):
        blk = key_s[rows(i), :]
        idx = i * kb + row_iota
        sel = ((blk > thr) | ((blk == thr) & (idx < tie_end))) & (idx < lim)
        bias_s[rows(i), :] = jnp.where(sel, 0.0, NEG)
        return carry

    lax.fori_loop(0, nblk, bias_block, 0)

    q = q_ref[...]
    pair_lane = lax.broadcasted_iota(I32, (tq, 2 * HEAD_DIM), 1)
    qm = []
    for h in range(N_HEADS):
        pair = q[:, (h // 2) * 2 * HEAD_DIM:(h // 2 + 1) * 2 * HEAD_DIM]
        qm.append(jnp.where((pair_lane // HEAD_DIM) == (h % 2), pair, jnp.zeros_like(pair)))
    acc_s[...] = jnp.zeros_like(acc_s)
    group = s_s.shape[0]

    def fold(t):
        return t.reshape(kb // SUBLANES, SUBLANES, tq)

    for g0 in range(0, N_HEADS, group):
        heads = range(g0, g0 + group)

        def score_pass(i, ms):
            bias = bias_s[rows(i), :]
            out = []
            for hh, h in enumerate(heads):
                kblk = k_ref[rows(i), (h // 2) * 2 * HEAD_DIM:(h // 2 + 1) * 2 * HEAD_DIM]
                s = lax.dot_general(kblk, qm[h], NT_DIMS, preferred_element_type=F32) + bias
                s_s[hh, rows(i), :] = s
                out.append(jnp.maximum(ms[hh], jnp.max(fold(s), axis=0)))
            return tuple(out)

        ms = lax.fori_loop(0, nblk, score_pass, tuple(jnp.full((SUBLANES, tq), NEG, F32) for _ in heads))
        mx = [jnp.max(m, axis=0, keepdims=True) for m in ms]

        def value_pass(i, ls):
            out = []
            for hh, h in enumerate(heads):
                hs = slice(h * HEAD_DIM, (h + 1) * HEAD_DIM)
                p = jnp.exp2(s_s[hh, rows(i), :] - mx[hh])
                out.append(ls[hh] + jnp.sum(fold(p), axis=0))
                acc_s[hs, :] += jnp.dot(vt_ref[0, i, hs, :], p.astype(BF16), preferred_element_type=F32)
            return tuple(out)

        ls = lax.fori_loop(0, nblk, value_pass, tuple(jnp.zeros((SUBLANES, tq), F32) for _ in heads))
        for hh, h in enumerate(heads):
            hs = slice(h * HEAD_DIM, (h + 1) * HEAD_DIM)
            acc_s[hs, :] = acc_s[hs, :] / jnp.sum(ls[hh], axis=0, keepdims=True)
    o_ref[...] = acc_s[...].T.astype(BF16)


def _dsa_attention(q, qi, wi, k, vt, ki, batch, seq, tq, kb):
    n, w = q.shape
    topk = min(TOPK_MAX, seq // 4)
    nq = seq // tq
    tile = lambda width: pl.BlockSpec((tq, width), lambda b, j: (b * nq + j, 0))
    whole = lambda width: pl.BlockSpec((seq, width), lambda b, j: (b, 0))
    return pl.pallas_call(
        functools.partial(_dsa_body, seq=seq, tq=tq, kb=kb, topk=topk),
        grid=(batch, nq),
        in_specs=[tile(w), tile(w), tile(LANES), whole(w),
                  pl.BlockSpec((1, seq // kb, w, kb), lambda b, j: (b, 0, 0, 0)), whole(IDX_DIM)],
        out_specs=tile(w),
        out_shape=jax.ShapeDtypeStruct((n, w), BF16),
        scratch_shapes=[pltpu.VMEM((seq, tq), I32), pltpu.VMEM((seq, tq), F32), pltpu.VMEM((w, tq), F32),
                        pltpu.VMEM((ATTN_HEAD_GROUP, seq, tq), F32),
                        pltpu.VMEM((seq, tq), I16), pltpu.VMEM((seq, tq), I16)],
        compiler_params=_params(("parallel", "arbitrary")),
        name="dsa_attention",
    )(q, qi, wi, k, vt, ki)


def _mix_body(attn_ref, p_ref, halo_ref, gate_ref, x_ref, wa_ref, wpb_ref, wo_ref, pw_ref, ps_ref, g2_ref,
              h_ref, hn_ref, ext_s, *, tm, tps):
    st = pl.program_id(0) % tps
    ext_s[0:POOL_HALO, :] = jnp.where(st == 0, 0.0, halo_ref[...])
    ext_s[POOL_HALO:POOL_HALO + tm, :] = p_ref[...]
    t1 = (st * tm + 1 + lax.broadcasted_iota(I32, (tm, POOL_GROUP_DIM), 0)).astype(F32)
    mixed = []
    for g, win in enumerate(POOL_WINDOWS):
        ls = slice(g * POOL_GROUP_DIM, (g + 1) * POOL_GROUP_DIM)
        frame = ext_s[POOL_HALO:POOL_HALO + tm, ls]
        tot = frame
        for dlt in range(1, win):
            tot = tot + ext_s[POOL_HALO - dlt:POOL_HALO - dlt + tm, ls]
        pooled = tot / jnp.minimum(t1, float(win)) - frame
        mixed.append(jnp.dot(pooled.astype(BF16), pw_ref[g], preferred_element_type=F32))
    mixed = jnp.concatenate(mixed, axis=1) * ps_ref[...]
    y_pool = jnp.dot(mixed.astype(BF16), wpb_ref[...], preferred_element_type=F32)
    y_attn = jnp.dot(attn_ref[...], wa_ref[...], preferred_element_type=F32)
    d = y_attn.shape[1]
    gate = gate_ref[...].astype(F32)
    z = gate[:, :d] * y_attn + gate[:, d:] * y_pool
    h = x_ref[...] + jnp.dot(z.astype(BF16), wo_ref[...], preferred_element_type=F32)
    h_ref[...] = h
    hn = h * lax.rsqrt(jnp.mean(h * h, axis=-1, keepdims=True) + EPS) * g2_ref[...]
    hn_ref[...] = hn.astype(BF16)


def _mixer_output(attn, p, gate, x2, w_branch_attn, w_branch_pool, w_out, pool_w, pool_scale, norm2_g, seq, tm):
    n, d = x2.shape
    tps = seq // tm
    hb = tm // POOL_HALO
    const = lambda shape: pl.BlockSpec(shape, lambda i: (0,) * len(shape))
    row = lambda width: pl.BlockSpec((tm, width), lambda i: (i, 0))
    return pl.pallas_call(
        functools.partial(_mix_body, tm=tm, tps=tps),
        grid=(n // tm,),
        in_specs=[row(ATTN_WIDTH), row(POOL_WIDTH),
                  pl.BlockSpec((POOL_HALO, POOL_WIDTH), lambda i: (jnp.maximum(i * hb - 1, 0), 0)),
                  row(2 * d), row(d), const((ATTN_WIDTH, d)), const((POOL_WIDTH, d)), const((d, d)),
                  const(pool_w.shape), const((1, POOL_WIDTH)), const((1, d))],
        out_specs=(row(d), row(d)),
        out_shape=(jax.ShapeDtypeStruct((n, d), F32), jax.ShapeDtypeStruct((n, d), BF16)),
        scratch_shapes=[pltpu.VMEM((POOL_HALO + tm, POOL_WIDTH), F32)],
        compiler_params=_params(("parallel",)),
        name="mixer_output",
    )(attn, p, p, gate, x2, w_branch_attn.astype(BF16), w_branch_pool.astype(BF16), w_out.astype(BF16),
      pool_w.astype(BF16), pool_scale.reshape(1, POOL_WIDTH), norm2_g.reshape(1, d))


def _candidate_pairs():
    return [(a, b) for a in range(PEER_TOPK) for b in range(PEER_TOPK) if (a + 1) * (b + 1) <= PEER_TOPK]


def _sort_desc(v):
    v = list(v)
    n = len(v)
    k = 2
    while k <= n:
        j = k // 2
        while j >= 1:
            for i in range(n):
                m = i ^ j
                if m > i:
                    hi, lo = jnp.maximum(v[i], v[m]), jnp.minimum(v[i], v[m])
                    v[i], v[m] = (hi, lo) if (i & k) == 0 else (lo, hi)
            j //= 2
        k *= 2
    return v


def _merge_top(a, b):
    n = len(a)
    c = [jnp.maximum(a[i], b[n - 1 - i]) for i in range(n)]
    j = n // 2
    while j >= 1:
        for i in range(n):
            m = i ^ j
            if m > i:
                c[i], c[m] = jnp.maximum(c[i], c[m]), jnp.minimum(c[i], c[m])
        j //= 2
    return c


def _route_body(hn_ref, wqt_ref, kbig_ref, a0_ref, l0_ref, b1_ref, r1_ref, vals_s, rank_s, ex_s, *, tr):
    nk, nh, kt = PEER_KEYS, PEER_HEADS, PEER_TOPK
    half_rows = nh * PEER_KEY_DIM
    qt = lax.dot_general(wqt_ref[...], hn_ref[...], NT_DIMS, preferred_element_type=F32).astype(BF16)
    for p in range(2):
        sub = jnp.dot(kbig_ref[p], qt[p * half_rows:(p + 1) * half_rows], preferred_element_type=F32)
        vals_s[p] = sub.reshape(nk, nh, tr)

    def best(p, lo, hi):
        if hi - lo == kt:
            return _sort_desc([vals_s[p, i] for i in range(lo, hi)])
        mid = (lo + hi) // 2
        return _merge_top(best(p, lo, mid), best(p, mid, hi))

    tops = [best(p, 0, nk) for p in range(2)]
    v0, v1 = tops
    tied = jnp.zeros((nh, tr), F32)
    for p in range(2):
        for a in range(kt - 1):
            tied = jnp.maximum(tied, jnp.where(tops[p][a] == tops[p][a + 1], 1.0, 0.0))
        above = [jnp.where(vals_s[p, i] >= tops[p][kt - 1], 1.0, 0.0) for i in range(nk)]
        while len(above) > 1:
            above = [above[i] + above[i + 1] for i in range(0, len(above), 2)]
        tied = jnp.maximum(tied, jnp.where(above[0] != float(kt), 1.0, 0.0))
    has_tie = jnp.max(tied) > 0.5

    pairs = _candidate_pairs()
    cand = [v0[a] + v1[b] for a, b in pairs]
    rank = [jnp.zeros((nh, tr), F32) for _ in pairs]
    for ia, (a0, a1) in enumerate(pairs):
        for ib in range(ia + 1, len(pairs)):
            b0, b1 = pairs[ib]
            if a0 <= b0 and a1 <= b1:
                rank[ib] = rank[ib] + 1.0
            else:
                wins = jnp.where(cand[ia] >= cand[ib], 1.0, 0.0)
                rank[ib] = rank[ib] + wins
                rank[ia] = rank[ia] + (1.0 - wins)
    e0 = [jnp.exp(v0[a] - v0[0]) for a in range(kt)]
    e1 = [jnp.exp(v1[b] - v1[0]) for b in range(kt)]
    width = [jnp.zeros((nh, tr), F32) for _ in range(kt)]
    z = jnp.zeros((nh, tr), F32)
    for ic, (a, b) in enumerate(pairs):
        sel = jnp.where(rank[ic] < float(kt), 1.0, 0.0)
        width[a] = width[a] + sel
        z = z + sel * (e0[a] * e1[b])
    inv_z = 1.0 / z

    def key_rows(i):
        return slice(i * nh, (i + 1) * nh)

    @pl.when(jnp.logical_not(has_tie))
    def _():
        for i in range(nk):
            x0, x1 = vals_s[0, i], vals_s[1, i]
            width_i = jnp.zeros((nh, tr), F32)
            for a in range(kt):
                width_i = jnp.where(x0 == v0[a], width[a], width_i)
            a0_ref[0, key_rows(i), :] = jnp.where(x0 >= v0[kt - 1], jnp.exp(x0 - v0[0]) * inv_z, 0.0)
            l0_ref[0, key_rows(i), :] = width_i
            above8 = v1[7] > x1
            piv = jnp.where(above8, v1[11], v1[3])
            above4 = piv > x1
            piv = jnp.where(above8, jnp.where(above4, v1[13], v1[9]), jnp.where(above4, v1[5], v1[1]))
            above2 = piv > x1
            piv = jnp.where(
                above8,
                jnp.where(above4, jnp.where(above2, v1[14], v1[12]), jnp.where(above2, v1[10], v1[8])),
                jnp.where(above4, jnp.where(above2, v1[6], v1[4]), jnp.where(above2, v1[2], v1[0])))
            pos = (jnp.where(above8, 8.0, 0.0) + jnp.where(above4, 4.0, 0.0) + jnp.where(above2, 2.0, 0.0)
                   + jnp.where(piv > x1, 1.0, 0.0))
            chosen = x1 >= v1[kt - 1]
            b1_ref[0, key_rows(i), :] = jnp.where(chosen, jnp.exp(x1 - v1[0]), 0.0)
            r1_ref[0, key_rows(i), :] = jnp.where(chosen, pos, float(kt))

    @pl.when(has_tie)
    def _():
        rank_s[...] = jnp.full(rank_s.shape, float(kt), F32)
        ex_s[...] = jnp.zeros(ex_s.shape, F32)
        key_iota = lax.broadcasted_iota(I32, (nk, nh, tr), 0)

        def extract(kk, carry):
            for p in range(2):
                v = vals_s[p]
                m = jnp.max(v, axis=0)
                idx = jnp.min(jnp.where(v == m[None], key_iota, nk), axis=0)
                hit = key_iota == idx[None]
                vals_s[p] = jnp.where(hit, -jnp.inf, v)
                rank_s[p] = jnp.where(hit, lax.convert_element_type(kk, F32), rank_s[p])
                ex_s[p] = jnp.where(hit, jnp.exp(m - tops[p][0])[None], ex_s[p])
            return carry

        lax.fori_loop(0, kt, extract, 0)
        r0 = rank_s[0]
        l0 = jnp.zeros((nk, nh, tr), F32)
        for a in range(kt):
            l0 = jnp.where(r0 == float(a), width[a][None], l0)
        a0_ref[0] = (ex_s[0] * inv_z[None]).reshape(nk * nh, tr)
        l0_ref[0] = l0.reshape(nk * nh, tr)
        b1_ref[0] = ex_s[1].reshape(nk * nh, tr)
        r1_ref[0] = rank_s[1].reshape(nk * nh, tr)


def _peer_routing(hn, peer_wq, peer_subkeys, tr):
    n, d = hn.shape
    nk, nh, kd = PEER_KEYS, PEER_HEADS, PEER_KEY_DIM
    wqt = peer_wq.reshape(d, nh, 2, kd).transpose(2, 1, 3, 0).reshape(2 * nh * kd, d).astype(BF16)
    eye = jnp.eye(nh, dtype=peer_subkeys.dtype)
    kbig = jnp.einsum("hpnd,hg->pnhgd", peer_subkeys, eye).reshape(2, nk * nh, nh * kd).astype(BF16)
    rows = nk * nh
    assert tr == LANES
    out = jax.ShapeDtypeStruct((n // tr, rows, tr), F32)
    spec = pl.BlockSpec((1, rows, tr), lambda i: (i, 0, 0))
    return pl.pallas_call(
        functools.partial(_route_body, tr=tr),
        grid=(n // tr,),
        in_specs=[pl.BlockSpec((tr, d), lambda i: (i, 0)),
                  pl.BlockSpec(wqt.shape, lambda i: (0, 0)),
                  pl.BlockSpec(kbig.shape, lambda i: (0, 0, 0))],
        out_specs=(spec, spec, spec, spec),
        out_shape=(out, out, out, out),
        scratch_shapes=[pltpu.VMEM((2, nk, nh, tr), F32), pltpu.VMEM((2, nk, nh, tr), F32),
                        pltpu.VMEM((2, nk, nh, tr), F32)],
        compiler_params=_params(("parallel",)),
        name="peer_routing",
    )(hn, wqt, kbig)


def _expert_body(hn_ref, h_ref, u_ref, vt_ref, a0_ref, l0_ref, b1_ref, r1_ref, y_ref,
                 acc_s, g0_s, g1_s, ga0_s, ga1_s, br_s, *, tm, te, n_eb):
    step = pl.program_id(1)
    nk, nh = PEER_KEYS, PEER_HEADS
    gt = (g0_s, g1_s)
    ga = (ga0_s, ga1_s)
    pack = 2 * SUBLANES

    every = slice(0, te)
    n_slices = te // EXPERT_SLICE
    slices = [slice(k * EXPERT_SLICE, (k + 1) * EXPERT_SLICE) for k in range(n_slices)]

    def project(rs=every):
        return lax.dot_general(u_ref[rs, :], hn_ref[...], NT_DIMS, preferred_element_type=F32)

    def finish(act, src, dst, rs=every):
        act = 0.5 * act * (1.0 + lax.erf(act * (2.0 ** -0.5)))
        dst[rs, :] = src[rs, :] * act.astype(BF16)

    def gate(dst, rs=every):
        for il in range(rs.start // nk, rs.stop // nk):
            i = step * (te // nk) + il
            for c in range(tm // LANES):
                g = jnp.zeros((nk, LANES), BF16)
                for h in range(nh):
                    row = pl.ds(i * nh + h, 1)
                    a_row = jnp.broadcast_to(a0_ref[c, row, :], (pack, LANES)).astype(BF16)
                    l_row = jnp.broadcast_to(l0_ref[c, row, :], (pack, LANES)).astype(BF16)
                    a_row = jnp.tile(a_row, (nk // pack, 1))
                    l_row = jnp.tile(l_row, (nk // pack, 1))
                    b1 = br_s[c, h, :, 0].reshape(nk, LANES)
                    r1 = br_s[c, h, :, 1].reshape(nk, LANES)
                    g = g + a_row * jnp.where(r1 < l_row, b1, jnp.zeros((), BF16))
                blk = (slice(il * nk, (il + 1) * nk), slice(c * LANES, (c + 1) * LANES))
                dst[blk] = g

    def apply(src):
        acc_s[...] += jnp.dot(vt_ref[0], src[...], preferred_element_type=F32)

    @pl.when(step == 0)
    def _():
        acc_s[...] = jnp.zeros_like(acc_s)
        for c in range(tm // LANES):
            for h in range(nh):
                b1 = b1_ref[c, pl.ds(h, nk, stride=nh), :].astype(BF16)
                r1 = r1_ref[c, pl.ds(h, nk, stride=nh), :].astype(BF16)
                br_s[c, h, :, 0] = b1.reshape(nk // pack, pack, LANES)
                br_s[c, h, :, 1] = r1.reshape(nk // pack, pack, LANES)
        gate(gt[0])

    @pl.when(step == 1)
    def _():
        finish(project(), gt[0], ga[0])
        gate(gt[1])

    for p in range(2):
        @pl.when((step >= 2) & (step < n_eb) & (step % 2 == p))
        def _():
            total = None
            for rs in slices:
                gate(gt[p], rs)
                act = project(rs)
                part = jnp.dot(vt_ref[0, :, rs], ga[p][rs, :], preferred_element_type=F32)
                total = part if total is None else total + part
                finish(act, gt[1 - p], ga[1 - p], rs)
            acc_s[...] += total

    @pl.when(step == n_eb)
    def _():
        p = n_eb % 2
        act = project()
        apply(ga[p])
        finish(act, gt[1 - p], ga[1 - p])

    @pl.when(step == n_eb + 1)
    def _():
        apply(ga[(n_eb + 1) % 2])
        y_ref[...] = h_ref[...] + acc_s[...].T


def _peer_experts(hn, h, peer_u, peer_v, a0, l0, b1, r1, tm, te):
    n, d = hn.shape
    ne = peer_u.shape[0]
    rows = a0.shape[1]
    tok = pl.BlockSpec((tm // LANES, rows, LANES), lambda t, e: (t, 0, 0))
    n_eb = ne // te
    vt_blocks = peer_v.astype(BF16).reshape(n_eb, te, d).transpose(0, 2, 1)
    return pl.pallas_call(
        functools.partial(_expert_body, tm=tm, te=te, n_eb=n_eb),
        grid=(n // tm, n_eb + 2),
        in_specs=[pl.BlockSpec((tm, d), lambda t, s: (t, 0)), pl.BlockSpec((tm, d), lambda t, s: (t, 0)),
                  pl.BlockSpec((te, d), lambda t, s: (jnp.clip(s - 1, 0, n_eb - 1), 0)),
                  pl.BlockSpec((1, d, te), lambda t, s: (jnp.clip(s - 2, 0, n_eb - 1), 0, 0)),
                  tok, tok, tok, tok],
        out_specs=pl.BlockSpec((tm, d), lambda t, s: (t, 0)),
        out_shape=jax.ShapeDtypeStruct((n, d), F32),
        scratch_shapes=[pltpu.VMEM((d, tm), F32), pltpu.VMEM((te, tm), BF16), pltpu.VMEM((te, tm), BF16),
                        pltpu.VMEM((te, tm), BF16), pltpu.VMEM((te, tm), BF16),
                        pltpu.VMEM((tm // LANES, PEER_HEADS, PEER_KEYS // (2 * SUBLANES), 2, 2 * SUBLANES, LANES),
                                   BF16)],
        compiler_params=_params(("parallel", "arbitrary")),
        name="peer_experts",
    )(hn, h, peer_u.astype(BF16), vt_blocks, a0, l0, b1, r1)


def _tiles(batch, seq):
    return dict(tm=256, tq=256, kb=512, tr=128, te_tm=512, te=1024)


def kernel(x, norm1_g, w_in, q_norm_g, k_norm_g, pool_w, pool_scale, w_branch_attn, w_branch_pool, w_out, norm2_g,
           peer_wq, peer_subkeys, peer_u, peer_v):
    batch, seq, d = x.shape
    t = _tiles(batch, seq)
    x2 = x.reshape(batch * seq, d)
    for l in range(norm1_g.shape[0]):
        q, k, vt, qi, ki, wi, p, gate = _input_projection(
            x2, norm1_g[l], w_in[l], q_norm_g[l], k_norm_g[l], batch, seq, t["tm"], t["kb"])
        attn = _dsa_attention(q, qi, wi, k, vt, ki, batch, seq, t["tq"], t["kb"])
        h, hn = _mixer_output(attn, p, gate, x2, w_branch_attn[l], w_branch_pool[l], w_out[l], pool_w[l],
                              pool_scale[l], norm2_g[l], seq, t["tm"])
        a0, l0, b1, r1 = _peer_routing(hn, peer_wq[l], peer_subkeys[l], t["tr"])
        x2 = _peer_experts(hn, h, peer_u[l], peer_v[l], a0, l0, b1, r1, t["te_tm"], t["te"])
    return x2.reshape(batch, seq, d)
```

```python
import functools

import jax
import jax.numpy as jnp
import numpy as np
from jax import lax
from jax.experimental import pallas as pl
from jax.experimental.pallas import tpu as pltpu

CHUNK = 64
EPS = 1e-6
N_HEADS = 8
HEAD_DIM = 64
ATTN_WIDTH = N_HEADS * HEAD_DIM
ROT_HALF = HEAD_DIM // 8
ROPE_THETA = 500000.0
IDX_HEADS = 8
IDX_DIM = 64
TOPK_MAX = 256
POOL_WINDOWS = (2, 4, 8, 16)
POOL_WIDTH = 512
POOL_GROUP_DIM = POOL_WIDTH // len(POOL_WINDOWS)
POOL_HALO = 16
PEER_HEADS = 8
PEER_KEYS = 128
PEER_KEY_DIM = 64
PEER_TOPK = 16

LANES = 128
SUBLANES = 8
VMEM_LIMIT = 56 * 1024 * 1024

F32 = jnp.float32
BF16 = jnp.bfloat16
I32 = jnp.int32
I16 = jnp.int16
INT_MIN = -2147483648
HALF_BITS = 16
HALF_MASK = 0xFFFF
HALF_BIAS = 32768
NEG = -1e30
LOG2_E = 1.4426950408889634
ATTN_HEAD_GROUP = 4
EXPERT_SLICE = 256
NT_DIMS = (((1,), (1,)), ((), ()))


def _params(sem):
    return pltpu.CompilerParams(dimension_semantics=sem, vmem_limit_bytes=VMEM_LIMIT)


def _rope(t, c, s_lo, s_hi):
    w = t.shape[-1]
    return t * c + pltpu.roll(t, w - ROT_HALF, 1) * s_lo + pltpu.roll(t, ROT_HALF, 1) * s_hi


def _proj_body(x_ref, g1_ref, wqkv_ref, wqi_ref, wki_ref, wwi_ref, wp_ref, wgl_ref, qg_ref, kg_ref, bd_ref,
               c_ref, slo_ref, shi_ref,
               q_ref, k_ref, vt_ref, qi_ref, ki_ref, wi_ref, p_ref, gate_ref):
    x = x_ref[...]
    xn = x * lax.rsqrt(jnp.mean(x * x, axis=-1, keepdims=True) + EPS) * g1_ref[...]
    xb = xn.astype(BF16)
    c, s_lo, s_hi = c_ref[...], slo_ref[...], shi_ref[...]

    def head_norm(t, g):
        ms = jnp.dot((t * t).astype(BF16), bd_ref[...], preferred_element_type=F32)
        return t * lax.rsqrt(ms + EPS) * g

    qkv = jnp.dot(xb, wqkv_ref[...], preferred_element_type=F32)
    w = ATTN_WIDTH
    q = _rope(head_norm(qkv[:, :w], qg_ref[...]), c, s_lo, s_hi) * (HEAD_DIM ** -0.5 * LOG2_E)
    k = _rope(head_norm(qkv[:, w:2 * w], kg_ref[...]), c, s_lo, s_hi)
    q_ref[...] = q.astype(BF16)
    k_ref[...] = k.astype(BF16)
    vt_ref[0, 0] = qkv[:, 2 * w:].T.astype(BF16)
    qi = jnp.dot(xb, wqi_ref[...], preferred_element_type=F32)
    qi_ref[...] = (_rope(qi, c, s_lo, s_hi) * (IDX_DIM ** -0.5)).astype(BF16)
    ki = jnp.dot(xb, wki_ref[...], preferred_element_type=F32)
    ki = _rope(ki, c[:, :LANES], s_lo[:, :LANES], s_hi[:, :LANES])
    ki_ref[...] = ki[:, :IDX_DIM].astype(BF16)
    wi_ref[...] = jnp.dot(xb, wwi_ref[...], preferred_element_type=F32) * (IDX_HEADS ** -0.5)
    p_ref[...] = jnp.dot(xb, wp_ref[...], preferred_element_type=F32)
    gate_ref[...] = jax.nn.sigmoid(jnp.dot(xb, wgl_ref[...], preferred_element_type=F32)).astype(BF16)


def _rope_tables(seq):
    inv_freq = ROPE_THETA ** (-jnp.arange(ROT_HALF, dtype=F32) / ROT_HALF)
    ang = jnp.arange(seq, dtype=F32)[:, None] * inv_freq[None, :]
    cos, sin = jnp.cos(ang), jnp.sin(ang)
    rest = HEAD_DIM - 2 * ROT_HALF
    ones = jnp.ones((seq, rest), F32)
    zeros = jnp.zeros((seq, rest), F32)
    zh = jnp.zeros((seq, ROT_HALF), F32)
    c = jnp.concatenate([cos, cos, ones], axis=1)
    s_lo = jnp.concatenate([-sin, zh, zeros], axis=1)
    s_hi = jnp.concatenate([zh, sin, zeros], axis=1)
    tile = lambda t: jnp.tile(t, (1, N_HEADS))
    return tile(c), tile(s_lo), tile(s_hi)


def _input_projection(x2, norm1_g, w_in, q_norm_g, k_norm_g, batch, seq, tm, kb):
    n, d = x2.shape
    w = ATTN_WIDTH
    o = np.cumsum([0, w, w, w, IDX_HEADS * IDX_DIM, IDX_DIM, IDX_HEADS, POOL_WIDTH, 2 * d])
    wb = w_in.astype(BF16)
    wqkv = wb[:, o[0]:o[3]]
    wqi = wb[:, o[3]:o[4]]
    wki = jnp.pad(wb[:, o[4]:o[5]], ((0, 0), (0, LANES - IDX_DIM)))
    wwi = jnp.pad(wb[:, o[5]:o[6]], ((0, 0), (0, LANES - IDX_HEADS)))
    wp = wb[:, o[6]:o[7]]
    wgl = wb[:, o[7]:o[8]]
    bd = jnp.kron(jnp.eye(N_HEADS, dtype=F32), jnp.full((HEAD_DIM, HEAD_DIM), 1.0 / HEAD_DIM, F32)).astype(BF16)
    c, s_lo, s_hi = _rope_tables(seq)
    tps = seq // tm
    const = lambda shape: pl.BlockSpec(shape, lambda i: (0,) * len(shape))
    row = lambda width: pl.BlockSpec((tm, width), lambda i: (i, 0))
    tab = pl.BlockSpec((tm, w), lambda i: (i % tps, 0))
    per_kb = kb // tm
    out_shapes = (
        jax.ShapeDtypeStruct((n, w), BF16),
        jax.ShapeDtypeStruct((n, w), BF16),
        jax.ShapeDtypeStruct((batch, seq // kb, w, kb), BF16),
        jax.ShapeDtypeStruct((n, w), BF16),
        jax.ShapeDtypeStruct((n, IDX_DIM), BF16),
        jax.ShapeDtypeStruct((n, LANES), F32),
        jax.ShapeDtypeStruct((n, POOL_WIDTH), F32),
        jax.ShapeDtypeStruct((n, 2 * d), BF16),
    )
    out_specs = (
        row(w), row(w),
        pl.BlockSpec((1, 1, w, tm), lambda i: (i // tps, (i % tps) // per_kb, 0, (i % tps) % per_kb)),
        row(w), row(IDX_DIM), row(LANES), row(POOL_WIDTH), row(2 * d),
    )
    return pl.pallas_call(
        _proj_body,
        grid=(n // tm,),
        in_specs=[row(d), const((1, d)), const(wqkv.shape), const(wqi.shape), const(wki.shape), const(wwi.shape),
                  const(wp.shape), const(wgl.shape), const((1, w)), const((1, w)), const(bd.shape), tab, tab, tab],
        out_specs=out_specs,
        out_shape=out_shapes,
        compiler_params=_params(("parallel",)),
        name="input_projection",
    )(x2, norm1_g.reshape(1, d), wqkv, wqi, wki, wwi, wp, wgl,
      jnp.tile(q_norm_g, N_HEADS).reshape(1, w), jnp.tile(k_norm_g, N_HEADS).reshape(1, w), bd, c, s_lo, s_hi)


def _sortable(v):
    b = lax.bitcast_convert_type(v, I32)
    b = jnp.where(b == INT_MIN, 0, b)
    return jnp.where(b < 0, b ^ 0x7FFFFFFF, b)


def _dsa_body(q_ref, qi_ref, wi_ref, k_ref, vt_ref, ki_ref, o_ref, key_s, bias_s, acc_s, s_s, hi_s, lo_s,
              *, seq, tq, kb, topk):
    j = pl.program_id(1)
    nblk = ((j + 1) * tq + kb - 1) // kb
    lane = lax.broadcasted_iota(I32, (1, tq), 1)
    lim = j * tq + (lane // CHUNK + 1) * CHUNK
    row_iota = lax.broadcasted_iota(I32, (kb, tq), 0)
    wi_t = wi_ref[...].T[:IDX_HEADS, :]
    qi = qi_ref[...]

    def rows(i):
        return pl.ds(pl.multiple_of(i * kb, kb), kb)

    def score_block(i, carry):
        kib = ki_ref[rows(i), :]
        acc = jnp.zeros((kb, tq), F32)
        for h in range(IDX_HEADS):
            lg = lax.dot_general(kib, qi[:, h * IDX_DIM:(h + 1) * IDX_DIM], NT_DIMS, preferred_element_type=F32)
            acc = acc + jnp.maximum(lg, 0.0) * wi_t[h:h + 1, :]
        key = jnp.where(i * kb + row_iota < lim, _sortable(acc), INT_MIN)
        key_s[rows(i), :] = key
        hi_s[rows(i), :] = (key >> HALF_BITS).astype(I16)
        lo_s[rows(i), :] = ((key & HALF_MASK) - HALF_BIAS).astype(I16)
        return carry

    lax.fori_loop(0, nblk, score_block, 0)

    def count(pred):
        def body(i, c):
            m = pred(key_s[rows(i), :], i * kb + row_iota)
            return c + jnp.sum(m.astype(I32).reshape(kb // SUBLANES, SUBLANES, tq), axis=0)
        c8 = lax.fori_loop(0, nblk, body, jnp.zeros((SUBLANES, tq), I32))
        return jnp.sum(c8, axis=0, keepdims=True)

    pack = 2 * SUBLANES
    one16, zero16 = jnp.ones((), I16), jnp.zeros((), I16)

    def spread16(v):
        return jnp.broadcast_to(v.astype(I16), (kb, tq))

    def count16(ref, pred):
        def body(i, c):
            m = pred(ref[rows(i), :])
            hit = jnp.where(m, one16, zero16)
            parts = [hit[r:r + pack, :] for r in range(0, kb, pack)]
            while len(parts) > 1:
                parts = [parts[r] + parts[r + 1] for r in range(0, len(parts), 2)]
            return c + parts[0]
        c16 = lax.fori_loop(0, nblk, body, jnp.zeros((pack, tq), I16))
        return jnp.sum(c16.astype(I32), axis=0, keepdims=True)

    def search16(ref, need):
        def bit(it, tu):
            cand_u = tu | lax.shift_left(jnp.int32(1), HALF_BITS - 1 - it)
            cand = spread16(cand_u - HALF_BIAS)
            return jnp.where(count16(ref, lambda blk: blk >= cand) >= need, cand_u, tu)
        return lax.fori_loop(0, HALF_BITS, bit, jnp.zeros((1, tq), I32))

    hi_u = search16(hi_s, topk)
    thr_hi = spread16(hi_u - HALF_BIAS)
    above = count16(hi_s, lambda blk: blk > thr_hi)

    def mask_low(i, carry):
        lo_s[rows(i), :] = jnp.where(hi_s[rows(i), :] == thr_hi, lo_s[rows(i), :], jnp.full((), -HALF_BIAS, I16))
        return carry

    lax.fori_loop(0, nblk, mask_low, 0)
    lo_u = search16(lo_s, topk - above)
    thr = lax.shift_left(hi_u - HALF_BIAS, HALF_BITS) | lo_u
    idx_bits = int(seq).bit_length()
    surplus = (count(lambda blk, idx: blk >= thr) != topk) & (thr != INT_MIN)
    has_tie = jnp.max(jnp.where(surplus, 1.0, 0.0)) > 0.5

    def resolve_ties():
        need = topk - count(lambda blk, idx: blk > thr)

        def index_bit(it, jj):
            cand = jj | lax.shift_left(jnp.int32(1), idx_bits - 1 - it)
            return jnp.where(count(lambda blk, idx: (blk == thr) & (idx < cand)) <= need, cand, jj)

        return lax.fori_loop(0, idx_bits, index_bit, jnp.zeros((1, tq), I32))

    tie_end = lax.cond(has_tie, resolve_ties, lambda: jnp.full((1, tq), (1 << idx_bits) - 1, I32))

    def bias_block(i, carry):
        blk = key_s[rows(i), :]
        idx = i * kb + row_iota
        sel = ((blk > thr) | ((blk == thr) & (idx < tie_end))) & (idx < lim)
        bias_s[rows(i), :] = jnp.where(sel, 0.0, NEG)
        return carry

    lax.fori_loop(0, nblk, bias_block, 0)

    q = q_ref[...]
    pair_lane = lax.broadcasted_iota(I32, (tq, 2 * HEAD_DIM), 1)
    qm = []
    for h in range(N_HEADS):
        pair = q[:, (h // 2) * 2 * HEAD_DIM:(h // 2 + 1) * 2 * HEAD_DIM]
        qm.append(jnp.where((pair_lane // HEAD_DIM) == (h % 2), pair, jnp.zeros_like(pair)))
    acc_s[...] = jnp.zeros_like(acc_s)
    group = s_s.shape[0]

    def fold(t):
        return t.reshape(kb // SUBLANES, SUBLANES, tq)

    for g0 in range(0, N_HEADS, group):
        heads = range(g0, g0 + group)

        def score_pass(i, ms):
            bias = bias_s[rows(i), :]
            out = []
            for hh, h in enumerate(heads):
                kblk = k_ref[rows(i), (h // 2) * 2 * HEAD_DIM:(h // 2 + 1) * 2 * HEAD_DIM]
                s = lax.dot_general(kblk, qm[h], NT_DIMS, preferred_element_type=F32) + bias
                s_s[hh, rows(i), :] = s
                out.append(jnp.maximum(ms[hh], jnp.max(fold(s), axis=0)))
            return tuple(out)

        ms = lax.fori_loop(0, nblk, score_pass, tuple(jnp.full((SUBLANES, tq), NEG, F32) for _ in heads))
        mx = [jnp.max(m, axis=0, keepdims=True) for m in ms]

        def value_pass(i, ls):
            out = []
            for hh, h in enumerate(heads):
                hs = slice(h * HEAD_DIM, (h + 1) * HEAD_DIM)
                p = jnp.exp2(s_s[hh, rows(i), :] - mx[hh])
                out.append(ls[hh] + jnp.sum(fold(p), axis=0))
                acc_s[hs, :] += jnp.dot(vt_ref[0, i, hs, :], p.astype(BF16), preferred_element_type=F32)
            return tuple(out)

        ls = lax.fori_loop(0, nblk, value_pass, tuple(jnp.zeros((SUBLANES, tq), F32) for _ in heads))
        for hh, h in enumerate(heads):
            hs = slice(h * HEAD_DIM, (h + 1) * HEAD_DIM)
            acc_s[hs, :] = acc_s[hs, :] / jnp.sum(ls[hh], axis=0, keepdims=True)
    o_ref[...] = acc_s[...].T.astype(BF16)


def _dsa_attention(q, qi, wi, k, vt, ki, batch, seq, tq, kb):
    n, w = q.shape
    topk = min(TOPK_MAX, seq // 4)
    nq = seq // tq
    tile = lambda width: pl.BlockSpec((tq, width), lambda b, j: (b * nq + j, 0))
    whole = lambda width: pl.BlockSpec((seq, width), lambda b, j: (b, 0))
    return pl.pallas_call(
        functools.partial(_dsa_body, seq=seq, tq=tq, kb=kb, topk=topk),
        grid=(batch, nq),
        in_specs=[tile(w), tile(w), tile(LANES), whole(w),
                  pl.BlockSpec((1, seq // kb, w, kb), lambda b, j: (b, 0, 0, 0)), whole(IDX_DIM)],
        out_specs=tile(w),
        out_shape=jax.ShapeDtypeStruct((n, w), BF16),
        scratch_shapes=[pltpu.VMEM((seq, tq), I32), pltpu.VMEM((seq, tq), F32), pltpu.VMEM((w, tq), F32),
                        pltpu.VMEM((ATTN_HEAD_GROUP, seq, tq), F32),
                        pltpu.VMEM((seq, tq), I16), pltpu.VMEM((seq, tq), I16)],
        compiler_params=_params(("parallel", "arbitrary")),
        name="dsa_attention",
    )(q, qi, wi, k, vt, ki)


def _mix_body(attn_ref, p_ref, halo_ref, gate_ref, x_ref, wa_ref, wpb_ref, wo_ref, pw_ref, ps_ref, g2_ref,
              h_ref, hn_ref, ext_s, *, tm, tps):
    st = pl.program_id(0) % tps
    ext_s[0:POOL_HALO, :] = jnp.where(st == 0, 0.0, halo_ref[...])
    ext_s[POOL_HALO:POOL_HALO + tm, :] = p_ref[...]
    t1 = (st * tm + 1 + lax.broadcasted_iota(I32, (tm, POOL_GROUP_DIM), 0)).astype(F32)
    mixed = []
    for g, win in enumerate(POOL_WINDOWS):
        ls = slice(g * POOL_GROUP_DIM, (g + 1) * POOL_GROUP_DIM)
        frame = ext_s[POOL_HALO:POOL_HALO + tm, ls]
        tot = frame
        for dlt in range(1, win):
            tot = tot + ext_s[POOL_HALO - dlt:POOL_HALO - dlt + tm, ls]
        pooled = tot / jnp.minimum(t1, float(win)) - frame
        mixed.append(jnp.dot(pooled.astype(BF16), pw_ref[g], preferred_element_type=F32))
    mixed = jnp.concatenate(mixed, axis=1) * ps_ref[...]
    y_pool = jnp.dot(mixed.astype(BF16), wpb_ref[...], preferred_element_type=F32)
    y_attn = jnp.dot(attn_ref[...], wa_ref[...], preferred_element_type=F32)
    d = y_attn.shape[1]
    gate = gate_ref[...].astype(F32)
    z = gate[:, :d] * y_attn + gate[:, d:] * y_pool
    h = x_ref[...] + jnp.dot(z.astype(BF16), wo_ref[...], preferred_element_type=F32)
    h_ref[...] = h
    hn = h * lax.rsqrt(jnp.mean(h * h, axis=-1, keepdims=True) + EPS) * g2_ref[...]
    hn_ref[...] = hn.astype(BF16)


def _mixer_output(attn, p, gate, x2, w_branch_attn, w_branch_pool, w_out, pool_w, pool_scale, norm2_g, seq, tm):
    n, d = x2.shape
    tps = seq // tm
    hb = tm // POOL_HALO
    const = lambda shape: pl.BlockSpec(shape, lambda i: (0,) * len(shape))
    row = lambda width: pl.BlockSpec((tm, width), lambda i: (i, 0))
    return pl.pallas_call(
        functools.partial(_mix_body, tm=tm, tps=tps),
        grid=(n // tm,),
        in_specs=[row(ATTN_WIDTH), row(POOL_WIDTH),
                  pl.BlockSpec((POOL_HALO, POOL_WIDTH), lambda i: (jnp.maximum(i * hb - 1, 0), 0)),
                  row(2 * d), row(d), const((ATTN_WIDTH, d)), const((POOL_WIDTH, d)), const((d, d)),
                  const(pool_w.shape), const((1, POOL_WIDTH)), const((1, d))],
        out_specs=(row(d), row(d)),
        out_shape=(jax.ShapeDtypeStruct((n, d), F32), jax.ShapeDtypeStruct((n, d), BF16)),
        scratch_shapes=[pltpu.VMEM((POOL_HALO + tm, POOL_WIDTH), F32)],
        compiler_params=_params(("parallel",)),
        name="mixer_output",
    )(attn, p, p, gate, x2, w_branch_attn.astype(BF16), w_branch_pool.astype(BF16), w_out.astype(BF16),
      pool_w.astype(BF16), pool_scale.reshape(1, POOL_WIDTH), norm2_g.reshape(1, d))


def _candidate_pairs():
    return [(a, b) for a in range(PEER_TOPK) for b in range(PEER_TOPK) if (a + 1) * (b + 1) <= PEER_TOPK]


def _sort_desc(v):
    v = list(v)
    n = len(v)
    k = 2
    while k <= n:
        j = k // 2
        while j >= 1:
            for i in range(n):
                m = i ^ j
                if m > i:
                    hi, lo = jnp.maximum(v[i], v[m]), jnp.minimum(v[i], v[m])
                    v[i], v[m] = (hi, lo) if (i & k) == 0 else (lo, hi)
            j //= 2
        k *= 2
    return v


def _merge_top(a, b):
    n = len(a)
    c = [jnp.maximum(a[i], b[n - 1 - i]) for i in range(n)]
    j = n // 2
    while j >= 1:
        for i in range(n):
            m = i ^ j
            if m > i:
                c[i], c[m] = jnp.maximum(c[i], c[m]), jnp.minimum(c[i], c[m])
        j //= 2
    return c


def _route_body(hn_ref, wqt_ref, kbig_ref, a0_ref, l0_ref, b1_ref, r1_ref, vals_s, rank_s, ex_s, *, tr):
    nk, nh, kt = PEER_KEYS, PEER_HEADS, PEER_TOPK
    half_rows = nh * PEER_KEY_DIM
    qt = lax.dot_general(wqt_ref[...], hn_ref[...], NT_DIMS, preferred_element_type=F32).astype(BF16)
    for p in range(2):
        sub = jnp.dot(kbig_ref[p], qt[p * half_rows:(p + 1) * half_rows], preferred_element_type=F32)
        vals_s[p] = sub.reshape(nk, nh, tr)

    def best(p, lo, hi):
        if hi - lo == kt:
            return _sort_desc([vals_s[p, i] for i in range(lo, hi)])
        mid = (lo + hi) // 2
        return _merge_top(best(p, lo, mid), best(p, mid, hi))

    tops = [best(p, 0, nk) for p in range(2)]
    v0, v1 = tops
    tied = jnp.zeros((nh, tr), F32)
    for p in range(2):
        for a in range(kt - 1):
            tied = jnp.maximum(tied, jnp.where(tops[p][a] == tops[p][a + 1], 1.0, 0.0))
        above = [jnp.where(vals_s[p, i] >= tops[p][kt - 1], 1.0, 0.0) for i in range(nk)]
        while len(above) > 1:
            above = [above[i] + above[i + 1] for i in range(0, len(above), 2)]
        tied = jnp.maximum(tied, jnp.where(above[0] != float(kt), 1.0, 0.0))
    has_tie = jnp.max(tied) > 0.5

    pairs = _candidate_pairs()
    cand = [v0[a] + v1[b] for a, b in pairs]
    rank = [jnp.zeros((nh, tr), F32) for _ in pairs]
    for ia, (a0, a1) in enumerate(pairs):
        for ib in range(ia + 1, len(pairs)):
            b0, b1 = pairs[ib]
            if a0 <= b0 and a1 <= b1:
                rank[ib] = rank[ib] + 1.0
            else:
                wins = jnp.where(cand[ia] >= cand[ib], 1.0, 0.0)
                rank[ib] = rank[ib] + wins
                rank[ia] = rank[ia] + (1.0 - wins)
    e0 = [jnp.exp(v0[a] - v0[0]) for a in range(kt)]
    e1 = [jnp.exp(v1[b] - v1[0]) for b in range(kt)]
    width = [jnp.zeros((nh, tr), F32) for _ in range(kt)]
    z = jnp.zeros((nh, tr), F32)
    for ic, (a, b) in enumerate(pairs):
        sel = jnp.where(rank[ic] < float(kt), 1.0, 0.0)
        width[a] = width[a] + sel
        z = z + sel * (e0[a] * e1[b])
    inv_z = 1.0 / z

    def key_rows(i):
        return slice(i * nh, (i + 1) * nh)

    @pl.when(jnp.logical_not(has_tie))
    def _():
        for i in range(nk):
            x0, x1 = vals_s[0, i], vals_s[1, i]
            width_i = jnp.zeros((nh, tr), F32)
            for a in range(kt):
                width_i = jnp.where(x0 == v0[a], width[a], width_i)
            a0_ref[0, key_rows(i), :] = jnp.where(x0 >= v0[kt - 1], jnp.exp(x0 - v0[0]) * inv_z, 0.0)
            l0_ref[0, key_rows(i), :] = width_i
            above8 = v1[7] > x1
            piv = jnp.where(above8, v1[11], v1[3])
            above4 = piv > x1
            piv = jnp.where(above8, jnp.where(above4, v1[13], v1[9]), jnp.where(above4, v1[5], v1[1]))
            above2 = piv > x1
            piv = jnp.where(
                above8,
                jnp.where(above4, jnp.where(above2, v1[14], v1[12]), jnp.where(above2, v1[10], v1[8])),
                jnp.where(above4, jnp.where(above2, v1[6], v1[4]), jnp.where(above2, v1[2], v1[0])))
            pos = (jnp.where(above8, 8.0, 0.0) + jnp.where(above4, 4.0, 0.0) + jnp.where(above2, 2.0, 0.0)
                   + jnp.where(piv > x1, 1.0, 0.0))
            chosen = x1 >= v1[kt - 1]
            b1_ref[0, key_rows(i), :] = jnp.where(chosen, jnp.exp(x1 - v1[0]), 0.0)
            r1_ref[0, key_rows(i), :] = jnp.where(chosen, pos, float(kt))

    @pl.when(has_tie)
    def _():
        rank_s[...] = jnp.full(rank_s.shape, float(kt), F32)
        ex_s[...] = jnp.zeros(ex_s.shape, F32)
        key_iota = lax.broadcasted_iota(I32, (nk, nh, tr), 0)

        def extract(kk, carry):
            for p in range(2):
                v = vals_s[p]
                m = jnp.max(v, axis=0)
                idx = jnp.min(jnp.where(v == m[None], key_iota, nk), axis=0)
                hit = key_iota == idx[None]
                vals_s[p] = jnp.where(hit, -jnp.inf, v)
                rank_s[p] = jnp.where(hit, lax.convert_element_type(kk, F32), rank_s[p])
                ex_s[p] = jnp.where(hit, jnp.exp(m - tops[p][0])[None], ex_s[p])
            return carry

        lax.fori_loop(0, kt, extract, 0)
        r0 = rank_s[0]
        l0 = jnp.zeros((nk, nh, tr), F32)
        for a in range(kt):
            l0 = jnp.where(r0 == float(a), width[a][None], l0)
        a0_ref[0] = (ex_s[0] * inv_z[None]).reshape(nk * nh, tr)
        l0_ref[0] = l0.reshape(nk * nh, tr)
        b1_ref[0] = ex_s[1].reshape(nk * nh, tr)
        r1_ref[0] = rank_s[1].reshape(nk * nh, tr)


def _peer_routing(hn, peer_wq, peer_subkeys, tr):
    n, d = hn.shape
    nk, nh, kd = PEER_KEYS, PEER_HEADS, PEER_KEY_DIM
    wqt = peer_wq.reshape(d, nh, 2, kd).transpose(2, 1, 3, 0).reshape(2 * nh * kd, d).astype(BF16)
    eye = jnp.eye(nh, dtype=peer_subkeys.dtype)
    kbig = jnp.einsum("hpnd,hg->pnhgd", peer_subkeys, eye).reshape(2, nk * nh, nh * kd).astype(BF16)
    rows = nk * nh
    assert tr == LANES
    out = jax.ShapeDtypeStruct((n // tr, rows, tr), F32)
    spec = pl.BlockSpec((1, rows, tr), lambda i: (i, 0, 0))
    return pl.pallas_call(
        functools.partial(_route_body, tr=tr),
        grid=(n // tr,),
        in_specs=[pl.BlockSpec((tr, d), lambda i: (i, 0)),
                  pl.BlockSpec(wqt.shape, lambda i: (0, 0)),
                  pl.BlockSpec(kbig.shape, lambda i: (0, 0, 0))],
        out_specs=(spec, spec, spec, spec),
        out_shape=(out, out, out, out),
        scratch_shapes=[pltpu.VMEM((2, nk, nh, tr), F32), pltpu.VMEM((2, nk, nh, tr), F32),
                        pltpu.VMEM((2, nk, nh, tr), F32)],
        compiler_params=_params(("parallel",)),
        name="peer_routing",
    )(hn, wqt, kbig)


def _expert_body(*refs, tm, te, n_eb):
    n_slices = te // EXPERT_SLICE
    hn_ref, h_ref = refs[:2]
    u_refs = refs[2:2 + n_slices]
    vt_refs = refs[2 + n_slices:2 + 2 * n_slices]
    a0_ref, l0_ref, b1_ref, r1_ref, y_ref, acc_s, g0_s, g1_s, ga0_s, ga1_s, br_s = refs[2 + 2 * n_slices:]
    step = pl.program_id(1)
    nk, nh = PEER_KEYS, PEER_HEADS
    gt = (g0_s, g1_s)
    ga = (ga0_s, ga1_s)
    pack = 2 * SUBLANES

    every = slice(0, te)
    slices = [slice(k * EXPERT_SLICE, (k + 1) * EXPERT_SLICE) for k in range(n_slices)]

    def project(k):
        return lax.dot_general(u_refs[k][...], hn_ref[...], NT_DIMS, preferred_element_type=F32)

    def finish(act, src, dst, rs):
        act = 0.5 * act * (1.0 + lax.erf(act * (2.0 ** -0.5)))
        dst[rs, :] = src[rs, :] * act.astype(BF16)

    def value_part(src, k):
        return jnp.dot(vt_refs[k][0], src[slices[k], :], preferred_element_type=F32)

    def gate(dst, rs=every):
        for il in range(rs.start // nk, rs.stop // nk):
            i = step * (te // nk) + il
            for c in range(tm // LANES):
                g = jnp.zeros((nk, LANES), BF16)
                for h in range(nh):
                    row = pl.ds(i * nh + h, 1)
                    a_row = jnp.broadcast_to(a0_ref[c, row, :], (pack, LANES)).astype(BF16)
                    l_row = jnp.broadcast_to(l0_ref[c, row, :], (pack, LANES)).astype(BF16)
                    a_row = jnp.tile(a_row, (nk // pack, 1))
                    l_row = jnp.tile(l_row, (nk // pack, 1))
                    b1 = br_s[c, h, :, 0].reshape(nk, LANES)
                    r1 = br_s[c, h, :, 1].reshape(nk, LANES)
                    g = g + a_row * jnp.where(r1 < l_row, b1, jnp.zeros((), BF16))
                blk = (slice(il * nk, (il + 1) * nk), slice(c * LANES, (c + 1) * LANES))
                dst[blk] = g

    def apply(src):
        total = value_part(src, 0)
        for k in range(1, n_slices):
            total = total + value_part(src, k)
        acc_s[...] += total

    def activate(src, dst):
        for k in range(n_slices):
            finish(project(k), src, dst, slices[k])

    @pl.when(step == 0)
    def _():
        acc_s[...] = jnp.zeros_like(acc_s)
        for c in range(tm // LANES):
            for h in range(nh):
                b1 = b1_ref[c, pl.ds(h, nk, stride=nh), :].astype(BF16)
                r1 = r1_ref[c, pl.ds(h, nk, stride=nh), :].astype(BF16)
                br_s[c, h, :, 0] = b1.reshape(nk // pack, pack, LANES)
                br_s[c, h, :, 1] = r1.reshape(nk // pack, pack, LANES)
        gate(gt[0])

    @pl.when(step == 1)
    def _():
        activate(gt[0], ga[0])
        gate(gt[1])

    for p in range(2):
        @pl.when((step >= 2) & (step < n_eb) & (step % 2 == p))
        def _():
            total = None
            for k, rs in enumerate(slices):
                gate(gt[p], rs)
                act = project(k)
                part = value_part(ga[p], k)
                total = part if total is None else total + part
                finish(act, gt[1 - p], ga[1 - p], rs)
            acc_s[...] += total

    @pl.when(step == n_eb)
    def _():
        p = n_eb % 2
        apply(ga[p])
        activate(gt[1 - p], ga[1 - p])

    @pl.when(step == n_eb + 1)
    def _():
        apply(ga[(n_eb + 1) % 2])
        y_ref[...] = h_ref[...] + acc_s[...].T


def _peer_experts(hn, h, peer_u, peer_v, a0, l0, b1, r1, tm, te):
    n, d = hn.shape
    ne = peer_u.shape[0]
    rows = a0.shape[1]
    tok = pl.BlockSpec((tm // LANES, rows, LANES), lambda t, e: (t, 0, 0))
    n_eb = ne // te
    n_slices = te // EXPERT_SLICE
    u_b = peer_u.astype(BF16)
    vt_slabs = peer_v.astype(BF16).reshape(ne // EXPERT_SLICE, EXPERT_SLICE, d).transpose(0, 2, 1)
    u_specs = [pl.BlockSpec((EXPERT_SLICE, d), lambda t, s, k=k: (n_slices * jnp.clip(s - 1, 0, n_eb - 1) + k, 0))
               for k in range(n_slices)]
    vt_specs = [pl.BlockSpec((1, d, EXPERT_SLICE),
                             lambda t, s, k=k: (n_slices * jnp.clip(s - 2, 0, n_eb - 1) + k, 0, 0))
                for k in range(n_slices)]
    return pl.pallas_call(
        functools.partial(_expert_body, tm=tm, te=te, n_eb=n_eb),
        grid=(n // tm, n_eb + 2),
        in_specs=[pl.BlockSpec((tm, d), lambda t, s: (t, 0)), pl.BlockSpec((tm, d), lambda t, s: (t, 0)),
                  *u_specs, *vt_specs, tok, tok, tok, tok],
        out_specs=pl.BlockSpec((tm, d), lambda t, s: (t, 0)),
        out_shape=jax.ShapeDtypeStruct((n, d), F32),
        scratch_shapes=[pltpu.VMEM((d, tm), F32), pltpu.VMEM((te, tm), BF16), pltpu.VMEM((te, tm), BF16),
                        pltpu.VMEM((te, tm), BF16), pltpu.VMEM((te, tm), BF16),
                        pltpu.VMEM((tm // LANES, PEER_HEADS, PEER_KEYS // (2 * SUBLANES), 2, 2 * SUBLANES, LANES),
                                   BF16)],
        compiler_params=_params(("parallel", "arbitrary")),
        name="peer_experts",
    )(hn, h, *([u_b] * n_slices), *([vt_slabs] * n_slices), a0, l0, b1, r1)


def _tiles(batch, seq):
    return dict(tm=256, tq=256, kb=512, tr=128, te_tm=512, te=1024)


def kernel(x, norm1_g, w_in, q_norm_g, k_norm_g, pool_w, pool_scale, w_branch_attn, w_branch_pool, w_out, norm2_g,
           peer_wq, peer_subkeys, peer_u, peer_v):
    batch, seq, d = x.shape
    t = _tiles(batch, seq)
    x2 = x.reshape(batch * seq, d)
    for l in range(norm1_g.shape[0]):
        q, k, vt, qi, ki, wi, p, gate = _input_projection(
            x2, norm1_g[l], w_in[l], q_norm_g[l], k_norm_g[l], batch, seq, t["tm"], t["kb"])
        attn = _dsa_attention(q, qi, wi, k, vt, ki, batch, seq, t["tq"], t["kb"])
        h, hn = _mixer_output(attn, p, gate, x2, w_branch_attn[l], w_branch_pool[l], w_out[l], pool_w[l],
                              pool_scale[l], norm2_g[l], seq, t["tm"])
        a0, l0, b1, r1 = _peer_routing(hn, peer_wq[l], peer_subkeys[l], t["tr"])
        x2 = _peer_experts(hn, h, peer_u[l], peer_v[l], a0, l0, b1, r1, t["te_tm"], t["te"])
    return x2.reshape(batch, seq, d)
```

```python
import functools

import jax
import jax.numpy as jnp
import numpy as np
from jax import lax
from jax.experimental import pallas as pl
from jax.experimental.pallas import tpu as pltpu

CHUNK = 64
EPS = 1e-6
N_HEADS = 8
HEAD_DIM = 64
ATTN_WIDTH = N_HEADS * HEAD_DIM
ROT_HALF = HEAD_DIM // 8
ROPE_THETA = 500000.0
IDX_HEADS = 8
IDX_DIM = 64
TOPK_MAX = 256
POOL_WINDOWS = (2, 4, 8, 16)
POOL_WIDTH = 512
POOL_GROUP_DIM = POOL_WIDTH // len(POOL_WINDOWS)
POOL_HALO = 16
PEER_HEADS = 8
PEER_KEYS = 128
PEER_KEY_DIM = 64
PEER_TOPK = 16

LANES = 128
SUBLANES = 8
VMEM_LIMIT = 56 * 1024 * 1024

F32 = jnp.float32
BF16 = jnp.bfloat16
I32 = jnp.int32
I16 = jnp.int16
INT_MIN = -2147483648
HALF_BITS = 16
HALF_MASK = 0xFFFF
HALF_BIAS = 32768
NEG = -1e30
LOG2_E = 1.4426950408889634
ATTN_HEAD_GROUP = 4
EXPERT_SLICE = 256
NT_DIMS = (((1,), (1,)), ((), ()))


def _params(sem):
    return pltpu.CompilerParams(dimension_semantics=sem, vmem_limit_bytes=VMEM_LIMIT)


def _rope(t, c, s_lo, s_hi):
    w = t.shape[-1]
    return t * c + pltpu.roll(t, w - ROT_HALF, 1) * s_lo + pltpu.roll(t, ROT_HALF, 1) * s_hi


def _proj_body(x_ref, g1_ref, wqkv_ref, wqi_ref, wki_ref, wwi_ref, wp_ref, wgl_ref, qg_ref, kg_ref, bd_ref,
               c_ref, slo_ref, shi_ref,
               q_ref, k_ref, vt_ref, qi_ref, ki_ref, wi_ref, p_ref, gate_ref):
    x = x_ref[...]
    xn = x * lax.rsqrt(jnp.mean(x * x, axis=-1, keepdims=True) + EPS) * g1_ref[...]
    xb = xn.astype(BF16)
    c, s_lo, s_hi = c_ref[...], slo_ref[...], shi_ref[...]

    def head_norm(t, g):
        ms = jnp.dot((t * t).astype(BF16), bd_ref[...], preferred_element_type=F32)
        return t * lax.rsqrt(ms + EPS) * g

    qkv = jnp.dot(xb, wqkv_ref[...], preferred_element_type=F32)
    w = ATTN_WIDTH
    q = _rope(head_norm(qkv[:, :w], qg_ref[...]), c, s_lo, s_hi) * (HEAD_DIM ** -0.5 * LOG2_E)
    k = _rope(head_norm(qkv[:, w:2 * w], kg_ref[...]), c, s_lo, s_hi)
    q_ref[...] = q.astype(BF16)
    k_ref[...] = k.astype(BF16)
    vt_ref[0, 0] = qkv[:, 2 * w:].T.astype(BF16)
    qi = jnp.dot(xb, wqi_ref[...], preferred_element_type=F32)
    qi_ref[...] = (_rope(qi, c, s_lo, s_hi) * (IDX_DIM ** -0.5)).astype(BF16)
    ki = jnp.dot(xb, wki_ref[...], preferred_element_type=F32)
    ki = _rope(ki, c[:, :LANES], s_lo[:, :LANES], s_hi[:, :LANES])
    ki_ref[...] = ki[:, :IDX_DIM].astype(BF16)
    wi_ref[...] = jnp.dot(xb, wwi_ref[...], preferred_element_type=F32) * (IDX_HEADS ** -0.5)
    p_ref[...] = jnp.dot(xb, wp_ref[...], preferred_element_type=F32)
    gate_ref[...] = jax.nn.sigmoid(jnp.dot(xb, wgl_ref[...], preferred_element_type=F32)).astype(BF16)


def _rope_tables(seq):
    inv_freq = ROPE_THETA ** (-jnp.arange(ROT_HALF, dtype=F32) / ROT_HALF)
    ang = jnp.arange(seq, dtype=F32)[:, None] * inv_freq[None, :]
    cos, sin = jnp.cos(ang), jnp.sin(ang)
    rest = HEAD_DIM - 2 * ROT_HALF
    ones = jnp.ones((seq, rest), F32)
    zeros = jnp.zeros((seq, rest), F32)
    zh = jnp.zeros((seq, ROT_HALF), F32)
    c = jnp.concatenate([cos, cos, ones], axis=1)
    s_lo = jnp.concatenate([-sin, zh, zeros], axis=1)
    s_hi = jnp.concatenate([zh, sin, zeros], axis=1)
    tile = lambda t: jnp.tile(t, (1, N_HEADS))
    return tile(c), tile(s_lo), tile(s_hi)


def _input_projection(x2, norm1_g, w_in, q_norm_g, k_norm_g, batch, seq, tm, kb):
    n, d = x2.shape
    w = ATTN_WIDTH
    o = np.cumsum([0, w, w, w, IDX_HEADS * IDX_DIM, IDX_DIM, IDX_HEADS, POOL_WIDTH, 2 * d])
    wb = w_in.astype(BF16)
    wqkv = wb[:, o[0]:o[3]]
    wqi = wb[:, o[3]:o[4]]
    wki = jnp.pad(wb[:, o[4]:o[5]], ((0, 0), (0, LANES - IDX_DIM)))
    wwi = jnp.pad(wb[:, o[5]:o[6]], ((0, 0), (0, LANES - IDX_HEADS)))
    wp = wb[:, o[6]:o[7]]
    wgl = wb[:, o[7]:o[8]]
    bd = jnp.kron(jnp.eye(N_HEADS, dtype=F32), jnp.full((HEAD_DIM, HEAD_DIM), 1.0 / HEAD_DIM, F32)).astype(BF16)
    c, s_lo, s_hi = _rope_tables(seq)
    tps = seq // tm
    const = lambda shape: pl.BlockSpec(shape, lambda i: (0,) * len(shape))
    row = lambda width: pl.BlockSpec((tm, width), lambda i: (i, 0))
    tab = pl.BlockSpec((tm, w), lambda i: (i % tps, 0))
    per_kb = kb // tm
    out_shapes = (
        jax.ShapeDtypeStruct((n, w), BF16),
        jax.ShapeDtypeStruct((n, w), BF16),
        jax.ShapeDtypeStruct((batch, seq // kb, w, kb), BF16),
        jax.ShapeDtypeStruct((n, w), BF16),
        jax.ShapeDtypeStruct((n, IDX_DIM), BF16),
        jax.ShapeDtypeStruct((n, LANES), F32),
        jax.ShapeDtypeStruct((n, POOL_WIDTH), F32),
        jax.ShapeDtypeStruct((n, 2 * d), BF16),
    )
    out_specs = (
        row(w), row(w),
        pl.BlockSpec((1, 1, w, tm), lambda i: (i // tps, (i % tps) // per_kb, 0, (i % tps) % per_kb)),
        row(w), row(IDX_DIM), row(LANES), row(POOL_WIDTH), row(2 * d),
    )
    return pl.pallas_call(
        _proj_body,
        grid=(n // tm,),
        in_specs=[row(d), const((1, d)), const(wqkv.shape), const(wqi.shape), const(wki.shape), const(wwi.shape),
                  const(wp.shape), const(wgl.shape), const((1, w)), const((1, w)), const(bd.shape), tab, tab, tab],
        out_specs=out_specs,
        out_shape=out_shapes,
        compiler_params=_params(("parallel",)),
        name="input_projection",
    )(x2, norm1_g.reshape(1, d), wqkv, wqi, wki, wwi, wp, wgl,
      jnp.tile(q_norm_g, N_HEADS).reshape(1, w), jnp.tile(k_norm_g, N_HEADS).reshape(1, w), bd, c, s_lo, s_hi)


def _sortable(v):
    b = lax.bitcast_convert_type(v, I32)
    b = jnp.where(b == INT_MIN, 0, b)
    return jnp.where(b < 0, b ^ 0x7FFFFFFF, b)


def _dsa_body(q_ref, qi_ref, wi_ref, k_ref, vt_ref, ki_ref, o_ref, key_s, bias_s, acc_s, s_s, hi_s, lo_s,
              *, seq, tq, kb, topk):
    j = pl.program_id(1)
    nblk = ((j + 1) * tq + kb - 1) // kb
    lane = lax.broadcasted_iota(I32, (1, tq), 1)
    lim = j * tq + (lane // CHUNK + 1) * CHUNK
    row_iota = lax.broadcasted_iota(I32, (kb, tq), 0)
    wi_t = wi_ref[...].T[:IDX_HEADS, :]
    qi = qi_ref[...]

    def rows(i):
        return pl.ds(pl.multiple_of(i * kb, kb), kb)

    def score_block(i, carry):
        kib = ki_ref[rows(i), :]
        acc = jnp.zeros((kb, tq), F32)
        for h in range(IDX_HEADS):
            lg = lax.dot_general(kib, qi[:, h * IDX_DIM:(h + 1) * IDX_DIM], NT_DIMS, preferred_element_type=F32)
            acc = acc + jnp.maximum(lg, 0.0) * wi_t[h:h + 1, :]
        key = jnp.where(i * kb + row_iota < lim, _sortable(acc), INT_MIN)
        key_s[rows(i), :] = key
        hi_s[rows(i), :] = (key >> HALF_BITS).astype(I16)
        lo_s[rows(i), :] = ((key & HALF_MASK) - HALF_BIAS).astype(I16)
        return carry

    lax.fori_loop(0, nblk, score_block, 0)

    def count(pred):
        def body(i, c):
            m = pred(key_s[rows(i), :], i * kb + row_iota)
            return c + jnp.sum(m.astype(I32).reshape(kb // SUBLANES, SUBLANES, tq), axis=0)
        c8 = lax.fori_loop(0, nblk, body, jnp.zeros((SUBLANES, tq), I32))
        return jnp.sum(c8, axis=0, keepdims=True)

    pack = 2 * SUBLANES
    one16, zero16 = jnp.ones((), I16), jnp.zeros((), I16)

    def spread16(v):
        return jnp.broadcast_to(v.astype(I16), (kb, tq))

    def count16(ref, pred):
        def body(i, c):
            m = pred(ref[rows(i), :])
            hit = jnp.where(m, one16, zero16)
            parts = [hit[r:r + pack, :] for r in range(0, kb, pack)]
            while len(parts) > 1:
                parts = [parts[r] + parts[r + 1] for r in range(0, len(parts), 2)]
            return c + parts[0]
        c16 = lax.fori_loop(0, nblk, body, jnp.zeros((pack, tq), I16))
        return jnp.sum(c16.astype(I32), axis=0, keepdims=True)

    def search16(ref, need):
        def bit(it, tu):
            cand_u = tu | lax.shift_left(jnp.int32(1), HALF_BITS - 1 - it)
            cand = spread16(cand_u - HALF_BIAS)
            return jnp.where(count16(ref, lambda blk: blk >= cand) >= need, cand_u, tu)
        return lax.fori_loop(0, HALF_BITS, bit, jnp.zeros((1, tq), I32))

    hi_u = search16(hi_s, topk)
    thr_hi = spread16(hi_u - HALF_BIAS)
    above = count16(hi_s, lambda blk: blk > thr_hi)

    def mask_low(i, carry):
        lo_s[rows(i), :] = jnp.where(hi_s[rows(i), :] == thr_hi, lo_s[rows(i), :], jnp.full((), -HALF_BIAS, I16))
        return carry

    lax.fori_loop(0, nblk, mask_low, 0)
    lo_u = search16(lo_s, topk - above)
    thr = lax.shift_left(hi_u - HALF_BIAS, HALF_BITS) | lo_u
    idx_bits = int(seq).bit_length()
    surplus = (count(lambda blk, idx: blk >= thr) != topk) & (thr != INT_MIN)
    has_tie = jnp.max(jnp.where(surplus, 1.0, 0.0)) > 0.5

    def resolve_ties():
        need = topk - count(lambda blk, idx: blk > thr)

        def index_bit(it, jj):
            cand = jj | lax.shift_left(jnp.int32(1), idx_bits - 1 - it)
            return jnp.where(count(lambda blk, idx: (blk == thr) & (idx < cand)) <= need, cand, jj)

        return lax.fori_loop(0, idx_bits, index_bit, jnp.zeros((1, tq), I32))

    tie_end = lax.cond(has_tie, resolve_ties, lambda: jnp.full((1, tq), (1 << idx_bits) - 1, I32))

    def bias_block(i, carry):
        blk = key_s[rows(i), :]
        idx = i * kb + row_iota
        sel = ((blk > thr) | ((blk == thr) & (idx < tie_end))) & (idx < lim)
        bias_s[rows(i), :] = jnp.where(sel, 0.0, NEG)
        return carry

    lax.fori_loop(0, nblk, bias_block, 0)

    q = q_ref[...]
    pair_lane = lax.broadcasted_iota(I32, (tq, 2 * HEAD_DIM), 1)
    qm = []
    for h in range(N_HEADS):
        pair = q[:, (h // 2) * 2 * HEAD_DIM:(h // 2 + 1) * 2 * HEAD_DIM]
        qm.append(jnp.where((pair_lane // HEAD_DIM) == (h % 2), pair, jnp.zeros_like(pair)))
    acc_s[...] = jnp.zeros_like(acc_s)
    group = s_s.shape[0]

    def fold(t):
        return t.reshape(kb // SUBLANES, SUBLANES, tq)

    for g0 in range(0, N_HEADS, group):
        heads = range(g0, g0 + group)

        def score_pass(i, ms):
            bias = bias_s[rows(i), :]
            out = []
            for hh, h in enumerate(heads):
                kblk = k_ref[rows(i), (h // 2) * 2 * HEAD_DIM:(h // 2 + 1) * 2 * HEAD_DIM]
                s = lax.dot_general(kblk, qm[h], NT_DIMS, preferred_element_type=F32) + bias
                s_s[hh, rows(i), :] = s
                out.append(jnp.maximum(ms[hh], jnp.max(fold(s), axis=0)))
            return tuple(out)

        ms = lax.fori_loop(0, nblk, score_pass, tuple(jnp.full((SUBLANES, tq), NEG, F32) for _ in heads))
        mx = [jnp.max(m, axis=0, keepdims=True) for m in ms]

        def value_pass(i, ls):
            out = []
            for hh, h in enumerate(heads):
                hs = slice(h * HEAD_DIM, (h + 1) * HEAD_DIM)
                p = jnp.exp2(s_s[hh, rows(i), :] - mx[hh])
                out.append(ls[hh] + jnp.sum(fold(p), axis=0))
                acc_s[hs, :] += jnp.dot(vt_ref[0, i, hs, :], p.astype(BF16), preferred_element_type=F32)
            return tuple(out)

        ls = lax.fori_loop(0, nblk, value_pass, tuple(jnp.zeros((SUBLANES, tq), F32) for _ in heads))
        for hh, h in enumerate(heads):
            hs = slice(h * HEAD_DIM, (h + 1) * HEAD_DIM)
            acc_s[hs, :] = acc_s[hs, :] / jnp.sum(ls[hh], axis=0, keepdims=True)
    o_ref[...] = acc_s[...].T.astype(BF16)


def _dsa_attention(q, qi, wi, k, vt, ki, batch, seq, tq, kb):
    n, w = q.shape
    topk = min(TOPK_MAX, seq // 4)
    nq = seq // tq
    tile = lambda width: pl.BlockSpec((tq, width), lambda b, j: (b * nq + j, 0))
    whole = lambda width: pl.BlockSpec((seq, width), lambda b, j: (b, 0))
    return pl.pallas_call(
        functools.partial(_dsa_body, seq=seq, tq=tq, kb=kb, topk=topk),
        grid=(batch, nq),
        in_specs=[tile(w), tile(w), tile(LANES), whole(w),
                  pl.BlockSpec((1, seq // kb, w, kb), lambda b, j: (b, 0, 0, 0)), whole(IDX_DIM)],
        out_specs=tile(w),
        out_shape=jax.ShapeDtypeStruct((n, w), BF16),
        scratch_shapes=[pltpu.VMEM((seq, tq), I32), pltpu.VMEM((seq, tq), F32), pltpu.VMEM((w, tq), F32),
                        pltpu.VMEM((ATTN_HEAD_GROUP, seq, tq), F32),
                        pltpu.VMEM((seq, tq), I16), pltpu.VMEM((seq, tq), I16)],
        compiler_params=_params(("parallel", "arbitrary")),
        name="dsa_attention",
    )(q, qi, wi, k, vt, ki)


def _mix_body(attn_ref, p_ref, halo_ref, gate_ref, x_ref, wa_ref, wpb_ref, wo_ref, pw_ref, ps_ref, g2_ref,
              h_ref, hn_ref, ext_s, *, tm, tps):
    st = pl.program_id(0) % tps
    ext_s[0:POOL_HALO, :] = jnp.where(st == 0, 0.0, halo_ref[...])
    ext_s[POOL_HALO:POOL_HALO + tm, :] = p_ref[...]
    t1 = (st * tm + 1 + lax.broadcasted_iota(I32, (tm, POOL_GROUP_DIM), 0)).astype(F32)
    mixed = []
    for g, win in enumerate(POOL_WINDOWS):
        ls = slice(g * POOL_GROUP_DIM, (g + 1) * POOL_GROUP_DIM)
        frame = ext_s[POOL_HALO:POOL_HALO + tm, ls]
        tot = frame
        for dlt in range(1, win):
            tot = tot + ext_s[POOL_HALO - dlt:POOL_HALO - dlt + tm, ls]
        pooled = tot / jnp.minimum(t1, float(win)) - frame
        mixed.append(jnp.dot(pooled.astype(BF16), pw_ref[g], preferred_element_type=F32))
    mixed = jnp.concatenate(mixed, axis=1) * ps_ref[...]
    y_pool = jnp.dot(mixed.astype(BF16), wpb_ref[...], preferred_element_type=F32)
    y_attn = jnp.dot(attn_ref[...], wa_ref[...], preferred_element_type=F32)
    d = y_attn.shape[1]
    gate = gate_ref[...].astype(F32)
    z = gate[:, :d] * y_attn + gate[:, d:] * y_pool
    h = x_ref[...] + jnp.dot(z.astype(BF16), wo_ref[...], preferred_element_type=F32)
    h_ref[...] = h
    hn = h * lax.rsqrt(jnp.mean(h * h, axis=-1, keepdims=True) + EPS) * g2_ref[...]
    hn_ref[...] = hn.astype(BF16)


def _mixer_output(attn, p, gate, x2, w_branch_attn, w_branch_pool, w_out, pool_w, pool_scale, norm2_g, seq, tm):
    n, d = x2.shape
    tps = seq // tm
    hb = tm // POOL_HALO
    const = lambda shape: pl.BlockSpec(shape, lambda i: (0,) * len(shape))
    row = lambda width: pl.BlockSpec((tm, width), lambda i: (i, 0))
    return pl.pallas_call(
        functools.partial(_mix_body, tm=tm, tps=tps),
        grid=(n // tm,),
        in_specs=[row(ATTN_WIDTH), row(POOL_WIDTH),
                  pl.BlockSpec((POOL_HALO, POOL_WIDTH), lambda i: (jnp.maximum(i * hb - 1, 0), 0)),
                  row(2 * d), row(d), const((ATTN_WIDTH, d)), const((POOL_WIDTH, d)), const((d, d)),
                  const(pool_w.shape), const((1, POOL_WIDTH)), const((1, d))],
        out_specs=(row(d), row(d)),
        out_shape=(jax.ShapeDtypeStruct((n, d), F32), jax.ShapeDtypeStruct((n, d), BF16)),
        scratch_shapes=[pltpu.VMEM((POOL_HALO + tm, POOL_WIDTH), F32)],
        compiler_params=_params(("parallel",)),
        name="mixer_output",
    )(attn, p, p, gate, x2, w_branch_attn.astype(BF16), w_branch_pool.astype(BF16), w_out.astype(BF16),
      pool_w.astype(BF16), pool_scale.reshape(1, POOL_WIDTH), norm2_g.reshape(1, d))


def _candidate_pairs():
    return [(a, b) for a in range(PEER_TOPK) for b in range(PEER_TOPK) if (a + 1) * (b + 1) <= PEER_TOPK]


def _sort_desc(v):
    v = list(v)
    n = len(v)
    k = 2
    while k <= n:
        j = k // 2
        while j >= 1:
            for i in range(n):
                m = i ^ j
                if m > i:
                    hi, lo = jnp.maximum(v[i], v[m]), jnp.minimum(v[i], v[m])
                    v[i], v[m] = (hi, lo) if (i & k) == 0 else (lo, hi)
            j //= 2
        k *= 2
    return v


def _merge_top(a, b):
    n = len(a)
    c = [jnp.maximum(a[i], b[n - 1 - i]) for i in range(n)]
    j = n // 2
    while j >= 1:
        for i in range(n):
            m = i ^ j
            if m > i:
                c[i], c[m] = jnp.maximum(c[i], c[m]), jnp.minimum(c[i], c[m])
        j //= 2
    return c


def _route_body(hn_ref, wqt_ref, kbig_ref, a0_ref, l0_ref, b1_ref, r1_ref, vals_s, rank_s, ex_s, *, tr):
    nk, nh, kt = PEER_KEYS, PEER_HEADS, PEER_TOPK
    half_rows = nh * PEER_KEY_DIM
    qt = lax.dot_general(wqt_ref[...], hn_ref[...], NT_DIMS, preferred_element_type=F32).astype(BF16)
    for p in range(2):
        sub = jnp.dot(kbig_ref[p], qt[p * half_rows:(p + 1) * half_rows], preferred_element_type=F32)
        vals_s[p] = sub.reshape(nk, nh, tr)

    def best(p, lo, hi):
        if hi - lo == kt:
            return _sort_desc([vals_s[p, i] for i in range(lo, hi)])
        mid = (lo + hi) // 2
        return _merge_top(best(p, lo, mid), best(p, mid, hi))

    tops = [best(p, 0, nk) for p in range(2)]
    v0, v1 = tops
    tied = jnp.zeros((nh, tr), F32)
    for p in range(2):
        for a in range(kt - 1):
            tied = jnp.maximum(tied, jnp.where(tops[p][a] == tops[p][a + 1], 1.0, 0.0))
        above = [jnp.where(vals_s[p, i] >= tops[p][kt - 1], 1.0, 0.0) for i in range(nk)]
        while len(above) > 1:
            above = [above[i] + above[i + 1] for i in range(0, len(above), 2)]
        tied = jnp.maximum(tied, jnp.where(above[0] != float(kt), 1.0, 0.0))
    has_tie = jnp.max(tied) > 0.5

    pairs = _candidate_pairs()
    cand = [v0[a] + v1[b] for a, b in pairs]
    rank = [jnp.zeros((nh, tr), F32) for _ in pairs]
    for ia, (a0, a1) in enumerate(pairs):
        for ib in range(ia + 1, len(pairs)):
            b0, b1 = pairs[ib]
            if a0 <= b0 and a1 <= b1:
                rank[ib] = rank[ib] + 1.0
            else:
                wins = jnp.where(cand[ia] >= cand[ib], 1.0, 0.0)
                rank[ib] = rank[ib] + wins
                rank[ia] = rank[ia] + (1.0 - wins)
    e0 = [jnp.exp(v0[a] - v0[0]) for a in range(kt)]
    e1 = [jnp.exp(v1[b] - v1[0]) for b in range(kt)]
    width = [jnp.zeros((nh, tr), F32) for _ in range(kt)]
    z = jnp.zeros((nh, tr), F32)
    for ic, (a, b) in enumerate(pairs):
        sel = jnp.where(rank[ic] < float(kt), 1.0, 0.0)
        width[a] = width[a] + sel
        z = z + sel * (e0[a] * e1[b])
    inv_z = 1.0 / z

    def key_rows(i):
        return slice(i * nh, (i + 1) * nh)

    @pl.when(jnp.logical_not(has_tie))
    def _():
        for i in range(nk):
            x0, x1 = vals_s[0, i], vals_s[1, i]
            width_i = jnp.zeros((nh, tr), F32)
            for a in range(kt):
                width_i = jnp.where(x0 == v0[a], width[a], width_i)
            a0_ref[0, key_rows(i), :] = jnp.where(x0 >= v0[kt - 1], jnp.exp(x0 - v0[0]) * inv_z, 0.0)
            l0_ref[0, key_rows(i), :] = width_i
            above8 = v1[7] > x1
            piv = jnp.where(above8, v1[11], v1[3])
            above4 = piv > x1
            piv = jnp.where(above8, jnp.where(above4, v1[13], v1[9]), jnp.where(above4, v1[5], v1[1]))
            above2 = piv > x1
            piv = jnp.where(
                above8,
                jnp.where(above4, jnp.where(above2, v1[14], v1[12]), jnp.where(above2, v1[10], v1[8])),
                jnp.where(above4, jnp.where(above2, v1[6], v1[4]), jnp.where(above2, v1[2], v1[0])))
            pos = (jnp.where(above8, 8.0, 0.0) + jnp.where(above4, 4.0, 0.0) + jnp.where(above2, 2.0, 0.0)
                   + jnp.where(piv > x1, 1.0, 0.0))
            chosen = x1 >= v1[kt - 1]
            b1_ref[0, key_rows(i), :] = jnp.where(chosen, jnp.exp(x1 - v1[0]), 0.0)
            r1_ref[0, key_rows(i), :] = jnp.where(chosen, pos, float(kt))

    @pl.when(has_tie)
    def _():
        rank_s[...] = jnp.full(rank_s.shape, float(kt), F32)
        ex_s[...] = jnp.zeros(ex_s.shape, F32)
        key_iota = lax.broadcasted_iota(I32, (nk, nh, tr), 0)

        def extract(kk, carry):
            for p in range(2):
                v = vals_s[p]
                m = jnp.max(v, axis=0)
                idx = jnp.min(jnp.where(v == m[None], key_iota, nk), axis=0)
                hit = key_iota == idx[None]
                vals_s[p] = jnp.where(hit, -jnp.inf, v)
                rank_s[p] = jnp.where(hit, lax.convert_element_type(kk, F32), rank_s[p])
                ex_s[p] = jnp.where(hit, jnp.exp(m - tops[p][0])[None], ex_s[p])
            return carry

        lax.fori_loop(0, kt, extract, 0)
        r0 = rank_s[0]
        l0 = jnp.zeros((nk, nh, tr), F32)
        for a in range(kt):
            l0 = jnp.where(r0 == float(a), width[a][None], l0)
        a0_ref[0] = (ex_s[0] * inv_z[None]).reshape(nk * nh, tr)
        l0_ref[0] = l0.reshape(nk * nh, tr)
        b1_ref[0] = ex_s[1].reshape(nk * nh, tr)
        r1_ref[0] = rank_s[1].reshape(nk * nh, tr)


def _peer_routing(hn, peer_wq, peer_subkeys, tr):
    n, d = hn.shape
    nk, nh, kd = PEER_KEYS, PEER_HEADS, PEER_KEY_DIM
    wqt = peer_wq.reshape(d, nh, 2, kd).transpose(2, 1, 3, 0).reshape(2 * nh * kd, d).astype(BF16)
    eye = jnp.eye(nh, dtype=peer_subkeys.dtype)
    kbig = jnp.einsum("hpnd,hg->pnhgd", peer_subkeys, eye).reshape(2, nk * nh, nh * kd).astype(BF16)
    rows = nk * nh
    assert tr == LANES
    out = jax.ShapeDtypeStruct((n // tr, rows, tr), F32)
    spec = pl.BlockSpec((1, rows, tr), lambda i: (i, 0, 0))
    return pl.pallas_call(
        functools.partial(_route_body, tr=tr),
        grid=(n // tr,),
        in_specs=[pl.BlockSpec((tr, d), lambda i: (i, 0)),
                  pl.BlockSpec(wqt.shape, lambda i: (0, 0)),
                  pl.BlockSpec(kbig.shape, lambda i: (0, 0, 0))],
        out_specs=(spec, spec, spec, spec),
        out_shape=(out, out, out, out),
        scratch_shapes=[pltpu.VMEM((2, nk, nh, tr), F32), pltpu.VMEM((2, nk, nh, tr), F32),
                        pltpu.VMEM((2, nk, nh, tr), F32)],
        compiler_params=_params(("parallel",)),
        name="peer_routing",
    )(hn, wqt, kbig)


def _expert_body(*refs, tm, te, n_eb):
    n_slices = te // EXPERT_SLICE
    hn_ref, h_ref = refs[:2]
    u_refs = refs[2:2 + n_slices]
    vt_refs = refs[2 + n_slices:2 + 2 * n_slices]
    a0_ref, l0_ref, b1_ref, r1_ref, y_ref, acc_s, g_s, ga_s, br_s = refs[2 + 2 * n_slices:]
    step = pl.program_id(1)
    nk, nh = PEER_KEYS, PEER_HEADS
    pack = 2 * SUBLANES

    def gt(slot):
        return g_s.at[slot]

    def ga(slot):
        return ga_s.at[slot]

    every = slice(0, te)
    slices = [slice(k * EXPERT_SLICE, (k + 1) * EXPERT_SLICE) for k in range(n_slices)]

    def project(k):
        return lax.dot_general(u_refs[k][...], hn_ref[...], NT_DIMS, preferred_element_type=F32)

    def finish(act, src, dst, rs):
        act = 0.5 * act * (1.0 + lax.erf(act * (2.0 ** -0.5)))
        dst[rs, :] = src[rs, :] * act.astype(BF16)

    def value_part(src, k):
        return jnp.dot(vt_refs[k][0], src[slices[k], :], preferred_element_type=F32)

    def gate(dst, rs=every):
        for il in range(rs.start // nk, rs.stop // nk):
            i = step * (te // nk) + il
            for c in range(tm // LANES):
                g = jnp.zeros((nk, LANES), BF16)
                for h in range(nh):
                    row = pl.ds(i * nh + h, 1)
                    a_row = jnp.broadcast_to(a0_ref[c, row, :], (pack, LANES)).astype(BF16)
                    l_row = jnp.broadcast_to(l0_ref[c, row, :], (pack, LANES)).astype(BF16)
                    a_row = jnp.tile(a_row, (nk // pack, 1))
                    l_row = jnp.tile(l_row, (nk // pack, 1))
                    b1 = br_s[c, h, :, 0].reshape(nk, LANES)
                    r1 = br_s[c, h, :, 1].reshape(nk, LANES)
                    g = g + a_row * jnp.where(r1 < l_row, b1, jnp.zeros((), BF16))
                blk = (slice(il * nk, (il + 1) * nk), slice(c * LANES, (c + 1) * LANES))
                dst[blk] = g

    def apply(src):
        total = value_part(src, 0)
        for k in range(1, n_slices):
            total = total + value_part(src, k)
        acc_s[...] += total

    def activate(src, dst):
        for k in range(n_slices):
            finish(project(k), src, dst, slices[k])

    @pl.when(step == 0)
    def _():
        acc_s[...] = jnp.zeros_like(acc_s)
        for c in range(tm // LANES):
            for h in range(nh):
                b1 = b1_ref[c, pl.ds(h, nk, stride=nh), :].astype(BF16)
                r1 = r1_ref[c, pl.ds(h, nk, stride=nh), :].astype(BF16)
                br_s[c, h, :, 0] = b1.reshape(nk // pack, pack, LANES)
                br_s[c, h, :, 1] = r1.reshape(nk // pack, pack, LANES)
        gate(gt(0))

    @pl.when(step == 1)
    def _():
        activate(gt(0), ga(0))
        gate(gt(1))

    @pl.when((step >= 2) & (step < n_eb))
    def _():
        p = step % 2
        total = None
        for k, rs in enumerate(slices):
            gate(gt(p), rs)
            act = project(k)
            part = value_part(ga(p), k)
            total = part if total is None else total + part
            finish(act, gt(1 - p), ga(1 - p), rs)
        acc_s[...] += total

    @pl.when(step == n_eb)
    def _():
        p = n_eb % 2
        apply(ga(p))
        activate(gt(1 - p), ga(1 - p))

    @pl.when(step == n_eb + 1)
    def _():
        apply(ga((n_eb + 1) % 2))
        y_ref[...] = h_ref[...] + acc_s[...].T


def _peer_experts(hn, h, peer_u, peer_v, a0, l0, b1, r1, tm, te):
    n, d = hn.shape
    ne = peer_u.shape[0]
    rows = a0.shape[1]
    tok = pl.BlockSpec((tm // LANES, rows, LANES), lambda t, e: (t, 0, 0))
    n_eb = ne // te
    n_slices = te // EXPERT_SLICE
    u_b = peer_u.astype(BF16)
    vt_slabs = peer_v.astype(BF16).reshape(ne // EXPERT_SLICE, EXPERT_SLICE, d).transpose(0, 2, 1)
    u_specs = [pl.BlockSpec((EXPERT_SLICE, d), lambda t, s, k=k: (n_slices * jnp.clip(s - 1, 0, n_eb - 1) + k, 0))
               for k in range(n_slices)]
    vt_specs = [pl.BlockSpec((1, d, EXPERT_SLICE),
                             lambda t, s, k=k: (n_slices * jnp.clip(s - 2, 0, n_eb - 1) + k, 0, 0))
                for k in range(n_slices)]
    return pl.pallas_call(
        functools.partial(_expert_body, tm=tm, te=te, n_eb=n_eb),
        grid=(n // tm, n_eb + 2),
        in_specs=[pl.BlockSpec((tm, d), lambda t, s: (t, 0)), pl.BlockSpec((tm, d), lambda t, s: (t, 0)),
                  *u_specs, *vt_specs, tok, tok, tok, tok],
        out_specs=pl.BlockSpec((tm, d), lambda t, s: (t, 0)),
        out_shape=jax.ShapeDtypeStruct((n, d), F32),
        scratch_shapes=[pltpu.VMEM((d, tm), F32), pltpu.VMEM((2, te, tm), BF16), pltpu.VMEM((2, te, tm), BF16),
                        pltpu.VMEM((tm // LANES, PEER_HEADS, PEER_KEYS // (2 * SUBLANES), 2, 2 * SUBLANES, LANES),
                                   BF16)],
        compiler_params=_params(("parallel", "arbitrary")),
        name="peer_experts",
    )(hn, h, *([u_b] * n_slices), *([vt_slabs] * n_slices), a0, l0, b1, r1)


def _tiles(batch, seq):
    return dict(tm=256, tq=256, kb=512, tr=128, te_tm=512, te=1024)


def kernel(x, norm1_g, w_in, q_norm_g, k_norm_g, pool_w, pool_scale, w_branch_attn, w_branch_pool, w_out, norm2_g,
           peer_wq, peer_subkeys, peer_u, peer_v):
    batch, seq, d = x.shape
    t = _tiles(batch, seq)
    x2 = x.reshape(batch * seq, d)
    for l in range(norm1_g.shape[0]):
        q, k, vt, qi, ki, wi, p, gate = _input_projection(
            x2, norm1_g[l], w_in[l], q_norm_g[l], k_norm_g[l], batch, seq, t["tm"], t["kb"])
        attn = _dsa_attention(q, qi, wi, k, vt, ki, batch, seq, t["tq"], t["kb"])
        h, hn = _mixer_output(attn, p, gate, x2, w_branch_attn[l], w_branch_pool[l], w_out[l], pool_w[l],
                              pool_scale[l], norm2_g[l], seq, t["tm"])
        a0, l0, b1, r1 = _peer_routing(hn, peer_wq[l], peer_subkeys[l], t["tr"])
        x2 = _peer_experts(hn, h, peer_u[l], peer_v[l], a0, l0, b1, r1, t["te_tm"], t["te"])
    return x2.reshape(batch, seq, d)
```

```python
import functools

import jax
import jax.numpy as jnp
import numpy as np
from jax import lax
from jax.experimental import pallas as pl
from jax.experimental.pallas import tpu as pltpu

CHUNK = 64
EPS = 1e-6
N_HEADS = 8
HEAD_DIM = 64
ATTN_WIDTH = N_HEADS * HEAD_DIM
ROT_HALF = HEAD_DIM // 8
ROPE_THETA = 500000.0
IDX_HEADS = 8
IDX_DIM = 64
TOPK_MAX = 256
POOL_WINDOWS = (2, 4, 8, 16)
POOL_WIDTH = 512
POOL_GROUP_DIM = POOL_WIDTH // len(POOL_WINDOWS)
POOL_HALO = 16
PEER_HEADS = 8
PEER_KEYS = 128
PEER_KEY_DIM = 64
PEER_TOPK = 16

LANES = 128
SUBLANES = 8
VMEM_LIMIT = 56 * 1024 * 1024

F32 = jnp.float32
BF16 = jnp.bfloat16
I32 = jnp.int32
I16 = jnp.int16
INT_MIN = -2147483648
HALF_BITS = 16
HALF_MASK = 0xFFFF
HALF_BIAS = 32768
NEG = -1e30
LOG2_E = 1.4426950408889634
ATTN_HEAD_GROUP = 4
EXPERT_SLICE = 256
NT_DIMS = (((1,), (1,)), ((), ()))


def _params(sem):
    return pltpu.CompilerParams(dimension_semantics=sem, vmem_limit_bytes=VMEM_LIMIT)


def _rope(t, c, s_lo, s_hi):
    w = t.shape[-1]
    return t * c + pltpu.roll(t, w - ROT_HALF, 1) * s_lo + pltpu.roll(t, ROT_HALF, 1) * s_hi


def _proj_body(x_ref, g1_ref, wqkv_ref, wqi_ref, wki_ref, wwi_ref, wp_ref, wgl_ref, qg_ref, kg_ref, bd_ref,
               c_ref, slo_ref, shi_ref,
               q_ref, k_ref, vt_ref, qi_ref, ki_ref, wi_ref, p_ref, gate_ref):
    x = x_ref[...]
    xn = x * lax.rsqrt(jnp.mean(x * x, axis=-1, keepdims=True) + EPS) * g1_ref[...]
    xb = xn.astype(BF16)
    c, s_lo, s_hi = c_ref[...], slo_ref[...], shi_ref[...]

    def head_norm(t, g):
        ms = jnp.dot((t * t).astype(BF16), bd_ref[...], preferred_element_type=F32)
        return t * lax.rsqrt(ms + EPS) * g

    qkv = jnp.dot(xb, wqkv_ref[...], preferred_element_type=F32)
    w = ATTN_WIDTH
    q = _rope(head_norm(qkv[:, :w], qg_ref[...]), c, s_lo, s_hi) * (HEAD_DIM ** -0.5 * LOG2_E)
    k = _rope(head_norm(qkv[:, w:2 * w], kg_ref[...]), c, s_lo, s_hi)
    q_ref[...] = q.astype(BF16)
    k_ref[...] = k.astype(BF16)
    vt_ref[0, 0] = qkv[:, 2 * w:].T.astype(BF16)
    qi = jnp.dot(xb, wqi_ref[...], preferred_element_type=F32)
    qi_ref[...] = (_rope(qi, c, s_lo, s_hi) * (IDX_DIM ** -0.5)).astype(BF16)
    ki = jnp.dot(xb, wki_ref[...], preferred_element_type=F32)
    ki = _rope(ki, c[:, :LANES], s_lo[:, :LANES], s_hi[:, :LANES])
    ki_ref[...] = ki[:, :IDX_DIM].astype(BF16)
    wi_ref[...] = jnp.dot(xb, wwi_ref[...], preferred_element_type=F32) * (IDX_HEADS ** -0.5)
    p_ref[...] = jnp.dot(xb, wp_ref[...], preferred_element_type=F32)
    gate_ref[...] = jax.nn.sigmoid(jnp.dot(xb, wgl_ref[...], preferred_element_type=F32)).astype(BF16)


def _rope_tables(seq):
    inv_freq = ROPE_THETA ** (-jnp.arange(ROT_HALF, dtype=F32) / ROT_HALF)
    ang = jnp.arange(seq, dtype=F32)[:, None] * inv_freq[None, :]
    cos, sin = jnp.cos(ang), jnp.sin(ang)
    rest = HEAD_DIM - 2 * ROT_HALF
    ones = jnp.ones((seq, rest), F32)
    zeros = jnp.zeros((seq, rest), F32)
    zh = jnp.zeros((seq, ROT_HALF), F32)
    c = jnp.concatenate([cos, cos, ones], axis=1)
    s_lo = jnp.concatenate([-sin, zh, zeros], axis=1)
    s_hi = jnp.concatenate([zh, sin, zeros], axis=1)
    tile = lambda t: jnp.tile(t, (1, N_HEADS))
    return tile(c), tile(s_lo), tile(s_hi)


def _input_projection(x2, norm1_g, w_in, q_norm_g, k_norm_g, batch, seq, tm, kb):
    n, d = x2.shape
    w = ATTN_WIDTH
    o = np.cumsum([0, w, w, w, IDX_HEADS * IDX_DIM, IDX_DIM, IDX_HEADS, POOL_WIDTH, 2 * d])
    wb = w_in.astype(BF16)
    wqkv = wb[:, o[0]:o[3]]
    wqi = wb[:, o[3]:o[4]]
    wki = jnp.pad(wb[:, o[4]:o[5]], ((0, 0), (0, LANES - IDX_DIM)))
    wwi = jnp.pad(wb[:, o[5]:o[6]], ((0, 0), (0, LANES - IDX_HEADS)))
    wp = wb[:, o[6]:o[7]]
    wgl = wb[:, o[7]:o[8]]
    bd = jnp.kron(jnp.eye(N_HEADS, dtype=F32), jnp.full((HEAD_DIM, HEAD_DIM), 1.0 / HEAD_DIM, F32)).astype(BF16)
    c, s_lo, s_hi = _rope_tables(seq)
    tps = seq // tm
    const = lambda shape: pl.BlockSpec(shape, lambda i: (0,) * len(shape))
    row = lambda width: pl.BlockSpec((tm, width), lambda i: (i, 0))
    tab = pl.BlockSpec((tm, w), lambda i: (i % tps, 0))
    per_kb = kb // tm
    out_shapes = (
        jax.ShapeDtypeStruct((n, w), BF16),
        jax.ShapeDtypeStruct((n, w), BF16),
        jax.ShapeDtypeStruct((batch, seq // kb, w, kb), BF16),
        jax.ShapeDtypeStruct((n, w), BF16),
        jax.ShapeDtypeStruct((n, IDX_DIM), BF16),
        jax.ShapeDtypeStruct((n, LANES), F32),
        jax.ShapeDtypeStruct((n, POOL_WIDTH), F32),
        jax.ShapeDtypeStruct((n, 2 * d), BF16),
    )
    out_specs = (
        row(w), row(w),
        pl.BlockSpec((1, 1, w, tm), lambda i: (i // tps, (i % tps) // per_kb, 0, (i % tps) % per_kb)),
        row(w), row(IDX_DIM), row(LANES), row(POOL_WIDTH), row(2 * d),
    )
    return pl.pallas_call(
        _proj_body,
        grid=(n // tm,),
        in_specs=[row(d), const((1, d)), const(wqkv.shape), const(wqi.shape), const(wki.shape), const(wwi.shape),
                  const(wp.shape), const(wgl.shape), const((1, w)), const((1, w)), const(bd.shape), tab, tab, tab],
        out_specs=out_specs,
        out_shape=out_shapes,
        compiler_params=_params(("parallel",)),
        name="input_projection",
    )(x2, norm1_g.reshape(1, d), wqkv, wqi, wki, wwi, wp, wgl,
      jnp.tile(q_norm_g, N_HEADS).reshape(1, w), jnp.tile(k_norm_g, N_HEADS).reshape(1, w), bd, c, s_lo, s_hi)


def _sortable(v):
    b = lax.bitcast_convert_type(v, I32)
    b = jnp.where(b == INT_MIN, 0, b)
    return jnp.where(b < 0, b ^ 0x7FFFFFFF, b)


def _dsa_body(q_ref, qi_ref, wi_ref, k_ref, vt_ref, ki_ref, o_ref, key_s, bias_s, acc_s, s_s, hi_s, lo_s,
              *, seq, tq, kb, topk):
    j = pl.program_id(1)
    nblk = ((j + 1) * tq + kb - 1) // kb
    lane = lax.broadcasted_iota(I32, (1, tq), 1)
    lim = j * tq + (lane // CHUNK + 1) * CHUNK
    row_iota = lax.broadcasted_iota(I32, (kb, tq), 0)
    wi_t = wi_ref[...].T[:IDX_HEADS, :]
    qi = qi_ref[...]

    def lanes_to_queries(t):
        return t.astype(F32).T.astype(BF16)

    qi_t = [lanes_to_queries(qi[:, h * IDX_DIM:(h + 1) * IDX_DIM]) for h in range(IDX_HEADS)]

    def rows(i):
        return pl.ds(pl.multiple_of(i * kb, kb), kb)

    def score_block(i, carry):
        kib = ki_ref[rows(i), :]
        acc = jnp.zeros((kb, tq), F32)
        for h in range(IDX_HEADS):
            lg = jnp.dot(kib, qi_t[h], preferred_element_type=F32)
            acc = acc + jnp.maximum(lg, 0.0) * wi_t[h:h + 1, :]
        key = jnp.where(i * kb + row_iota < lim, _sortable(acc), INT_MIN)
        key_s[rows(i), :] = key
        hi_s[rows(i), :] = (key >> HALF_BITS).astype(I16)
        lo_s[rows(i), :] = ((key & HALF_MASK) - HALF_BIAS).astype(I16)
        return carry

    lax.fori_loop(0, nblk, score_block, 0)

    def count(pred):
        def body(i, c):
            m = pred(key_s[rows(i), :], i * kb + row_iota)
            return c + jnp.sum(m.astype(I32).reshape(kb // SUBLANES, SUBLANES, tq), axis=0)
        c8 = lax.fori_loop(0, nblk, body, jnp.zeros((SUBLANES, tq), I32))
        return jnp.sum(c8, axis=0, keepdims=True)

    pack = 2 * SUBLANES
    one16, zero16 = jnp.ones((), I16), jnp.zeros((), I16)

    def spread16(v):
        return jnp.broadcast_to(v.astype(I16), (kb, tq))

    def count16(ref, pred):
        def body(i, c):
            m = pred(ref[rows(i), :])
            hit = jnp.where(m, one16, zero16)
            parts = [hit[r:r + pack, :] for r in range(0, kb, pack)]
            while len(parts) > 1:
                parts = [parts[r] + parts[r + 1] for r in range(0, len(parts), 2)]
            return c + parts[0]
        c16 = lax.fori_loop(0, nblk, body, jnp.zeros((pack, tq), I16))
        return jnp.sum(c16.astype(I32), axis=0, keepdims=True)

    def search16(ref, need):
        def bit(it, tu):
            cand_u = tu | lax.shift_left(jnp.int32(1), HALF_BITS - 1 - it)
            cand = spread16(cand_u - HALF_BIAS)
            return jnp.where(count16(ref, lambda blk: blk >= cand) >= need, cand_u, tu)
        return lax.fori_loop(0, HALF_BITS, bit, jnp.zeros((1, tq), I32))

    hi_u = search16(hi_s, topk)
    thr_hi = spread16(hi_u - HALF_BIAS)
    above = count16(hi_s, lambda blk: blk > thr_hi)

    def mask_low(i, carry):
        lo_s[rows(i), :] = jnp.where(hi_s[rows(i), :] == thr_hi, lo_s[rows(i), :], jnp.full((), -HALF_BIAS, I16))
        return carry

    lax.fori_loop(0, nblk, mask_low, 0)
    lo_u = search16(lo_s, topk - above)
    thr = lax.shift_left(hi_u - HALF_BIAS, HALF_BITS) | lo_u
    idx_bits = int(seq).bit_length()
    surplus = (count(lambda blk, idx: blk >= thr) != topk) & (thr != INT_MIN)
    has_tie = jnp.max(jnp.where(surplus, 1.0, 0.0)) > 0.5

    def resolve_ties():
        need = topk - count(lambda blk, idx: blk > thr)

        def index_bit(it, jj):
            cand = jj | lax.shift_left(jnp.int32(1), idx_bits - 1 - it)
            return jnp.where(count(lambda blk, idx: (blk == thr) & (idx < cand)) <= need, cand, jj)

        return lax.fori_loop(0, idx_bits, index_bit, jnp.zeros((1, tq), I32))

    tie_end = lax.cond(has_tie, resolve_ties, lambda: jnp.full((1, tq), (1 << idx_bits) - 1, I32))

    def bias_block(i, carry):
        blk = key_s[rows(i), :]
        idx = i * kb + row_iota
        sel = ((blk > thr) | ((blk == thr) & (idx < tie_end))) & (idx < lim)
        bias_s[rows(i), :] = jnp.where(sel, 0.0, NEG)
        return carry

    lax.fori_loop(0, nblk, bias_block, 0)

    q = q_ref[...]
    pair_lane = lax.broadcasted_iota(I32, (tq, 2 * HEAD_DIM), 1)
    qm = []
    for h in range(N_HEADS):
        pair = q[:, (h // 2) * 2 * HEAD_DIM:(h // 2 + 1) * 2 * HEAD_DIM]
        qm.append(lanes_to_queries(jnp.where((pair_lane // HEAD_DIM) == (h % 2), pair, jnp.zeros_like(pair))))
    acc_s[...] = jnp.zeros_like(acc_s)
    group = s_s.shape[0]

    def fold(t):
        return t.reshape(kb // SUBLANES, SUBLANES, tq)

    for g0 in range(0, N_HEADS, group):
        heads = range(g0, g0 + group)

        def score_pass(i, ms):
            bias = bias_s[rows(i), :]
            out = []
            for hh, h in enumerate(heads):
                kblk = k_ref[rows(i), (h // 2) * 2 * HEAD_DIM:(h // 2 + 1) * 2 * HEAD_DIM]
                s = jnp.dot(kblk, qm[h], preferred_element_type=F32) + bias
                s_s[hh, rows(i), :] = s
                out.append(jnp.maximum(ms[hh], jnp.max(fold(s), axis=0)))
            return tuple(out)

        ms = lax.fori_loop(0, nblk, score_pass, tuple(jnp.full((SUBLANES, tq), NEG, F32) for _ in heads))
        mx = [jnp.max(m, axis=0, keepdims=True) for m in ms]

        def value_pass(i, ls):
            out = []
            for hh, h in enumerate(heads):
                hs = slice(h * HEAD_DIM, (h + 1) * HEAD_DIM)
                p = jnp.exp2(s_s[hh, rows(i), :] - mx[hh])
                out.append(ls[hh] + jnp.sum(fold(p), axis=0))
                acc_s[hs, :] += jnp.dot(vt_ref[0, i, hs, :], p.astype(BF16), preferred_element_type=F32)
            return tuple(out)

        ls = lax.fori_loop(0, nblk, value_pass, tuple(jnp.zeros((SUBLANES, tq), F32) for _ in heads))
        for hh, h in enumerate(heads):
            hs = slice(h * HEAD_DIM, (h + 1) * HEAD_DIM)
            acc_s[hs, :] = acc_s[hs, :] / jnp.sum(ls[hh], axis=0, keepdims=True)
    o_ref[...] = acc_s[...].T.astype(BF16)


def _dsa_attention(q, qi, wi, k, vt, ki, batch, seq, tq, kb):
    n, w = q.shape
    topk = min(TOPK_MAX, seq // 4)
    nq = seq // tq
    tile = lambda width: pl.BlockSpec((tq, width), lambda b, j: (b * nq + j, 0))
    whole = lambda width: pl.BlockSpec((seq, width), lambda b, j: (b, 0))
    return pl.pallas_call(
        functools.partial(_dsa_body, seq=seq, tq=tq, kb=kb, topk=topk),
        grid=(batch, nq),
        in_specs=[tile(w), tile(w), tile(LANES), whole(w),
                  pl.BlockSpec((1, seq // kb, w, kb), lambda b, j: (b, 0, 0, 0)), whole(IDX_DIM)],
        out_specs=tile(w),
        out_shape=jax.ShapeDtypeStruct((n, w), BF16),
        scratch_shapes=[pltpu.VMEM((seq, tq), I32), pltpu.VMEM((seq, tq), F32), pltpu.VMEM((w, tq), F32),
                        pltpu.VMEM((ATTN_HEAD_GROUP, seq, tq), F32),
                        pltpu.VMEM((seq, tq), I16), pltpu.VMEM((seq, tq), I16)],
        compiler_params=_params(("parallel", "arbitrary")),
        name="dsa_attention",
    )(q, qi, wi, k, vt, ki)


def _mix_body(attn_ref, p_ref, halo_ref, gate_ref, x_ref, wa_ref, wpb_ref, wo_ref, pw_ref, ps_ref, g2_ref,
              h_ref, hn_ref, ext_s, *, tm, tps):
    st = pl.program_id(0) % tps
    ext_s[0:POOL_HALO, :] = jnp.where(st == 0, 0.0, halo_ref[...])
    ext_s[POOL_HALO:POOL_HALO + tm, :] = p_ref[...]
    t1 = (st * tm + 1 + lax.broadcasted_iota(I32, (tm, POOL_GROUP_DIM), 0)).astype(F32)
    mixed = []
    for g, win in enumerate(POOL_WINDOWS):
        ls = slice(g * POOL_GROUP_DIM, (g + 1) * POOL_GROUP_DIM)
        frame = ext_s[POOL_HALO:POOL_HALO + tm, ls]
        tot = frame
        for dlt in range(1, win):
            tot = tot + ext_s[POOL_HALO - dlt:POOL_HALO - dlt + tm, ls]
        pooled = tot / jnp.minimum(t1, float(win)) - frame
        mixed.append(jnp.dot(pooled.astype(BF16), pw_ref[g], preferred_element_type=F32))
    mixed = jnp.concatenate(mixed, axis=1) * ps_ref[...]
    y_pool = jnp.dot(mixed.astype(BF16), wpb_ref[...], preferred_element_type=F32)
    y_attn = jnp.dot(attn_ref[...], wa_ref[...], preferred_element_type=F32)
    d = y_attn.shape[1]
    gate = gate_ref[...].astype(F32)
    z = gate[:, :d] * y_attn + gate[:, d:] * y_pool
    h = x_ref[...] + jnp.dot(z.astype(BF16), wo_ref[...], preferred_element_type=F32)
    h_ref[...] = h
    hn = h * lax.rsqrt(jnp.mean(h * h, axis=-1, keepdims=True) + EPS) * g2_ref[...]
    hn_ref[...] = hn.astype(BF16)


def _mixer_output(attn, p, gate, x2, w_branch_attn, w_branch_pool, w_out, pool_w, pool_scale, norm2_g, seq, tm):
    n, d = x2.shape
    tps = seq // tm
    hb = tm // POOL_HALO
    const = lambda shape: pl.BlockSpec(shape, lambda i: (0,) * len(shape))
    row = lambda width: pl.BlockSpec((tm, width), lambda i: (i, 0))
    return pl.pallas_call(
        functools.partial(_mix_body, tm=tm, tps=tps),
        grid=(n // tm,),
        in_specs=[row(ATTN_WIDTH), row(POOL_WIDTH),
                  pl.BlockSpec((POOL_HALO, POOL_WIDTH), lambda i: (jnp.maximum(i * hb - 1, 0), 0)),
                  row(2 * d), row(d), const((ATTN_WIDTH, d)), const((POOL_WIDTH, d)), const((d, d)),
                  const(pool_w.shape), const((1, POOL_WIDTH)), const((1, d))],
        out_specs=(row(d), row(d)),
        out_shape=(jax.ShapeDtypeStruct((n, d), F32), jax.ShapeDtypeStruct((n, d), BF16)),
        scratch_shapes=[pltpu.VMEM((POOL_HALO + tm, POOL_WIDTH), F32)],
        compiler_params=_params(("parallel",)),
        name="mixer_output",
    )(attn, p, p, gate, x2, w_branch_attn.astype(BF16), w_branch_pool.astype(BF16), w_out.astype(BF16),
      pool_w.astype(BF16), pool_scale.reshape(1, POOL_WIDTH), norm2_g.reshape(1, d))


def _candidate_pairs():
    return [(a, b) for a in range(PEER_TOPK) for b in range(PEER_TOPK) if (a + 1) * (b + 1) <= PEER_TOPK]


def _sort_desc(v):
    v = list(v)
    n = len(v)
    k = 2
    while k <= n:
        j = k // 2
        while j >= 1:
            for i in range(n):
                m = i ^ j
                if m > i:
                    hi, lo = jnp.maximum(v[i], v[m]), jnp.minimum(v[i], v[m])
                    v[i], v[m] = (hi, lo) if (i & k) == 0 else (lo, hi)
            j //= 2
        k *= 2
    return v


def _merge_top(a, b):
    n = len(a)
    c = [jnp.maximum(a[i], b[n - 1 - i]) for i in range(n)]
    j = n // 2
    while j >= 1:
        for i in range(n):
            m = i ^ j
            if m > i:
                c[i], c[m] = jnp.maximum(c[i], c[m]), jnp.minimum(c[i], c[m])
        j //= 2
    return c


def _route_body(hn_ref, wqt_ref, kbig_ref, a0_ref, l0_ref, b1_ref, r1_ref, vals_s, rank_s, ex_s, *, tr):
    nk, nh, kt = PEER_KEYS, PEER_HEADS, PEER_TOPK
    half_rows = nh * PEER_KEY_DIM
    qt = lax.dot_general(wqt_ref[...], hn_ref[...], NT_DIMS, preferred_element_type=F32).astype(BF16)
    for p in range(2):
        sub = jnp.dot(kbig_ref[p], qt[p * half_rows:(p + 1) * half_rows], preferred_element_type=F32)
        vals_s[p] = sub.reshape(nk, nh, tr)

    def best(p, lo, hi):
        if hi - lo == kt:
            return _sort_desc([vals_s[p, i] for i in range(lo, hi)])
        mid = (lo + hi) // 2
        return _merge_top(best(p, lo, mid), best(p, mid, hi))

    tops = [best(p, 0, nk) for p in range(2)]
    v0, v1 = tops
    tied = jnp.zeros((nh, tr), F32)
    for p in range(2):
        for a in range(kt - 1):
            tied = jnp.maximum(tied, jnp.where(tops[p][a] == tops[p][a + 1], 1.0, 0.0))
        above = [jnp.where(vals_s[p, i] >= tops[p][kt - 1], 1.0, 0.0) for i in range(nk)]
        while len(above) > 1:
            above = [above[i] + above[i + 1] for i in range(0, len(above), 2)]
        tied = jnp.maximum(tied, jnp.where(above[0] != float(kt), 1.0, 0.0))
    has_tie = jnp.max(tied) > 0.5

    pairs = _candidate_pairs()
    cand = [v0[a] + v1[b] for a, b in pairs]
    rank = [jnp.zeros((nh, tr), F32) for _ in pairs]
    for ia, (a0, a1) in enumerate(pairs):
        for ib in range(ia + 1, len(pairs)):
            b0, b1 = pairs[ib]
            if a0 <= b0 and a1 <= b1:
                rank[ib] = rank[ib] + 1.0
            else:
                wins = jnp.where(cand[ia] >= cand[ib], 1.0, 0.0)
                rank[ib] = rank[ib] + wins
                rank[ia] = rank[ia] + (1.0 - wins)
    e0 = [jnp.exp(v0[a] - v0[0]) for a in range(kt)]
    e1 = [jnp.exp(v1[b] - v1[0]) for b in range(kt)]
    width = [jnp.zeros((nh, tr), F32) for _ in range(kt)]
    z = jnp.zeros((nh, tr), F32)
    for ic, (a, b) in enumerate(pairs):
        sel = jnp.where(rank[ic] < float(kt), 1.0, 0.0)
        width[a] = width[a] + sel
        z = z + sel * (e0[a] * e1[b])
    inv_z = 1.0 / z

    def key_rows(i):
        return slice(i * nh, (i + 1) * nh)

    @pl.when(jnp.logical_not(has_tie))
    def _():
        for i in range(nk):
            x0, x1 = vals_s[0, i], vals_s[1, i]
            width_i = jnp.zeros((nh, tr), F32)
            for a in range(kt):
                width_i = jnp.where(x0 == v0[a], width[a], width_i)
            a0_ref[0, key_rows(i), :] = jnp.where(x0 >= v0[kt - 1], jnp.exp(x0 - v0[0]) * inv_z, 0.0)
            l0_ref[0, key_rows(i), :] = width_i
            above8 = v1[7] > x1
            piv = jnp.where(above8, v1[11], v1[3])
            above4 = piv > x1
            piv = jnp.where(above8, jnp.where(above4, v1[13], v1[9]), jnp.where(above4, v1[5], v1[1]))
            above2 = piv > x1
            piv = jnp.where(
                above8,
                jnp.where(above4, jnp.where(above2, v1[14], v1[12]), jnp.where(above2, v1[10], v1[8])),
                jnp.where(above4, jnp.where(above2, v1[6], v1[4]), jnp.where(above2, v1[2], v1[0])))
            pos = (jnp.where(above8, 8.0, 0.0) + jnp.where(above4, 4.0, 0.0) + jnp.where(above2, 2.0, 0.0)
                   + jnp.where(piv > x1, 1.0, 0.0))
            chosen = x1 >= v1[kt - 1]
            b1_ref[0, key_rows(i), :] = jnp.where(chosen, jnp.exp(x1 - v1[0]), 0.0)
            r1_ref[0, key_rows(i), :] = jnp.where(chosen, pos, float(kt))

    @pl.when(has_tie)
    def _():
        rank_s[...] = jnp.full(rank_s.shape, float(kt), F32)
        ex_s[...] = jnp.zeros(ex_s.shape, F32)
        key_iota = lax.broadcasted_iota(I32, (nk, nh, tr), 0)

        def extract(kk, carry):
            for p in range(2):
                v = vals_s[p]
                m = jnp.max(v, axis=0)
                idx = jnp.min(jnp.where(v == m[None], key_iota, nk), axis=0)
                hit = key_iota == idx[None]
                vals_s[p] = jnp.where(hit, -jnp.inf, v)
                rank_s[p] = jnp.where(hit, lax.convert_element_type(kk, F32), rank_s[p])
                ex_s[p] = jnp.where(hit, jnp.exp(m - tops[p][0])[None], ex_s[p])
            return carry

        lax.fori_loop(0, kt, extract, 0)
        r0 = rank_s[0]
        l0 = jnp.zeros((nk, nh, tr), F32)
        for a in range(kt):
            l0 = jnp.where(r0 == float(a), width[a][None], l0)
        a0_ref[0] = (ex_s[0] * inv_z[None]).reshape(nk * nh, tr)
        l0_ref[0] = l0.reshape(nk * nh, tr)
        b1_ref[0] = ex_s[1].reshape(nk * nh, tr)
        r1_ref[0] = rank_s[1].reshape(nk * nh, tr)


def _peer_routing(hn, peer_wq, peer_subkeys, tr):
    n, d = hn.shape
    nk, nh, kd = PEER_KEYS, PEER_HEADS, PEER_KEY_DIM
    wqt = peer_wq.reshape(d, nh, 2, kd).transpose(2, 1, 3, 0).reshape(2 * nh * kd, d).astype(BF16)
    eye = jnp.eye(nh, dtype=peer_subkeys.dtype)
    kbig = jnp.einsum("hpnd,hg->pnhgd", peer_subkeys, eye).reshape(2, nk * nh, nh * kd).astype(BF16)
    rows = nk * nh
    assert tr == LANES
    out = jax.ShapeDtypeStruct((n // tr, rows, tr), F32)
    spec = pl.BlockSpec((1, rows, tr), lambda i: (i, 0, 0))
    return pl.pallas_call(
        functools.partial(_route_body, tr=tr),
        grid=(n // tr,),
        in_specs=[pl.BlockSpec((tr, d), lambda i: (i, 0)),
                  pl.BlockSpec(wqt.shape, lambda i: (0, 0)),
                  pl.BlockSpec(kbig.shape, lambda i: (0, 0, 0))],
        out_specs=(spec, spec, spec, spec),
        out_shape=(out, out, out, out),
        scratch_shapes=[pltpu.VMEM((2, nk, nh, tr), F32), pltpu.VMEM((2, nk, nh, tr), F32),
                        pltpu.VMEM((2, nk, nh, tr), F32)],
        compiler_params=_params(("parallel",)),
        name="peer_routing",
    )(hn, wqt, kbig)


def _expert_body(*refs, tm, te, n_eb):
    n_slices = te // EXPERT_SLICE
    hn_ref, h_ref = refs[:2]
    u_refs = refs[2:2 + n_slices]
    vt_refs = refs[2 + n_slices:2 + 2 * n_slices]
    a0_ref, l0_ref, b1_ref, r1_ref, y_ref, acc_s, g0_s, g1_s, ga0_s, ga1_s, br_s, hnt_s = refs[2 + 2 * n_slices:]
    step = pl.program_id(1)
    nk, nh = PEER_KEYS, PEER_HEADS
    pack = 2 * SUBLANES

    def gt(slot):
        return (g0_s, g1_s)[slot]

    def ga(slot):
        return (ga0_s, ga1_s)[slot]

    every = slice(0, te)
    slices = [slice(k * EXPERT_SLICE, (k + 1) * EXPERT_SLICE) for k in range(n_slices)]

    def project(k):
        return jnp.dot(u_refs[k][...], hnt_s[...], preferred_element_type=F32)

    def finish(act, src, dst, rs):
        act = 0.5 * act * (1.0 + lax.erf(act * (2.0 ** -0.5)))
        dst[rs, :] = src[rs, :] * act.astype(BF16)

    def value_part(src, k):
        return jnp.dot(vt_refs[k][0], src[slices[k], :], preferred_element_type=F32)

    def gate(dst, rs=every):
        for il in range(rs.start // nk, rs.stop // nk):
            i = step * (te // nk) + il
            for c in range(tm // LANES):
                g = jnp.zeros((nk, LANES), BF16)
                for h in range(nh):
                    row = pl.ds(i * nh + h, 1)
                    a_row = jnp.broadcast_to(a0_ref[c, row, :], (pack, LANES)).astype(BF16)
                    l_row = jnp.broadcast_to(l0_ref[c, row, :], (pack, LANES)).astype(BF16)
                    a_row = jnp.tile(a_row, (nk // pack, 1))
                    l_row = jnp.tile(l_row, (nk // pack, 1))
                    b1 = br_s[c, h, :, 0].reshape(nk, LANES)
                    r1 = br_s[c, h, :, 1].reshape(nk, LANES)
                    g = g + a_row * jnp.where(r1 < l_row, b1, jnp.zeros((), BF16))
                blk = (slice(il * nk, (il + 1) * nk), slice(c * LANES, (c + 1) * LANES))
                dst[blk] = g

    def apply(src):
        total = value_part(src, 0)
        for k in range(1, n_slices):
            total = total + value_part(src, k)
        acc_s[...] += total

    def activate(src, dst):
        for k in range(n_slices):
            finish(project(k), src, dst, slices[k])

    @pl.when(step == 0)
    def _():
        acc_s[...] = jnp.zeros_like(acc_s)
        hnt_s[...] = hn_ref[...].astype(F32).T.astype(BF16)
        for c in range(tm // LANES):
            for h in range(nh):
                b1 = b1_ref[c, pl.ds(h, nk, stride=nh), :].astype(BF16)
                r1 = r1_ref[c, pl.ds(h, nk, stride=nh), :].astype(BF16)
                br_s[c, h, :, 0] = b1.reshape(nk // pack, pack, LANES)
                br_s[c, h, :, 1] = r1.reshape(nk // pack, pack, LANES)
        gate(gt(0))

    @pl.when(step == 1)
    def _():
        activate(gt(0), ga(0))
        gate(gt(1))

    for p in range(2):
        @pl.when((step >= 2) & (step < n_eb) & (step % 2 == p))
        def _():
            total = None
            for k, rs in enumerate(slices):
                gate(gt(p), rs)
                act = project(k)
                part = value_part(ga(p), k)
                total = part if total is None else total + part
                finish(act, gt(1 - p), ga(1 - p), rs)
            acc_s[...] += total

    @pl.when(step == n_eb)
    def _():
        p = n_eb % 2
        apply(ga(p))
        activate(gt(1 - p), ga(1 - p))

    @pl.when(step == n_eb + 1)
    def _():
        apply(ga((n_eb + 1) % 2))
        y_ref[...] = h_ref[...] + acc_s[...].T


def _peer_experts(hn, h, peer_u, peer_v, a0, l0, b1, r1, tm, te):
    n, d = hn.shape
    ne = peer_u.shape[0]
    rows = a0.shape[1]
    tok = pl.BlockSpec((tm // LANES, rows, LANES), lambda t, e: (t, 0, 0))
    n_eb = ne // te
    n_slices = te // EXPERT_SLICE
    u_b = peer_u.astype(BF16)
    vt_slabs = peer_v.astype(BF16).reshape(ne // EXPERT_SLICE, EXPERT_SLICE, d).transpose(0, 2, 1)
    u_specs = [pl.BlockSpec((EXPERT_SLICE, d), lambda t, s, k=k: (n_slices * jnp.clip(s - 1, 0, n_eb - 1) + k, 0))
               for k in range(n_slices)]
    vt_specs = [pl.BlockSpec((1, d, EXPERT_SLICE),
                             lambda t, s, k=k: (n_slices * jnp.clip(s - 2, 0, n_eb - 1) + k, 0, 0))
                for k in range(n_slices)]
    return pl.pallas_call(
        functools.partial(_expert_body, tm=tm, te=te, n_eb=n_eb),
        grid=(n // tm, n_eb + 2),
        in_specs=[pl.BlockSpec((tm, d), lambda t, s: (t, 0)), pl.BlockSpec((tm, d), lambda t, s: (t, 0)),
                  *u_specs, *vt_specs, tok, tok, tok, tok],
        out_specs=pl.BlockSpec((tm, d), lambda t, s: (t, 0)),
        out_shape=jax.ShapeDtypeStruct((n, d), F32),
        scratch_shapes=[pltpu.VMEM((d, tm), F32), pltpu.VMEM((te, tm), BF16), pltpu.VMEM((te, tm), BF16),
                        pltpu.VMEM((te, tm), BF16), pltpu.VMEM((te, tm), BF16),
                        pltpu.VMEM((tm // LANES, PEER_HEADS, PEER_KEYS // (2 * SUBLANES), 2, 2 * SUBLANES, LANES),
                                   BF16),
                        pltpu.VMEM((d, tm), BF16)],
        compiler_params=_params(("parallel", "arbitrary")),
        name="peer_experts",
    )(hn, h, *([u_b] * n_slices), *([vt_slabs] * n_slices), a0, l0, b1, r1)


def _tiles(batch, seq):
    return dict(tm=256, tq=256, kb=512, tr=128, te_tm=512, te=1024)


def kernel(x, norm1_g, w_in, q_norm_g, k_norm_g, pool_w, pool_scale, w_branch_attn, w_branch_pool, w_out, norm2_g,
           peer_wq, peer_subkeys, peer_u, peer_v):
    batch, seq, d = x.shape
    t = _tiles(batch, seq)
    x2 = x.reshape(batch * seq, d)
    for l in range(norm1_g.shape[0]):
        q, k, vt, qi, ki, wi, p, gate = _input_projection(
            x2, norm1_g[l], w_in[l], q_norm_g[l], k_norm_g[l], batch, seq, t["tm"], t["kb"])
        attn = _dsa_attention(q, qi, wi, k, vt, ki, batch, seq, t["tq"], t["kb"])
        h, hn = _mixer_output(attn, p, gate, x2, w_branch_attn[l], w_branch_pool[l], w_out[l], pool_w[l],
                              pool_scale[l], norm2_g[l], seq, t["tm"])
        a0, l0, b1, r1 = _peer_routing(hn, peer_wq[l], peer_subkeys[l], t["tr"])
        x2 = _peer_experts(hn, h, peer_u[l], peer_v[l], a0, l0, b1, r1, t["te_tm"], t["te"])
    return x2.reshape(batch, seq, d)
```

```python
import functools

import jax
import jax.numpy as jnp
import numpy as np
from jax import lax
from jax.experimental import pallas as pl
from jax.experimental.pallas import tpu as pltpu

CHUNK = 64
EPS = 1e-6
N_HEADS = 8
HEAD_DIM = 64
ATTN_WIDTH = N_HEADS * HEAD_DIM
ROT_HALF = HEAD_DIM // 8
ROPE_THETA = 500000.0
IDX_HEADS = 8
IDX_DIM = 64
TOPK_MAX = 256
POOL_WINDOWS = (2, 4, 8, 16)
POOL_WIDTH = 512
POOL_GROUP_DIM = POOL_WIDTH // len(POOL_WINDOWS)
POOL_HALO = 16
PEER_HEADS = 8
PEER_KEYS = 128
PEER_KEY_DIM = 64
PEER_TOPK = 16

LANES = 128
SUBLANES = 8
VMEM_LIMIT = 56 * 1024 * 1024

F32 = jnp.float32
BF16 = jnp.bfloat16
I32 = jnp.int32
I16 = jnp.int16
INT_MIN = -2147483648
HALF_BITS = 16
HALF_MASK = 0xFFFF
HALF_BIAS = 32768
NEG = -1e30
LOG2_E = 1.4426950408889634
ATTN_HEAD_GROUP = 4
EXPERT_SLICE = 256
NT_DIMS = (((1,), (1,)), ((), ()))


def _params(sem):
    return pltpu.CompilerParams(dimension_semantics=sem, vmem_limit_bytes=VMEM_LIMIT)


def _rope(t, c, s_lo, s_hi):
    w = t.shape[-1]
    return t * c + pltpu.roll(t, w - ROT_HALF, 1) * s_lo + pltpu.roll(t, ROT_HALF, 1) * s_hi


def _proj_body(x_ref, g1_ref, wqkv_ref, wqi_ref, wki_ref, wwi_ref, wp_ref, wgl_ref, qg_ref, kg_ref, bd_ref,
               c_ref, slo_ref, shi_ref,
               q_ref, k_ref, vt_ref, qi_ref, ki_ref, wi_ref, p_ref, gate_ref):
    x = x_ref[...]
    xn = x * lax.rsqrt(jnp.mean(x * x, axis=-1, keepdims=True) + EPS) * g1_ref[...]
    xb = xn.astype(BF16)
    c, s_lo, s_hi = c_ref[...], slo_ref[...], shi_ref[...]

    def head_norm(t, g):
        ms = jnp.dot((t * t).astype(BF16), bd_ref[...], preferred_element_type=F32)
        return t * lax.rsqrt(ms + EPS) * g

    qkv = jnp.dot(xb, wqkv_ref[...], preferred_element_type=F32)
    w = ATTN_WIDTH
    q = _rope(head_norm(qkv[:, :w], qg_ref[...]), c, s_lo, s_hi) * (HEAD_DIM ** -0.5 * LOG2_E)
    k = _rope(head_norm(qkv[:, w:2 * w], kg_ref[...]), c, s_lo, s_hi)
    q_ref[...] = q.astype(BF16)
    k_ref[...] = k.astype(BF16)
    vt_ref[0, 0] = qkv[:, 2 * w:].T.astype(BF16)
    qi = jnp.dot(xb, wqi_ref[...], preferred_element_type=F32)
    qi_ref[...] = (_rope(qi, c, s_lo, s_hi) * (IDX_DIM ** -0.5)).astype(BF16)
    ki = jnp.dot(xb, wki_ref[...], preferred_element_type=F32)
    ki = _rope(ki, c[:, :LANES], s_lo[:, :LANES], s_hi[:, :LANES])
    ki_ref[...] = ki[:, :IDX_DIM].astype(BF16)
    wi_ref[...] = jnp.dot(xb, wwi_ref[...], preferred_element_type=F32) * (IDX_HEADS ** -0.5)
    p_ref[...] = jnp.dot(xb, wp_ref[...], preferred_element_type=F32)
    gate_ref[...] = jax.nn.sigmoid(jnp.dot(xb, wgl_ref[...], preferred_element_type=F32)).astype(BF16)


def _rope_tables(seq):
    inv_freq = ROPE_THETA ** (-jnp.arange(ROT_HALF, dtype=F32) / ROT_HALF)
    ang = jnp.arange(seq, dtype=F32)[:, None] * inv_freq[None, :]
    cos, sin = jnp.cos(ang), jnp.sin(ang)
    rest = HEAD_DIM - 2 * ROT_HALF
    ones = jnp.ones((seq, rest), F32)
    zeros = jnp.zeros((seq, rest), F32)
    zh = jnp.zeros((seq, ROT_HALF), F32)
    c = jnp.concatenate([cos, cos, ones], axis=1)
    s_lo = jnp.concatenate([-sin, zh, zeros], axis=1)
    s_hi = jnp.concatenate([zh, sin, zeros], axis=1)
    tile = lambda t: jnp.tile(t, (1, N_HEADS))
    return tile(c), tile(s_lo), tile(s_hi)


def _input_projection(x2, norm1_g, w_in, q_norm_g, k_norm_g, batch, seq, tm, kb):
    n, d = x2.shape
    w = ATTN_WIDTH
    o = np.cumsum([0, w, w, w, IDX_HEADS * IDX_DIM, IDX_DIM, IDX_HEADS, POOL_WIDTH, 2 * d])
    wb = w_in.astype(BF16)
    wqkv = wb[:, o[0]:o[3]]
    wqi = wb[:, o[3]:o[4]]
    wki = jnp.pad(wb[:, o[4]:o[5]], ((0, 0), (0, LANES - IDX_DIM)))
    wwi = jnp.pad(wb[:, o[5]:o[6]], ((0, 0), (0, LANES - IDX_HEADS)))
    wp = wb[:, o[6]:o[7]]
    wgl = wb[:, o[7]:o[8]]
    bd = jnp.kron(jnp.eye(N_HEADS, dtype=F32), jnp.full((HEAD_DIM, HEAD_DIM), 1.0 / HEAD_DIM, F32)).astype(BF16)
    c, s_lo, s_hi = _rope_tables(seq)
    tps = seq // tm
    const = lambda shape: pl.BlockSpec(shape, lambda i: (0,) * len(shape))
    row = lambda width: pl.BlockSpec((tm, width), lambda i: (i, 0))
    tab = pl.BlockSpec((tm, w), lambda i: (i % tps, 0))
    per_kb = kb // tm
    out_shapes = (
        jax.ShapeDtypeStruct((n, w), BF16),
        jax.ShapeDtypeStruct((n, w), BF16),
        jax.ShapeDtypeStruct((batch, seq // kb, w, kb), BF16),
        jax.ShapeDtypeStruct((n, w), BF16),
        jax.ShapeDtypeStruct((n, IDX_DIM), BF16),
        jax.ShapeDtypeStruct((n, LANES), F32),
        jax.ShapeDtypeStruct((n, POOL_WIDTH), F32),
        jax.ShapeDtypeStruct((n, 2 * d), BF16),
    )
    out_specs = (
        row(w), row(w),
        pl.BlockSpec((1, 1, w, tm), lambda i: (i // tps, (i % tps) // per_kb, 0, (i % tps) % per_kb)),
        row(w), row(IDX_DIM), row(LANES), row(POOL_WIDTH), row(2 * d),
    )
    return pl.pallas_call(
        _proj_body,
        grid=(n // tm,),
        in_specs=[row(d), const((1, d)), const(wqkv.shape), const(wqi.shape), const(wki.shape), const(wwi.shape),
                  const(wp.shape), const(wgl.shape), const((1, w)), const((1, w)), const(bd.shape), tab, tab, tab],
        out_specs=out_specs,
        out_shape=out_shapes,
        compiler_params=_params(("parallel",)),
        name="input_projection",
    )(x2, norm1_g.reshape(1, d), wqkv, wqi, wki, wwi, wp, wgl,
      jnp.tile(q_norm_g, N_HEADS).reshape(1, w), jnp.tile(k_norm_g, N_HEADS).reshape(1, w), bd, c, s_lo, s_hi)


def _sortable(v):
    b = lax.bitcast_convert_type(v, I32)
    b = jnp.where(b == INT_MIN, 0, b)
    return jnp.where(b < 0, b ^ 0x7FFFFFFF, b)


def _dsa_body(q_ref, qi_ref, wi_ref, k_ref, vt_ref, ki_ref, o_ref, key_s, bias_s, acc_s, s_s, hi_s, lo_s,
              *, seq, tq, kb, topk):
    j = pl.program_id(1)
    nblk = ((j + 1) * tq + kb - 1) // kb
    lane = lax.broadcasted_iota(I32, (1, tq), 1)
    lim = j * tq + (lane // CHUNK + 1) * CHUNK
    row_iota = lax.broadcasted_iota(I32, (kb, tq), 0)
    wi_t = wi_ref[...].T[:IDX_HEADS, :]
    qi = qi_ref[...]

    def lanes_to_queries(t):
        return t.astype(F32).T.astype(BF16)

    qi_t = [lanes_to_queries(qi[:, h * IDX_DIM:(h + 1) * IDX_DIM]) for h in range(IDX_HEADS)]

    def rows(i):
        return pl.ds(pl.multiple_of(i * kb, kb), kb)

    def score_block(i, carry):
        kib = ki_ref[rows(i), :]
        acc = jnp.zeros((kb, tq), F32)
        for h in range(IDX_HEADS):
            lg = jnp.dot(kib, qi_t[h], preferred_element_type=F32)
            acc = acc + jnp.maximum(lg, 0.0) * wi_t[h:h + 1, :]
        key = jnp.where(i * kb + row_iota < lim, _sortable(acc), INT_MIN)
        key_s[rows(i), :] = key
        hi_s[rows(i), :] = (key >> HALF_BITS).astype(I16)
        lo_s[rows(i), :] = ((key & HALF_MASK) - HALF_BIAS).astype(I16)
        return carry

    lax.fori_loop(0, nblk, score_block, 0)

    def count(pred):
        def body(i, c):
            m = pred(key_s[rows(i), :], i * kb + row_iota)
            return c + jnp.sum(m.astype(I32).reshape(kb // SUBLANES, SUBLANES, tq), axis=0)
        c8 = lax.fori_loop(0, nblk, body, jnp.zeros((SUBLANES, tq), I32))
        return jnp.sum(c8, axis=0, keepdims=True)

    pack = 2 * SUBLANES
    one16, zero16 = jnp.ones((), I16), jnp.zeros((), I16)

    def spread16(v):
        return jnp.broadcast_to(v.astype(I16), (kb, tq))

    def count16(ref, pred):
        def body(i, c):
            m = pred(ref[rows(i), :])
            hit = jnp.where(m, one16, zero16)
            parts = [hit[r:r + pack, :] for r in range(0, kb, pack)]
            while len(parts) > 1:
                parts = [parts[r] + parts[r + 1] for r in range(0, len(parts), 2)]
            return c + parts[0]
        c16 = lax.fori_loop(0, nblk, body, jnp.zeros((pack, tq), I16))
        return jnp.sum(c16.astype(I32), axis=0, keepdims=True)

    def search16(ref, need):
        def bit(it, tu):
            cand_u = tu | lax.shift_left(jnp.int32(1), HALF_BITS - 1 - it)
            cand = spread16(cand_u - HALF_BIAS)
            return jnp.where(count16(ref, lambda blk: blk >= cand) >= need, cand_u, tu)
        return lax.fori_loop(0, HALF_BITS, bit, jnp.zeros((1, tq), I32))

    hi_u = search16(hi_s, topk)
    thr_hi = spread16(hi_u - HALF_BIAS)
    above = count16(hi_s, lambda blk: blk > thr_hi)

    def mask_low(i, carry):
        lo_s[rows(i), :] = jnp.where(hi_s[rows(i), :] == thr_hi, lo_s[rows(i), :], jnp.full((), -HALF_BIAS, I16))
        return carry

    lax.fori_loop(0, nblk, mask_low, 0)
    lo_u = search16(lo_s, topk - above)
    thr = lax.shift_left(hi_u - HALF_BIAS, HALF_BITS) | lo_u
    idx_bits = int(seq).bit_length()
    surplus = (count(lambda blk, idx: blk >= thr) != topk) & (thr != INT_MIN)
    has_tie = jnp.max(jnp.where(surplus, 1.0, 0.0)) > 0.5

    def resolve_ties():
        need = topk - count(lambda blk, idx: blk > thr)

        def index_bit(it, jj):
            cand = jj | lax.shift_left(jnp.int32(1), idx_bits - 1 - it)
            return jnp.where(count(lambda blk, idx: (blk == thr) & (idx < cand)) <= need, cand, jj)

        return lax.fori_loop(0, idx_bits, index_bit, jnp.zeros((1, tq), I32))

    tie_end = lax.cond(has_tie, resolve_ties, lambda: jnp.full((1, tq), (1 << idx_bits) - 1, I32))

    def bias_block(i, carry):
        blk = key_s[rows(i), :]
        idx = i * kb + row_iota
        sel = ((blk > thr) | ((blk == thr) & (idx < tie_end))) & (idx < lim)
        bias_s[rows(i), :] = jnp.where(sel, 0.0, NEG)
        return carry

    lax.fori_loop(0, nblk, bias_block, 0)

    q = q_ref[...]
    pair_lane = lax.broadcasted_iota(I32, (tq, 2 * HEAD_DIM), 1)
    qm = []
    for h in range(N_HEADS):
        pair = q[:, (h // 2) * 2 * HEAD_DIM:(h // 2 + 1) * 2 * HEAD_DIM]
        qm.append(lanes_to_queries(jnp.where((pair_lane // HEAD_DIM) == (h % 2), pair, jnp.zeros_like(pair))))
    acc_s[...] = jnp.zeros_like(acc_s)
    group = s_s.shape[0]

    def fold(t):
        return t.reshape(kb // SUBLANES, SUBLANES, tq)

    for g0 in range(0, N_HEADS, group):
        heads = range(g0, g0 + group)

        def score_pass(i, ms):
            bias = bias_s[rows(i), :]
            out = []
            for hh, h in enumerate(heads):
                kblk = k_ref[rows(i), (h // 2) * 2 * HEAD_DIM:(h // 2 + 1) * 2 * HEAD_DIM]
                s = jnp.dot(kblk, qm[h], preferred_element_type=F32) + bias
                s_s[hh, rows(i), :] = s
                out.append(jnp.maximum(ms[hh], jnp.max(fold(s), axis=0)))
            return tuple(out)

        ms = lax.fori_loop(0, nblk, score_pass, tuple(jnp.full((SUBLANES, tq), NEG, F32) for _ in heads))
        mx = [jnp.max(m, axis=0, keepdims=True) for m in ms]

        def value_pass(i, ls):
            out = []
            for hh, h in enumerate(heads):
                hs = slice(h * HEAD_DIM, (h + 1) * HEAD_DIM)
                p = jnp.exp2(s_s[hh, rows(i), :] - mx[hh])
                out.append(ls[hh] + jnp.sum(fold(p), axis=0))
                acc_s[hs, :] += jnp.dot(vt_ref[0, i, hs, :], p.astype(BF16), preferred_element_type=F32)
            return tuple(out)

        ls = lax.fori_loop(0, nblk, value_pass, tuple(jnp.zeros((SUBLANES, tq), F32) for _ in heads))
        for hh, h in enumerate(heads):
            hs = slice(h * HEAD_DIM, (h + 1) * HEAD_DIM)
            acc_s[hs, :] = acc_s[hs, :] / jnp.sum(ls[hh], axis=0, keepdims=True)
    o_ref[...] = acc_s[...].T.astype(BF16)


def _dsa_attention(q, qi, wi, k, vt, ki, batch, seq, tq, kb):
    n, w = q.shape
    topk = min(TOPK_MAX, seq // 4)
    nq = seq // tq
    tile = lambda width: pl.BlockSpec((tq, width), lambda b, j: (b * nq + j, 0))
    whole = lambda width: pl.BlockSpec((seq, width), lambda b, j: (b, 0))
    return pl.pallas_call(
        functools.partial(_dsa_body, seq=seq, tq=tq, kb=kb, topk=topk),
        grid=(batch, nq),
        in_specs=[tile(w), tile(w), tile(LANES), whole(w),
                  pl.BlockSpec((1, seq // kb, w, kb), lambda b, j: (b, 0, 0, 0)), whole(IDX_DIM)],
        out_specs=tile(w),
        out_shape=jax.ShapeDtypeStruct((n, w), BF16),
        scratch_shapes=[pltpu.VMEM((seq, tq), I32), pltpu.VMEM((seq, tq), F32), pltpu.VMEM((w, tq), F32),
                        pltpu.VMEM((ATTN_HEAD_GROUP, seq, tq), F32),
                        pltpu.VMEM((seq, tq), I16), pltpu.VMEM((seq, tq), I16)],
        compiler_params=_params(("parallel", "arbitrary")),
        name="dsa_attention",
    )(q, qi, wi, k, vt, ki)


def _mix_body(attn_ref, p_ref, halo_ref, gate_ref, x_ref, wa_ref, wpb_ref, wo_ref, pw_ref, ps_ref, g2_ref,
              h_ref, hn_ref, ext_s, *, tm, tps):
    st = pl.program_id(0) % tps
    ext_s[0:POOL_HALO, :] = jnp.where(st == 0, 0.0, halo_ref[...])
    ext_s[POOL_HALO:POOL_HALO + tm, :] = p_ref[...]
    t1 = (st * tm + 1 + lax.broadcasted_iota(I32, (tm, POOL_GROUP_DIM), 0)).astype(F32)
    mixed = []
    for g, win in enumerate(POOL_WINDOWS):
        ls = slice(g * POOL_GROUP_DIM, (g + 1) * POOL_GROUP_DIM)
        frame = ext_s[POOL_HALO:POOL_HALO + tm, ls]
        tot = frame
        for dlt in range(1, win):
            tot = tot + ext_s[POOL_HALO - dlt:POOL_HALO - dlt + tm, ls]
        pooled = tot / jnp.minimum(t1, float(win)) - frame
        mixed.append(jnp.dot(pooled.astype(BF16), pw_ref[g], preferred_element_type=F32))
    mixed = jnp.concatenate(mixed, axis=1) * ps_ref[...]
    y_pool = jnp.dot(mixed.astype(BF16), wpb_ref[...], preferred_element_type=F32)
    y_attn = jnp.dot(attn_ref[...], wa_ref[...], preferred_element_type=F32)
    d = y_attn.shape[1]
    gate = gate_ref[...].astype(F32)
    z = gate[:, :d] * y_attn + gate[:, d:] * y_pool
    h = x_ref[...] + jnp.dot(z.astype(BF16), wo_ref[...], preferred_element_type=F32)
    h_ref[...] = h
    hn = h * lax.rsqrt(jnp.mean(h * h, axis=-1, keepdims=True) + EPS) * g2_ref[...]
    hn_ref[...] = hn.astype(BF16)


def _mixer_output(attn, p, gate, x2, w_branch_attn, w_branch_pool, w_out, pool_w, pool_scale, norm2_g, seq, tm):
    n, d = x2.shape
    tps = seq // tm
    hb = tm // POOL_HALO
    const = lambda shape: pl.BlockSpec(shape, lambda i: (0,) * len(shape))
    row = lambda width: pl.BlockSpec((tm, width), lambda i: (i, 0))
    return pl.pallas_call(
        functools.partial(_mix_body, tm=tm, tps=tps),
        grid=(n // tm,),
        in_specs=[row(ATTN_WIDTH), row(POOL_WIDTH),
                  pl.BlockSpec((POOL_HALO, POOL_WIDTH), lambda i: (jnp.maximum(i * hb - 1, 0), 0)),
                  row(2 * d), row(d), const((ATTN_WIDTH, d)), const((POOL_WIDTH, d)), const((d, d)),
                  const(pool_w.shape), const((1, POOL_WIDTH)), const((1, d))],
        out_specs=(row(d), row(d)),
        out_shape=(jax.ShapeDtypeStruct((n, d), F32), jax.ShapeDtypeStruct((n, d), BF16)),
        scratch_shapes=[pltpu.VMEM((POOL_HALO + tm, POOL_WIDTH), F32)],
        compiler_params=_params(("parallel",)),
        name="mixer_output",
    )(attn, p, p, gate, x2, w_branch_attn.astype(BF16), w_branch_pool.astype(BF16), w_out.astype(BF16),
      pool_w.astype(BF16), pool_scale.reshape(1, POOL_WIDTH), norm2_g.reshape(1, d))


def _candidate_pairs():
    return [(a, b) for a in range(PEER_TOPK) for b in range(PEER_TOPK) if (a + 1) * (b + 1) <= PEER_TOPK]


def _sort_desc(v):
    v = list(v)
    n = len(v)
    k = 2
    while k <= n:
        j = k // 2
        while j >= 1:
            for i in range(n):
                m = i ^ j
                if m > i:
                    hi, lo = jnp.maximum(v[i], v[m]), jnp.minimum(v[i], v[m])
                    v[i], v[m] = (hi, lo) if (i & k) == 0 else (lo, hi)
            j //= 2
        k *= 2
    return v


def _merge_top(a, b):
    n = len(a)
    c = [jnp.maximum(a[i], b[n - 1 - i]) for i in range(n)]
    j = n // 2
    while j >= 1:
        for i in range(n):
            m = i ^ j
            if m > i:
                c[i], c[m] = jnp.maximum(c[i], c[m]), jnp.minimum(c[i], c[m])
        j //= 2
    return c


def _route_body(hn_ref, wqt_ref, kbig_ref, a0_ref, l0_ref, b1_ref, r1_ref, vals_s, rank_s, ex_s, *, tr):
    nk, nh, kt = PEER_KEYS, PEER_HEADS, PEER_TOPK
    half_rows = nh * PEER_KEY_DIM
    qt = lax.dot_general(wqt_ref[...], hn_ref[...], NT_DIMS, preferred_element_type=F32).astype(BF16)
    for p in range(2):
        sub = jnp.dot(kbig_ref[p], qt[p * half_rows:(p + 1) * half_rows], preferred_element_type=F32)
        vals_s[p] = sub.reshape(nk, nh, tr)

    def best(p, lo, hi):
        if hi - lo == kt:
            return _sort_desc([vals_s[p, i] for i in range(lo, hi)])
        mid = (lo + hi) // 2
        return _merge_top(best(p, lo, mid), best(p, mid, hi))

    tops = [best(p, 0, nk) for p in range(2)]
    v0, v1 = tops
    tied = jnp.zeros((nh, tr), F32)
    for p in range(2):
        for a in range(kt - 1):
            tied = jnp.maximum(tied, jnp.where(tops[p][a] == tops[p][a + 1], 1.0, 0.0))
        above = [jnp.where(vals_s[p, i] >= tops[p][kt - 1], 1.0, 0.0) for i in range(nk)]
        while len(above) > 1:
            above = [above[i] + above[i + 1] for i in range(0, len(above), 2)]
        tied = jnp.maximum(tied, jnp.where(above[0] != float(kt), 1.0, 0.0))
    has_tie = jnp.max(tied) > 0.5

    pairs = _candidate_pairs()
    cand = [v0[a] + v1[b] for a, b in pairs]
    rank = [jnp.zeros((nh, tr), F32) for _ in pairs]
    for ia, (a0, a1) in enumerate(pairs):
        for ib in range(ia + 1, len(pairs)):
            b0, b1 = pairs[ib]
            if a0 <= b0 and a1 <= b1:
                rank[ib] = rank[ib] + 1.0
            else:
                wins = jnp.where(cand[ia] >= cand[ib], 1.0, 0.0)
                rank[ib] = rank[ib] + wins
                rank[ia] = rank[ia] + (1.0 - wins)
    e0 = [jnp.exp(v0[a] - v0[0]) for a in range(kt)]
    e1 = [jnp.exp(v1[b] - v1[0]) for b in range(kt)]
    width = [jnp.zeros((nh, tr), F32) for _ in range(kt)]
    z = jnp.zeros((nh, tr), F32)
    for ic, (a, b) in enumerate(pairs):
        sel = jnp.where(rank[ic] < float(kt), 1.0, 0.0)
        width[a] = width[a] + sel
        z = z + sel * (e0[a] * e1[b])
    inv_z = 1.0 / z

    def key_rows(i):
        return slice(i * nh, (i + 1) * nh)

    @pl.when(jnp.logical_not(has_tie))
    def _():
        for i in range(nk):
            x0, x1 = vals_s[0, i], vals_s[1, i]
            width_i = jnp.zeros((nh, tr), F32)
            for a in range(kt):
                width_i = jnp.where(x0 == v0[a], width[a], width_i)
            a0_ref[0, key_rows(i), :] = jnp.where(x0 >= v0[kt - 1], jnp.exp(x0 - v0[0]) * inv_z, 0.0)
            l0_ref[0, key_rows(i), :] = width_i
            above8 = v1[7] > x1
            piv = jnp.where(above8, v1[11], v1[3])
            above4 = piv > x1
            piv = jnp.where(above8, jnp.where(above4, v1[13], v1[9]), jnp.where(above4, v1[5], v1[1]))
            above2 = piv > x1
            piv = jnp.where(
                above8,
                jnp.where(above4, jnp.where(above2, v1[14], v1[12]), jnp.where(above2, v1[10], v1[8])),
                jnp.where(above4, jnp.where(above2, v1[6], v1[4]), jnp.where(above2, v1[2], v1[0])))
            pos = (jnp.where(above8, 8.0, 0.0) + jnp.where(above4, 4.0, 0.0) + jnp.where(above2, 2.0, 0.0)
                   + jnp.where(piv > x1, 1.0, 0.0))
            chosen = x1 >= v1[kt - 1]
            b1_ref[0, key_rows(i), :] = jnp.where(chosen, jnp.exp(x1 - v1[0]), 0.0)
            r1_ref[0, key_rows(i), :] = jnp.where(chosen, pos, float(kt))

    @pl.when(has_tie)
    def _():
        rank_s[...] = jnp.full(rank_s.shape, float(kt), F32)
        ex_s[...] = jnp.zeros(ex_s.shape, F32)
        key_iota = lax.broadcasted_iota(I32, (nk, nh, tr), 0)

        def extract(kk, carry):
            for p in range(2):
                v = vals_s[p]
                m = jnp.max(v, axis=0)
                idx = jnp.min(jnp.where(v == m[None], key_iota, nk), axis=0)
                hit = key_iota == idx[None]
                vals_s[p] = jnp.where(hit, -jnp.inf, v)
                rank_s[p] = jnp.where(hit, lax.convert_element_type(kk, F32), rank_s[p])
                ex_s[p] = jnp.where(hit, jnp.exp(m - tops[p][0])[None], ex_s[p])
            return carry

        lax.fori_loop(0, kt, extract, 0)
        r0 = rank_s[0]
        l0 = jnp.zeros((nk, nh, tr), F32)
        for a in range(kt):
            l0 = jnp.where(r0 == float(a), width[a][None], l0)
        a0_ref[0] = (ex_s[0] * inv_z[None]).reshape(nk * nh, tr)
        l0_ref[0] = l0.reshape(nk * nh, tr)
        b1_ref[0] = ex_s[1].reshape(nk * nh, tr)
        r1_ref[0] = rank_s[1].reshape(nk * nh, tr)


def _peer_routing(hn, peer_wq, peer_subkeys, tr):
    n, d = hn.shape
    nk, nh, kd = PEER_KEYS, PEER_HEADS, PEER_KEY_DIM
    wqt = peer_wq.reshape(d, nh, 2, kd).transpose(2, 1, 3, 0).reshape(2 * nh * kd, d).astype(BF16)
    eye = jnp.eye(nh, dtype=peer_subkeys.dtype)
    kbig = jnp.einsum("hpnd,hg->pnhgd", peer_subkeys, eye).reshape(2, nk * nh, nh * kd).astype(BF16)
    rows = nk * nh
    assert tr == LANES
    out = jax.ShapeDtypeStruct((n // tr, rows, tr), F32)
    spec = pl.BlockSpec((1, rows, tr), lambda i: (i, 0, 0))
    return pl.pallas_call(
        functools.partial(_route_body, tr=tr),
        grid=(n // tr,),
        in_specs=[pl.BlockSpec((tr, d), lambda i: (i, 0)),
                  pl.BlockSpec(wqt.shape, lambda i: (0, 0)),
                  pl.BlockSpec(kbig.shape, lambda i: (0, 0, 0))],
        out_specs=(spec, spec, spec, spec),
        out_shape=(out, out, out, out),
        scratch_shapes=[pltpu.VMEM((2, nk, nh, tr), F32), pltpu.VMEM((2, nk, nh, tr), F32),
                        pltpu.VMEM((2, nk, nh, tr), F32)],
        compiler_params=_params(("parallel",)),
        name="peer_routing",
    )(hn, wqt, kbig)


def _expert_body(*refs, tm, te, n_eb):
    n_slices = te // EXPERT_SLICE
    hn_ref, h_ref = refs[:2]
    u_refs = refs[2:2 + n_slices]
    vt_refs = refs[2 + n_slices:2 + 2 * n_slices]
    a0_ref, l0_ref, b1_ref, r1_ref, y_ref, acc_s, g0_s, g1_s, ga0_s, ga1_s, br_s, hnt_s = refs[2 + 2 * n_slices:]
    step = pl.program_id(1)
    nk, nh = PEER_KEYS, PEER_HEADS
    pack = 2 * SUBLANES

    def gt(slot):
        return (g0_s, g1_s)[slot]

    def ga(slot):
        return (ga0_s, ga1_s)[slot]

    every = slice(0, te)
    slices = [slice(k * EXPERT_SLICE, (k + 1) * EXPERT_SLICE) for k in range(n_slices)]

    def project(k):
        return jnp.dot(u_refs[k][...], hnt_s[...], preferred_element_type=F32)

    def finish(act, src, dst, rs):
        act = 0.5 * act * (1.0 + lax.erf(act * (2.0 ** -0.5)))
        dst[rs, :] = src[rs, :] * act.astype(BF16)

    def value_part(src, k):
        return jnp.dot(vt_refs[k][0], src[slices[k], :], preferred_element_type=F32)

    def gate_piece(dst, il, c):
        i = step * (te // nk) + il
        g = jnp.zeros((nk, LANES), BF16)
        for h in range(nh):
            row = pl.ds(i * nh + h, 1)
            a_row = jnp.broadcast_to(a0_ref[c, row, :], (pack, LANES)).astype(BF16)
            l_row = jnp.broadcast_to(l0_ref[c, row, :], (pack, LANES)).astype(BF16)
            a_row = jnp.tile(a_row, (nk // pack, 1))
            l_row = jnp.tile(l_row, (nk // pack, 1))
            b1 = br_s[c, h, :, 0].reshape(nk, LANES)
            r1 = br_s[c, h, :, 1].reshape(nk, LANES)
            g = g + a_row * jnp.where(r1 < l_row, b1, jnp.zeros((), BF16))
        dst[il * nk:(il + 1) * nk, c * LANES:(c + 1) * LANES] = g

    def gate(dst, rs=every):
        for il in range(rs.start // nk, rs.stop // nk):
            for c in range(tm // LANES):
                gate_piece(dst, il, c)

    def apply(src):
        total = value_part(src, 0)
        for k in range(1, n_slices):
            total = total + value_part(src, k)
        acc_s[...] += total

    def activate(src, dst):
        for k in range(n_slices):
            finish(project(k), src, dst, slices[k])

    @pl.when(step == 0)
    def _():
        acc_s[...] = jnp.zeros_like(acc_s)
        hnt_s[...] = hn_ref[...].astype(F32).T.astype(BF16)
        for c in range(tm // LANES):
            for h in range(nh):
                b1 = b1_ref[c, pl.ds(h, nk, stride=nh), :].astype(BF16)
                r1 = r1_ref[c, pl.ds(h, nk, stride=nh), :].astype(BF16)
                br_s[c, h, :, 0] = b1.reshape(nk // pack, pack, LANES)
                br_s[c, h, :, 1] = r1.reshape(nk // pack, pack, LANES)
        gate(gt(0))

    @pl.when(step == 1)
    def _():
        activate(gt(0), ga(0))
        gate(gt(1))

    for p in range(2):
        @pl.when((step >= 2) & (step < n_eb) & (step % 2 == p))
        def _():
            per_slice = EXPERT_SLICE // nk
            pieces = [(il, c) for il in range(te // nk) for c in range(tm // LANES)]
            assert len(pieces) == (te // nk) * n_slices and acc_s.shape[0] == te
            for mb in range(te // nk):
                blk = slice(mb * nk, (mb + 1) * nk)
                u_rows = slice((mb % per_slice) * nk, (mb % per_slice + 1) * nk)
                act = val = None
                for ks, cols in enumerate(slices):
                    gate_piece(gt(p), *pieces[mb * n_slices + ks])
                    pu = jnp.dot(u_refs[mb // per_slice][u_rows, cols], hnt_s[cols, :], preferred_element_type=F32)
                    pv = jnp.dot(vt_refs[ks][0, blk, :], ga(p)[cols, :], preferred_element_type=F32)
                    act = pu if act is None else act + pu
                    val = pv if val is None else val + pv
                finish(act, gt(1 - p), ga(1 - p), blk)
                acc_s[blk, :] += val

    @pl.when(step == n_eb)
    def _():
        p = n_eb % 2
        apply(ga(p))
        activate(gt(1 - p), ga(1 - p))

    @pl.when(step == n_eb + 1)
    def _():
        apply(ga((n_eb + 1) % 2))
        y_ref[...] = h_ref[...] + acc_s[...].T


def _peer_experts(hn, h, peer_u, peer_v, a0, l0, b1, r1, tm, te):
    n, d = hn.shape
    ne = peer_u.shape[0]
    rows = a0.shape[1]
    tok = pl.BlockSpec((tm // LANES, rows, LANES), lambda t, e: (t, 0, 0))
    n_eb = ne // te
    n_slices = te // EXPERT_SLICE
    u_b = peer_u.astype(BF16)
    vt_slabs = peer_v.astype(BF16).reshape(ne // EXPERT_SLICE, EXPERT_SLICE, d).transpose(0, 2, 1)
    u_specs = [pl.BlockSpec((EXPERT_SLICE, d), lambda t, s, k=k: (n_slices * jnp.clip(s - 1, 0, n_eb - 1) + k, 0))
               for k in range(n_slices)]
    vt_specs = [pl.BlockSpec((1, d, EXPERT_SLICE),
                             lambda t, s, k=k: (n_slices * jnp.clip(s - 2, 0, n_eb - 1) + k, 0, 0))
                for k in range(n_slices)]
    return pl.pallas_call(
        functools.partial(_expert_body, tm=tm, te=te, n_eb=n_eb),
        grid=(n // tm, n_eb + 2),
        in_specs=[pl.BlockSpec((tm, d), lambda t, s: (t, 0)), pl.BlockSpec((tm, d), lambda t, s: (t, 0)),
                  *u_specs, *vt_specs, tok, tok, tok, tok],
        out_specs=pl.BlockSpec((tm, d), lambda t, s: (t, 0)),
        out_shape=jax.ShapeDtypeStruct((n, d), F32),
        scratch_shapes=[pltpu.VMEM((d, tm), F32), pltpu.VMEM((te, tm), BF16), pltpu.VMEM((te, tm), BF16),
                        pltpu.VMEM((te, tm), BF16), pltpu.VMEM((te, tm), BF16),
                        pltpu.VMEM((tm // LANES, PEER_HEADS, PEER_KEYS // (2 * SUBLANES), 2, 2 * SUBLANES, LANES),
                                   BF16),
                        pltpu.VMEM((d, tm), BF16)],
        compiler_params=_params(("parallel", "arbitrary")),
        name="peer_experts",
    )(hn, h, *([u_b] * n_slices), *([vt_slabs] * n_slices), a0, l0, b1, r1)


def _tiles(batch, seq):
    return dict(tm=256, tq=256, kb=512, tr=128, te_tm=512, te=1024)


def kernel(x, norm1_g, w_in, q_norm_g, k_norm_g, pool_w, pool_scale, w_branch_attn, w_branch_pool, w_out, norm2_g,
           peer_wq, peer_subkeys, peer_u, peer_v):
    batch, seq, d = x.shape
    t = _tiles(batch, seq)
    x2 = x.reshape(batch * seq, d)
    for l in range(norm1_g.shape[0]):
        q, k, vt, qi, ki, wi, p, gate = _input_projection(
            x2, norm1_g[l], w_in[l], q_norm_g[l], k_norm_g[l], batch, seq, t["tm"], t["kb"])
        attn = _dsa_attention(q, qi, wi, k, vt, ki, batch, seq, t["tq"], t["kb"])
        h, hn = _mixer_output(attn, p, gate, x2, w_branch_attn[l], w_branch_pool[l], w_out[l], pool_w[l],
                              pool_scale[l], norm2_g[l], seq, t["tm"])
        a0, l0, b1, r1 = _peer_routing(hn, peer_wq[l], peer_subkeys[l], t["tr"])
        x2 = _peer_experts(hn, h, peer_u[l], peer_v[l], a0, l0, b1, r1, t["te_tm"], t["te"])
    return x2.reshape(batch, seq, d)
```

```python
import functools

import jax
import jax.numpy as jnp
import numpy as np
from jax import lax
from jax.experimental import pallas as pl
from jax.experimental.pallas import tpu as pltpu

CHUNK = 64
EPS = 1e-6
N_HEADS = 8
HEAD_DIM = 64
ATTN_WIDTH = N_HEADS * HEAD_DIM
ROT_HALF = HEAD_DIM // 8
ROPE_THETA = 500000.0
IDX_HEADS = 8
IDX_DIM = 64
TOPK_MAX = 256
POOL_WINDOWS = (2, 4, 8, 16)
POOL_WIDTH = 512
POOL_GROUP_DIM = POOL_WIDTH // len(POOL_WINDOWS)
POOL_HALO = 16
PEER_HEADS = 8
PEER_KEYS = 128
PEER_KEY_DIM = 64
PEER_TOPK = 16

LANES = 128
SUBLANES = 8
VMEM_LIMIT = 56 * 1024 * 1024

F32 = jnp.float32
BF16 = jnp.bfloat16
I32 = jnp.int32
I16 = jnp.int16
INT_MIN = -2147483648
HALF_BITS = 16
HALF_MASK = 0xFFFF
HALF_BIAS = 32768
NEG = -1e30
LOG2_E = 1.4426950408889634
ATTN_HEAD_GROUP = 4
EXPERT_SLICE = 256
NT_DIMS = (((1,), (1,)), ((), ()))


def _params(sem):
    return pltpu.CompilerParams(dimension_semantics=sem, vmem_limit_bytes=VMEM_LIMIT)


def _rope(t, c, s_lo, s_hi):
    w = t.shape[-1]
    return t * c + pltpu.roll(t, w - ROT_HALF, 1) * s_lo + pltpu.roll(t, ROT_HALF, 1) * s_hi


def _proj_body(x_ref, g1_ref, wqkv_ref, wqi_ref, wki_ref, wwi_ref, wp_ref, wgl_ref, qg_ref, kg_ref, bd_ref,
               c_ref, slo_ref, shi_ref,
               q_ref, k_ref, vt_ref, qi_ref, ki_ref, wi_ref, p_ref, gate_ref):
    x = x_ref[...]
    xn = x * lax.rsqrt(jnp.mean(x * x, axis=-1, keepdims=True) + EPS) * g1_ref[...]
    xb = xn.astype(BF16)
    c, s_lo, s_hi = c_ref[...], slo_ref[...], shi_ref[...]

    def head_norm(t, g):
        ms = jnp.dot((t * t).astype(BF16), bd_ref[...], preferred_element_type=F32)
        return t * lax.rsqrt(ms + EPS) * g

    qkv = jnp.dot(xb, wqkv_ref[...], preferred_element_type=F32)
    w = ATTN_WIDTH
    q = _rope(head_norm(qkv[:, :w], qg_ref[...]), c, s_lo, s_hi) * (HEAD_DIM ** -0.5 * LOG2_E)
    k = _rope(head_norm(qkv[:, w:2 * w], kg_ref[...]), c, s_lo, s_hi)
    q_ref[...] = q.astype(BF16)
    k_ref[...] = k.astype(BF16)
    vt_ref[0, 0] = qkv[:, 2 * w:].T.astype(BF16)
    qi = jnp.dot(xb, wqi_ref[...], preferred_element_type=F32)
    qi_ref[...] = (_rope(qi, c, s_lo, s_hi) * (IDX_DIM ** -0.5)).astype(BF16)
    ki = jnp.dot(xb, wki_ref[...], preferred_element_type=F32)
    ki = _rope(ki, c[:, :LANES], s_lo[:, :LANES], s_hi[:, :LANES])
    ki_ref[...] = ki[:, :IDX_DIM].astype(BF16)
    wi_ref[...] = jnp.dot(xb, wwi_ref[...], preferred_element_type=F32) * (IDX_HEADS ** -0.5)
    p_ref[...] = jnp.dot(xb, wp_ref[...], preferred_element_type=F32)
    gate_ref[...] = jax.nn.sigmoid(jnp.dot(xb, wgl_ref[...], preferred_element_type=F32)).astype(BF16)


def _rope_tables(seq):
    inv_freq = ROPE_THETA ** (-jnp.arange(ROT_HALF, dtype=F32) / ROT_HALF)
    ang = jnp.arange(seq, dtype=F32)[:, None] * inv_freq[None, :]
    cos, sin = jnp.cos(ang), jnp.sin(ang)
    rest = HEAD_DIM - 2 * ROT_HALF
    ones = jnp.ones((seq, rest), F32)
    zeros = jnp.zeros((seq, rest), F32)
    zh = jnp.zeros((seq, ROT_HALF), F32)
    c = jnp.concatenate([cos, cos, ones], axis=1)
    s_lo = jnp.concatenate([-sin, zh, zeros], axis=1)
    s_hi = jnp.concatenate([zh, sin, zeros], axis=1)
    tile = lambda t: jnp.tile(t, (1, N_HEADS))
    return tile(c), tile(s_lo), tile(s_hi)


def _input_projection(x2, norm1_g, w_in, q_norm_g, k_norm_g, batch, seq, tm, kb):
    n, d = x2.shape
    w = ATTN_WIDTH
    o = np.cumsum([0, w, w, w, IDX_HEADS * IDX_DIM, IDX_DIM, IDX_HEADS, POOL_WIDTH, 2 * d])
    wb = w_in.astype(BF16)
    wqkv = wb[:, o[0]:o[3]]
    wqi = wb[:, o[3]:o[4]]
    wki = jnp.pad(wb[:, o[4]:o[5]], ((0, 0), (0, LANES - IDX_DIM)))
    wwi = jnp.pad(wb[:, o[5]:o[6]], ((0, 0), (0, LANES - IDX_HEADS)))
    wp = wb[:, o[6]:o[7]]
    wgl = wb[:, o[7]:o[8]]
    bd = jnp.kron(jnp.eye(N_HEADS, dtype=F32), jnp.full((HEAD_DIM, HEAD_DIM), 1.0 / HEAD_DIM, F32)).astype(BF16)
    c, s_lo, s_hi = _rope_tables(seq)
    tps = seq // tm
    const = lambda shape: pl.BlockSpec(shape, lambda i: (0,) * len(shape))
    row = lambda width: pl.BlockSpec((tm, width), lambda i: (i, 0))
    tab = pl.BlockSpec((tm, w), lambda i: (i % tps, 0))
    per_kb = kb // tm
    out_shapes = (
        jax.ShapeDtypeStruct((n, w), BF16),
        jax.ShapeDtypeStruct((n, w), BF16),
        jax.ShapeDtypeStruct((batch, seq // kb, w, kb), BF16),
        jax.ShapeDtypeStruct((n, w), BF16),
        jax.ShapeDtypeStruct((n, IDX_DIM), BF16),
        jax.ShapeDtypeStruct((n, LANES), F32),
        jax.ShapeDtypeStruct((n, POOL_WIDTH), F32),
        jax.ShapeDtypeStruct((n, 2 * d), BF16),
    )
    out_specs = (
        row(w), row(w),
        pl.BlockSpec((1, 1, w, tm), lambda i: (i // tps, (i % tps) // per_kb, 0, (i % tps) % per_kb)),
        row(w), row(IDX_DIM), row(LANES), row(POOL_WIDTH), row(2 * d),
    )
    return pl.pallas_call(
        _proj_body,
        grid=(n // tm,),
        in_specs=[row(d), const((1, d)), const(wqkv.shape), const(wqi.shape), const(wki.shape), const(wwi.shape),
                  const(wp.shape), const(wgl.shape), const((1, w)), const((1, w)), const(bd.shape), tab, tab, tab],
        out_specs=out_specs,
        out_shape=out_shapes,
        compiler_params=_params(("parallel",)),
        name="input_projection",
    )(x2, norm1_g.reshape(1, d), wqkv, wqi, wki, wwi, wp, wgl,
      jnp.tile(q_norm_g, N_HEADS).reshape(1, w), jnp.tile(k_norm_g, N_HEADS).reshape(1, w), bd, c, s_lo, s_hi)


def _sortable(v):
    b = lax.bitcast_convert_type(v, I32)
    b = jnp.where(b == INT_MIN, 0, b)
    return jnp.where(b < 0, b ^ 0x7FFFFFFF, b)


def _dsa_body(q_ref, qi_ref, wi_ref, k_ref, vt_ref, ki_ref, o_ref, key_s, bias_s, acc_s, s_s, hi_s, lo_s,
              *, seq, tq, kb, topk):
    j = pl.program_id(1)
    nblk = ((j + 1) * tq + kb - 1) // kb
    lane = lax.broadcasted_iota(I32, (1, tq), 1)
    lim = j * tq + (lane // CHUNK + 1) * CHUNK
    row_iota = lax.broadcasted_iota(I32, (kb, tq), 0)
    wi_t = wi_ref[...].T[:IDX_HEADS, :]
    qi = qi_ref[...]

    def lanes_to_queries(t):
        return t.astype(F32).T.astype(BF16)

    qi_t = [lanes_to_queries(qi[:, h * IDX_DIM:(h + 1) * IDX_DIM]) for h in range(IDX_HEADS)]

    def rows(i):
        return pl.ds(pl.multiple_of(i * kb, kb), kb)

    def score_block(i, carry):
        kib = ki_ref[rows(i), :]
        acc = jnp.zeros((kb, tq), F32)
        for h in range(IDX_HEADS):
            lg = jnp.dot(kib, qi_t[h], preferred_element_type=F32)
            acc = acc + jnp.maximum(lg, 0.0) * wi_t[h:h + 1, :]
        key = jnp.where(i * kb + row_iota < lim, _sortable(acc), INT_MIN)
        key_s[rows(i), :] = key
        hi_s[rows(i), :] = (key >> HALF_BITS).astype(I16)
        lo_s[rows(i), :] = ((key & HALF_MASK) - HALF_BIAS).astype(I16)
        return carry

    lax.fori_loop(0, nblk, score_block, 0)

    def count(pred):
        def body(i, c):
            m = pred(key_s[rows(i), :], i * kb + row_iota)
            return c + jnp.sum(m.astype(I32).reshape(kb // SUBLANES, SUBLANES, tq), axis=0)
        c8 = lax.fori_loop(0, nblk, body, jnp.zeros((SUBLANES, tq), I32))
        return jnp.sum(c8, axis=0, keepdims=True)

    pack = 2 * SUBLANES
    one16, zero16 = jnp.ones((), I16), jnp.zeros((), I16)

    def spread16(v):
        return jnp.broadcast_to(v.astype(I16), (kb, tq))

    def count16(ref, pred):
        def body(i, c):
            m = pred(ref[rows(i), :])
            hit = jnp.where(m, one16, zero16)
            parts = [hit[r:r + pack, :] for r in range(0, kb, pack)]
            while len(parts) > 1:
                parts = [parts[r] + parts[r + 1] for r in range(0, len(parts), 2)]
            return c + parts[0]
        c16 = lax.fori_loop(0, nblk, body, jnp.zeros((pack, tq), I16))
        return jnp.sum(c16.astype(I32), axis=0, keepdims=True)

    def search16(ref, need):
        def bit(it, tu):
            cand_u = tu | lax.shift_left(jnp.int32(1), HALF_BITS - 1 - it)
            cand = spread16(cand_u - HALF_BIAS)
            return jnp.where(count16(ref, lambda blk: blk >= cand) >= need, cand_u, tu)
        return lax.fori_loop(0, HALF_BITS, bit, jnp.zeros((1, tq), I32))

    hi_u = search16(hi_s, topk)
    thr_hi = spread16(hi_u - HALF_BIAS)
    above = count16(hi_s, lambda blk: blk > thr_hi)

    def mask_low(i, carry):
        lo_s[rows(i), :] = jnp.where(hi_s[rows(i), :] == thr_hi, lo_s[rows(i), :], jnp.full((), -HALF_BIAS, I16))
        return carry

    lax.fori_loop(0, nblk, mask_low, 0)
    lo_u = search16(lo_s, topk - above)
    thr = lax.shift_left(hi_u - HALF_BIAS, HALF_BITS) | lo_u
    idx_bits = int(seq).bit_length()
    surplus = (count(lambda blk, idx: blk >= thr) != topk) & (thr != INT_MIN)
    has_tie = jnp.max(jnp.where(surplus, 1.0, 0.0)) > 0.5

    def resolve_ties():
        need = topk - count(lambda blk, idx: blk > thr)

        def index_bit(it, jj):
            cand = jj | lax.shift_left(jnp.int32(1), idx_bits - 1 - it)
            return jnp.where(count(lambda blk, idx: (blk == thr) & (idx < cand)) <= need, cand, jj)

        return lax.fori_loop(0, idx_bits, index_bit, jnp.zeros((1, tq), I32))

    tie_end = lax.cond(has_tie, resolve_ties, lambda: jnp.full((1, tq), (1 << idx_bits) - 1, I32))

    def bias_block(i, carry):
        blk = key_s[rows(i), :]
        idx = i * kb + row_iota
        sel = ((blk > thr) | ((blk == thr) & (idx < tie_end))) & (idx < lim)
        bias_s[rows(i), :] = jnp.where(sel, 0.0, NEG)
        return carry

    lax.fori_loop(0, nblk, bias_block, 0)

    q = q_ref[...]
    pair_lane = lax.broadcasted_iota(I32, (tq, 2 * HEAD_DIM), 1)
    qm = []
    for h in range(N_HEADS):
        pair = q[:, (h // 2) * 2 * HEAD_DIM:(h // 2 + 1) * 2 * HEAD_DIM]
        qm.append(lanes_to_queries(jnp.where((pair_lane // HEAD_DIM) == (h % 2), pair, jnp.zeros_like(pair))))
    acc_s[...] = jnp.zeros_like(acc_s)
    group = s_s.shape[0]

    def fold(t):
        return t.reshape(kb // SUBLANES, SUBLANES, tq)

    for g0 in range(0, N_HEADS, group):
        heads = range(g0, g0 + group)

        def score_pass(i, ms):
            bias = bias_s[rows(i), :]
            out = []
            for hh, h in enumerate(heads):
                kblk = k_ref[rows(i), (h // 2) * 2 * HEAD_DIM:(h // 2 + 1) * 2 * HEAD_DIM]
                s = jnp.dot(kblk, qm[h], preferred_element_type=F32) + bias
                s_s[hh, rows(i), :] = s
                out.append(jnp.maximum(ms[hh], jnp.max(fold(s), axis=0)))
            return tuple(out)

        ms = lax.fori_loop(0, nblk, score_pass, tuple(jnp.full((SUBLANES, tq), NEG, F32) for _ in heads))
        mx = [jnp.max(m, axis=0, keepdims=True) for m in ms]

        def value_pass(i, ls):
            out = []
            for hh, h in enumerate(heads):
                hs = slice(h * HEAD_DIM, (h + 1) * HEAD_DIM)
                p = jnp.exp2(s_s[hh, rows(i), :] - mx[hh])
                out.append(ls[hh] + jnp.sum(fold(p), axis=0))
                acc_s[hs, :] += jnp.dot(vt_ref[0, i, hs, :], p.astype(BF16), preferred_element_type=F32)
            return tuple(out)

        ls = lax.fori_loop(0, nblk, value_pass, tuple(jnp.zeros((SUBLANES, tq), F32) for _ in heads))
        for hh, h in enumerate(heads):
            hs = slice(h * HEAD_DIM, (h + 1) * HEAD_DIM)
            acc_s[hs, :] = acc_s[hs, :] / jnp.sum(ls[hh], axis=0, keepdims=True)
    o_ref[...] = acc_s[...].T.astype(BF16)


def _dsa_attention(q, qi, wi, k, vt, ki, batch, seq, tq, kb):
    n, w = q.shape
    topk = min(TOPK_MAX, seq // 4)
    nq = seq // tq
    tile = lambda width: pl.BlockSpec((tq, width), lambda b, j: (b * nq + j, 0))
    whole = lambda width: pl.BlockSpec((seq, width), lambda b, j: (b, 0))
    return pl.pallas_call(
        functools.partial(_dsa_body, seq=seq, tq=tq, kb=kb, topk=topk),
        grid=(batch, nq),
        in_specs=[tile(w), tile(w), tile(LANES), whole(w),
                  pl.BlockSpec((1, seq // kb, w, kb), lambda b, j: (b, 0, 0, 0)), whole(IDX_DIM)],
        out_specs=tile(w),
        out_shape=jax.ShapeDtypeStruct((n, w), BF16),
        scratch_shapes=[pltpu.VMEM((seq, tq), I32), pltpu.VMEM((seq, tq), F32), pltpu.VMEM((w, tq), F32),
                        pltpu.VMEM((ATTN_HEAD_GROUP, seq, tq), F32),
                        pltpu.VMEM((seq, tq), I16), pltpu.VMEM((seq, tq), I16)],
        compiler_params=_params(("parallel", "arbitrary")),
        name="dsa_attention",
    )(q, qi, wi, k, vt, ki)


def _mix_body(attn_ref, p_ref, halo_ref, gate_ref, x_ref, wa_ref, wpb_ref, wo_ref, pw_ref, ps_ref, g2_ref,
              h_ref, hn_ref, ext_s, *, tm, tps):
    st = pl.program_id(0) % tps
    ext_s[0:POOL_HALO, :] = jnp.where(st == 0, 0.0, halo_ref[...])
    ext_s[POOL_HALO:POOL_HALO + tm, :] = p_ref[...]
    t1 = (st * tm + 1 + lax.broadcasted_iota(I32, (tm, POOL_GROUP_DIM), 0)).astype(F32)
    mixed = []
    for g, win in enumerate(POOL_WINDOWS):
        ls = slice(g * POOL_GROUP_DIM, (g + 1) * POOL_GROUP_DIM)
        frame = ext_s[POOL_HALO:POOL_HALO + tm, ls]
        tot = frame
        for dlt in range(1, win):
            tot = tot + ext_s[POOL_HALO - dlt:POOL_HALO - dlt + tm, ls]
        pooled = tot / jnp.minimum(t1, float(win)) - frame
        mixed.append(jnp.dot(pooled.astype(BF16), pw_ref[g], preferred_element_type=F32))
    mixed = jnp.concatenate(mixed, axis=1) * ps_ref[...]
    y_pool = jnp.dot(mixed.astype(BF16), wpb_ref[...], preferred_element_type=F32)
    y_attn = jnp.dot(attn_ref[...], wa_ref[...], preferred_element_type=F32)
    d = y_attn.shape[1]
    gate = gate_ref[...].astype(F32)
    z = gate[:, :d] * y_attn + gate[:, d:] * y_pool
    h = x_ref[...] + jnp.dot(z.astype(BF16), wo_ref[...], preferred_element_type=F32)
    h_ref[...] = h
    hn = h * lax.rsqrt(jnp.mean(h * h, axis=-1, keepdims=True) + EPS) * g2_ref[...]
    hn_ref[...] = hn.astype(BF16)


def _mixer_output(attn, p, gate, x2, w_branch_attn, w_branch_pool, w_out, pool_w, pool_scale, norm2_g, seq, tm):
    n, d = x2.shape
    tps = seq // tm
    hb = tm // POOL_HALO
    const = lambda shape: pl.BlockSpec(shape, lambda i: (0,) * len(shape))
    row = lambda width: pl.BlockSpec((tm, width), lambda i: (i, 0))
    return pl.pallas_call(
        functools.partial(_mix_body, tm=tm, tps=tps),
        grid=(n // tm,),
        in_specs=[row(ATTN_WIDTH), row(POOL_WIDTH),
                  pl.BlockSpec((POOL_HALO, POOL_WIDTH), lambda i: (jnp.maximum(i * hb - 1, 0), 0)),
                  row(2 * d), row(d), const((ATTN_WIDTH, d)), const((POOL_WIDTH, d)), const((d, d)),
                  const(pool_w.shape), const((1, POOL_WIDTH)), const((1, d))],
        out_specs=(row(d), row(d)),
        out_shape=(jax.ShapeDtypeStruct((n, d), F32), jax.ShapeDtypeStruct((n, d), BF16)),
        scratch_shapes=[pltpu.VMEM((POOL_HALO + tm, POOL_WIDTH), F32)],
        compiler_params=_params(("parallel",)),
        name="mixer_output",
    )(attn, p, p, gate, x2, w_branch_attn.astype(BF16), w_branch_pool.astype(BF16), w_out.astype(BF16),
      pool_w.astype(BF16), pool_scale.reshape(1, POOL_WIDTH), norm2_g.reshape(1, d))


def _candidate_pairs():
    return [(a, b) for a in range(PEER_TOPK) for b in range(PEER_TOPK) if (a + 1) * (b + 1) <= PEER_TOPK]


def _sort_desc(v):
    v = list(v)
    n = len(v)
    k = 2
    while k <= n:
        j = k // 2
        while j >= 1:
            for i in range(n):
                m = i ^ j
                if m > i:
                    hi, lo = jnp.maximum(v[i], v[m]), jnp.minimum(v[i], v[m])
                    v[i], v[m] = (hi, lo) if (i & k) == 0 else (lo, hi)
            j //= 2
        k *= 2
    return v


def _merge_top(a, b):
    n = len(a)
    c = [jnp.maximum(a[i], b[n - 1 - i]) for i in range(n)]
    j = n // 2
    while j >= 1:
        for i in range(n):
            m = i ^ j
            if m > i:
                c[i], c[m] = jnp.maximum(c[i], c[m]), jnp.minimum(c[i], c[m])
        j //= 2
    return c


def _route_body(hn_ref, wqt_ref, kbig_ref, a0_ref, l0_ref, b1_ref, r1_ref, vals_s, rank_s, ex_s, *, tr):
    nk, nh, kt = PEER_KEYS, PEER_HEADS, PEER_TOPK
    half_rows = nh * PEER_KEY_DIM
    qt = lax.dot_general(wqt_ref[...], hn_ref[...], NT_DIMS, preferred_element_type=F32).astype(BF16)
    for p in range(2):
        sub = jnp.dot(kbig_ref[p], qt[p * half_rows:(p + 1) * half_rows], preferred_element_type=F32)
        vals_s[p] = sub.reshape(nk, nh, tr)

    def best(p, lo, hi):
        if hi - lo == kt:
            return _sort_desc([vals_s[p, i] for i in range(lo, hi)])
        mid = (lo + hi) // 2
        return _merge_top(best(p, lo, mid), best(p, mid, hi))

    tops = [best(p, 0, nk) for p in range(2)]
    v0, v1 = tops
    tied = jnp.zeros((nh, tr), F32)
    for p in range(2):
        for a in range(kt - 1):
            tied = jnp.maximum(tied, jnp.where(tops[p][a] == tops[p][a + 1], 1.0, 0.0))
        above = [jnp.where(vals_s[p, i] >= tops[p][kt - 1], 1.0, 0.0) for i in range(nk)]
        while len(above) > 1:
            above = [above[i] + above[i + 1] for i in range(0, len(above), 2)]
        tied = jnp.maximum(tied, jnp.where(above[0] != float(kt), 1.0, 0.0))
    has_tie = jnp.max(tied) > 0.5

    pairs = _candidate_pairs()
    cand = [v0[a] + v1[b] for a, b in pairs]
    rank = [jnp.zeros((nh, tr), F32) for _ in pairs]
    for ia, (a0, a1) in enumerate(pairs):
        for ib in range(ia + 1, len(pairs)):
            b0, b1 = pairs[ib]
            if a0 <= b0 and a1 <= b1:
                rank[ib] = rank[ib] + 1.0
            else:
                wins = jnp.where(cand[ia] >= cand[ib], 1.0, 0.0)
                rank[ib] = rank[ib] + wins
                rank[ia] = rank[ia] + (1.0 - wins)
    e0 = [jnp.exp(v0[a] - v0[0]) for a in range(kt)]
    e1 = [jnp.exp(v1[b] - v1[0]) for b in range(kt)]
    width = [jnp.zeros((nh, tr), F32) for _ in range(kt)]
    z = jnp.zeros((nh, tr), F32)
    for ic, (a, b) in enumerate(pairs):
        sel = jnp.where(rank[ic] < float(kt), 1.0, 0.0)
        width[a] = width[a] + sel
        z = z + sel * (e0[a] * e1[b])
    inv_z = 1.0 / z

    def key_rows(i):
        return slice(i * nh, (i + 1) * nh)

    @pl.when(jnp.logical_not(has_tie))
    def _():
        for i in range(nk):
            x0, x1 = vals_s[0, i], vals_s[1, i]
            width_i = jnp.zeros((nh, tr), F32)
            for a in range(kt):
                width_i = jnp.where(x0 == v0[a], width[a], width_i)
            a0_ref[0, key_rows(i), :] = jnp.where(x0 >= v0[kt - 1], jnp.exp(x0 - v0[0]) * inv_z, 0.0)
            l0_ref[0, key_rows(i), :] = width_i
            above8 = v1[7] > x1
            piv = jnp.where(above8, v1[11], v1[3])
            above4 = piv > x1
            piv = jnp.where(above8, jnp.where(above4, v1[13], v1[9]), jnp.where(above4, v1[5], v1[1]))
            above2 = piv > x1
            piv = jnp.where(
                above8,
                jnp.where(above4, jnp.where(above2, v1[14], v1[12]), jnp.where(above2, v1[10], v1[8])),
                jnp.where(above4, jnp.where(above2, v1[6], v1[4]), jnp.where(above2, v1[2], v1[0])))
            pos = (jnp.where(above8, 8.0, 0.0) + jnp.where(above4, 4.0, 0.0) + jnp.where(above2, 2.0, 0.0)
                   + jnp.where(piv > x1, 1.0, 0.0))
            chosen = x1 >= v1[kt - 1]
            b1_ref[0, key_rows(i), :] = jnp.where(chosen, jnp.exp(x1 - v1[0]), 0.0)
            r1_ref[0, key_rows(i), :] = jnp.where(chosen, pos, float(kt))

    @pl.when(has_tie)
    def _():
        rank_s[...] = jnp.full(rank_s.shape, float(kt), F32)
        ex_s[...] = jnp.zeros(ex_s.shape, F32)
        key_iota = lax.broadcasted_iota(I32, (nk, nh, tr), 0)

        def extract(kk, carry):
            for p in range(2):
                v = vals_s[p]
                m = jnp.max(v, axis=0)
                idx = jnp.min(jnp.where(v == m[None], key_iota, nk), axis=0)
                hit = key_iota == idx[None]
                vals_s[p] = jnp.where(hit, -jnp.inf, v)
                rank_s[p] = jnp.where(hit, lax.convert_element_type(kk, F32), rank_s[p])
                ex_s[p] = jnp.where(hit, jnp.exp(m - tops[p][0])[None], ex_s[p])
            return carry

        lax.fori_loop(0, kt, extract, 0)
        r0 = rank_s[0]
        l0 = jnp.zeros((nk, nh, tr), F32)
        for a in range(kt):
            l0 = jnp.where(r0 == float(a), width[a][None], l0)
        a0_ref[0] = (ex_s[0] * inv_z[None]).reshape(nk * nh, tr)
        l0_ref[0] = l0.reshape(nk * nh, tr)
        b1_ref[0] = ex_s[1].reshape(nk * nh, tr)
        r1_ref[0] = rank_s[1].reshape(nk * nh, tr)


def _peer_routing(hn, peer_wq, peer_subkeys, tr):
    n, d = hn.shape
    nk, nh, kd = PEER_KEYS, PEER_HEADS, PEER_KEY_DIM
    wqt = peer_wq.reshape(d, nh, 2, kd).transpose(2, 1, 3, 0).reshape(2 * nh * kd, d).astype(BF16)
    eye = jnp.eye(nh, dtype=peer_subkeys.dtype)
    kbig = jnp.einsum("hpnd,hg->pnhgd", peer_subkeys, eye).reshape(2, nk * nh, nh * kd).astype(BF16)
    rows = nk * nh
    assert tr == LANES
    out = jax.ShapeDtypeStruct((n // tr, rows, tr), F32)
    spec = pl.BlockSpec((1, rows, tr), lambda i: (i, 0, 0))
    return pl.pallas_call(
        functools.partial(_route_body, tr=tr),
        grid=(n // tr,),
        in_specs=[pl.BlockSpec((tr, d), lambda i: (i, 0)),
                  pl.BlockSpec(wqt.shape, lambda i: (0, 0)),
                  pl.BlockSpec(kbig.shape, lambda i: (0, 0, 0))],
        out_specs=(spec, spec, spec, spec),
        out_shape=(out, out, out, out),
        scratch_shapes=[pltpu.VMEM((2, nk, nh, tr), F32), pltpu.VMEM((2, nk, nh, tr), F32),
                        pltpu.VMEM((2, nk, nh, tr), F32)],
        compiler_params=_params(("parallel",)),
        name="peer_routing",
    )(hn, wqt, kbig)


def _expert_body(*refs, tm, te, n_eb):
    n_slices = te // EXPERT_SLICE
    hn_ref, h_ref = refs[:2]
    u_refs = refs[2:2 + n_slices]
    vt_refs = refs[2 + n_slices:2 + 2 * n_slices]
    a0_ref, l0_ref, b1_ref, r1_ref, y_ref, acc_s, g0_s, g1_s, ga0_s, ga1_s, br_s, hnt_s = refs[2 + 2 * n_slices:]
    step = pl.program_id(1)
    nk, nh = PEER_KEYS, PEER_HEADS
    pack = 2 * SUBLANES

    def gt(slot):
        return (g0_s, g1_s)[slot]

    def ga(slot):
        return (ga0_s, ga1_s)[slot]

    every = slice(0, te)
    slices = [slice(k * EXPERT_SLICE, (k + 1) * EXPERT_SLICE) for k in range(n_slices)]

    def project(k):
        return jnp.dot(u_refs[k][...], hnt_s[...], preferred_element_type=F32)

    def finish(act, src, dst, rs):
        act = 0.5 * act * (1.0 + lax.erf(act * (2.0 ** -0.5)))
        dst[rs, :] = src[rs, :] * act.astype(BF16)

    def value_part(src, k):
        return jnp.dot(vt_refs[k][0], src[slices[k], :], preferred_element_type=F32)

    def gate_piece(dst, il, c):
        i = step * (te // nk) + il
        g = jnp.zeros((nk, LANES), BF16)
        for h in range(nh):
            row = pl.ds(i * nh + h, 1)
            a_row = jnp.broadcast_to(a0_ref[c, row, :], (pack, LANES)).astype(BF16)
            l_row = jnp.broadcast_to(l0_ref[c, row, :], (pack, LANES)).astype(BF16)
            a_row = jnp.tile(a_row, (nk // pack, 1))
            l_row = jnp.tile(l_row, (nk // pack, 1))
            b1 = br_s[c, h, :, 0].reshape(nk, LANES)
            r1 = br_s[c, h, :, 1].reshape(nk, LANES)
            g = g + a_row * jnp.where(r1 < l_row, b1, jnp.zeros((), BF16))
        dst[il * nk:(il + 1) * nk, c * LANES:(c + 1) * LANES] = g

    def gate(dst, rs=every):
        for il in range(rs.start // nk, rs.stop // nk):
            for c in range(tm // LANES):
                gate_piece(dst, il, c)

    def apply(src):
        total = value_part(src, 0)
        for k in range(1, n_slices):
            total = total + value_part(src, k)
        acc_s[...] += total

    def activate(src, dst):
        for k in range(n_slices):
            finish(project(k), src, dst, slices[k])

    @pl.when(step == 0)
    def _():
        acc_s[...] = jnp.zeros_like(acc_s)
        hnt_s[...] = hn_ref[...].astype(F32).T.astype(BF16)
        for c in range(tm // LANES):
            for h in range(nh):
                b1 = b1_ref[c, pl.ds(h, nk, stride=nh), :].astype(BF16)
                r1 = r1_ref[c, pl.ds(h, nk, stride=nh), :].astype(BF16)
                br_s[c, h, :, 0] = b1.reshape(nk // pack, pack, LANES)
                br_s[c, h, :, 1] = r1.reshape(nk // pack, pack, LANES)
        gate(gt(0))

    @pl.when(step == 1)
    def _():
        activate(gt(0), ga(0))
        gate(gt(1))

    for p in range(2):
        @pl.when((step >= 2) & (step < n_eb) & (step % 2 == p))
        def _():
            pieces = [(il, c) for il in range(te // nk) for c in range(tm // LANES)]
            per_pass = len(pieces) // (n_slices * n_slices)
            assert per_pass * n_slices * n_slices == len(pieces) and acc_s.shape[0] == te
            for mb, blk in enumerate(slices):
                act = val = None
                for ks, cols in enumerate(slices):
                    first = (mb * n_slices + ks) * per_pass
                    for piece in pieces[first:first + per_pass]:
                        gate_piece(gt(p), *piece)
                    pu = jnp.dot(u_refs[mb][:, cols], hnt_s[cols, :], preferred_element_type=F32)
                    pv = jnp.dot(vt_refs[ks][0, blk, :], ga(p)[cols, :], preferred_element_type=F32)
                    act = pu if act is None else act + pu
                    val = pv if val is None else val + pv
                finish(act, gt(1 - p), ga(1 - p), blk)
                acc_s[blk, :] += val

    @pl.when(step == n_eb)
    def _():
        p = n_eb % 2
        apply(ga(p))
        activate(gt(1 - p), ga(1 - p))

    @pl.when(step == n_eb + 1)
    def _():
        apply(ga((n_eb + 1) % 2))
        y_ref[...] = h_ref[...] + acc_s[...].T


def _peer_experts(hn, h, peer_u, peer_v, a0, l0, b1, r1, tm, te):
    n, d = hn.shape
    ne = peer_u.shape[0]
    rows = a0.shape[1]
    tok = pl.BlockSpec((tm // LANES, rows, LANES), lambda t, e: (t, 0, 0))
    n_eb = ne // te
    n_slices = te // EXPERT_SLICE
    u_b = peer_u.astype(BF16)
    vt_slabs = peer_v.astype(BF16).reshape(ne // EXPERT_SLICE, EXPERT_SLICE, d).transpose(0, 2, 1)
    u_specs = [pl.BlockSpec((EXPERT_SLICE, d), lambda t, s, k=k: (n_slices * jnp.clip(s - 1, 0, n_eb - 1) + k, 0))
               for k in range(n_slices)]
    vt_specs = [pl.BlockSpec((1, d, EXPERT_SLICE),
                             lambda t, s, k=k: (n_slices * jnp.clip(s - 2, 0, n_eb - 1) + k, 0, 0))
                for k in range(n_slices)]
    return pl.pallas_call(
        functools.partial(_expert_body, tm=tm, te=te, n_eb=n_eb),
        grid=(n // tm, n_eb + 2),
        in_specs=[pl.BlockSpec((tm, d), lambda t, s: (t, 0)), pl.BlockSpec((tm, d), lambda t, s: (t, 0)),
                  *u_specs, *vt_specs, tok, tok, tok, tok],
        out_specs=pl.BlockSpec((tm, d), lambda t, s: (t, 0)),
        out_shape=jax.ShapeDtypeStruct((n, d), F32),
        scratch_shapes=[pltpu.VMEM((d, tm), F32), pltpu.VMEM((te, tm), BF16), pltpu.VMEM((te, tm), BF16),
                        pltpu.VMEM((te, tm), BF16), pltpu.VMEM((te, tm), BF16),
                        pltpu.VMEM((tm // LANES, PEER_HEADS, PEER_KEYS // (2 * SUBLANES), 2, 2 * SUBLANES, LANES),
                                   BF16),
                        pltpu.VMEM((d, tm), BF16)],
        compiler_params=_params(("parallel", "arbitrary")),
        name="peer_experts",
    )(hn, h, *([u_b] * n_slices), *([vt_slabs] * n_slices), a0, l0, b1, r1)


def _tiles(batch, seq):
    return dict(tm=256, tq=256, kb=512, tr=128, te_tm=512, te=1024)


def kernel(x, norm1_g, w_in, q_norm_g, k_norm_g, pool_w, pool_scale, w_branch_attn, w_branch_pool, w_out, norm2_g,
           peer_wq, peer_subkeys, peer_u, peer_v):
    batch, seq, d = x.shape
    t = _tiles(batch, seq)
    x2 = x.reshape(batch * seq, d)
    for l in range(norm1_g.shape[0]):
        q, k, vt, qi, ki, wi, p, gate = _input_projection(
            x2, norm1_g[l], w_in[l], q_norm_g[l], k_norm_g[l], batch, seq, t["tm"], t["kb"])
        attn = _dsa_attention(q, qi, wi, k, vt, ki, batch, seq, t["tq"], t["kb"])
        h, hn = _mixer_output(attn, p, gate, x2, w_branch_attn[l], w_branch_pool[l], w_out[l], pool_w[l],
                              pool_scale[l], norm2_g[l], seq, t["tm"])
        a0, l0, b1, r1 = _peer_routing(hn, peer_wq[l], peer_subkeys[l], t["tr"])
        x2 = _peer_experts(hn, h, peer_u[l], peer_v[l], a0, l0, b1, r1, t["te_tm"], t["te"])
    return x2.reshape(batch, seq, d)
```

```python
import functools

import jax
import jax.numpy as jnp
import numpy as np
from jax import lax
from jax.experimental import pallas as pl
from jax.experimental.pallas import tpu as pltpu

CHUNK = 64
EPS = 1e-6
N_HEADS = 8
HEAD_DIM = 64
ATTN_WIDTH = N_HEADS * HEAD_DIM
ROT_HALF = HEAD_DIM // 8
ROPE_THETA = 500000.0
IDX_HEADS = 8
IDX_DIM = 64
TOPK_MAX = 256
POOL_WINDOWS = (2, 4, 8, 16)
POOL_WIDTH = 512
POOL_GROUP_DIM = POOL_WIDTH // len(POOL_WINDOWS)
POOL_HALO = 16
PEER_HEADS = 8
PEER_KEYS = 128
PEER_KEY_DIM = 64
PEER_TOPK = 16

LANES = 128
SUBLANES = 8
VMEM_LIMIT = 56 * 1024 * 1024

F32 = jnp.float32
BF16 = jnp.bfloat16
I32 = jnp.int32
I16 = jnp.int16
INT_MIN = -2147483648
HALF_BITS = 16
HALF_MASK = 0xFFFF
HALF_BIAS = 32768
NEG = -1e30
LOG2_E = 1.4426950408889634
ATTN_HEAD_GROUP = 4
EXPERT_SLICE = 256
NT_DIMS = (((1,), (1,)), ((), ()))


def _params(sem):
    return pltpu.CompilerParams(dimension_semantics=sem, vmem_limit_bytes=VMEM_LIMIT)


def _rope(t, c, s_lo, s_hi):
    w = t.shape[-1]
    return t * c + pltpu.roll(t, w - ROT_HALF, 1) * s_lo + pltpu.roll(t, ROT_HALF, 1) * s_hi


def _proj_body(x_ref, g1_ref, wqkv_ref, wqi_ref, wki_ref, wwi_ref, wp_ref, wgl_ref, qg_ref, kg_ref, bd_ref,
               c_ref, slo_ref, shi_ref,
               q_ref, k_ref, vt_ref, qi_ref, ki_ref, wi_ref, p_ref, gate_ref):
    x = x_ref[...]
    xn = x * lax.rsqrt(jnp.mean(x * x, axis=-1, keepdims=True) + EPS) * g1_ref[...]
    xb = xn.astype(BF16)
    c, s_lo, s_hi = c_ref[...], slo_ref[...], shi_ref[...]

    def head_norm(t, g):
        ms = jnp.dot((t * t).astype(BF16), bd_ref[...], preferred_element_type=F32)
        return t * lax.rsqrt(ms + EPS) * g

    qkv = jnp.dot(xb, wqkv_ref[...], preferred_element_type=F32)
    w = ATTN_WIDTH
    q = _rope(head_norm(qkv[:, :w], qg_ref[...]), c, s_lo, s_hi) * (HEAD_DIM ** -0.5 * LOG2_E)
    k = _rope(head_norm(qkv[:, w:2 * w], kg_ref[...]), c, s_lo, s_hi)
    q_ref[...] = q.astype(BF16)
    k_ref[...] = k.astype(BF16)
    vt_ref[0, 0] = qkv[:, 2 * w:].T.astype(BF16)
    qi = jnp.dot(xb, wqi_ref[...], preferred_element_type=F32)
    qi_ref[...] = (_rope(qi, c, s_lo, s_hi) * (IDX_DIM ** -0.5)).astype(BF16)
    ki = jnp.dot(xb, wki_ref[...], preferred_element_type=F32)
    ki = _rope(ki, c[:, :LANES], s_lo[:, :LANES], s_hi[:, :LANES])
    ki_ref[...] = ki[:, :IDX_DIM].astype(BF16)
    wi_ref[...] = jnp.dot(xb, wwi_ref[...], preferred_element_type=F32) * (IDX_HEADS ** -0.5)
    p_ref[...] = jnp.dot(xb, wp_ref[...], preferred_element_type=F32)
    gate_ref[...] = jax.nn.sigmoid(jnp.dot(xb, wgl_ref[...], preferred_element_type=F32)).astype(BF16)


def _rope_tables(seq):
    inv_freq = ROPE_THETA ** (-jnp.arange(ROT_HALF, dtype=F32) / ROT_HALF)
    ang = jnp.arange(seq, dtype=F32)[:, None] * inv_freq[None, :]
    cos, sin = jnp.cos(ang), jnp.sin(ang)
    rest = HEAD_DIM - 2 * ROT_HALF
    ones = jnp.ones((seq, rest), F32)
    zeros = jnp.zeros((seq, rest), F32)
    zh = jnp.zeros((seq, ROT_HALF), F32)
    c = jnp.concatenate([cos, cos, ones], axis=1)
    s_lo = jnp.concatenate([-sin, zh, zeros], axis=1)
    s_hi = jnp.concatenate([zh, sin, zeros], axis=1)
    tile = lambda t: jnp.tile(t, (1, N_HEADS))
    return tile(c), tile(s_lo), tile(s_hi)


def _input_projection(x2, norm1_g, w_in, q_norm_g, k_norm_g, batch, seq, tm, kb):
    n, d = x2.shape
    w = ATTN_WIDTH
    o = np.cumsum([0, w, w, w, IDX_HEADS * IDX_DIM, IDX_DIM, IDX_HEADS, POOL_WIDTH, 2 * d])
    wb = w_in.astype(BF16)
    wqkv = wb[:, o[0]:o[3]]
    wqi = wb[:, o[3]:o[4]]
    wki = jnp.pad(wb[:, o[4]:o[5]], ((0, 0), (0, LANES - IDX_DIM)))
    wwi = jnp.pad(wb[:, o[5]:o[6]], ((0, 0), (0, LANES - IDX_HEADS)))
    wp = wb[:, o[6]:o[7]]
    wgl = wb[:, o[7]:o[8]]
    bd = jnp.kron(jnp.eye(N_HEADS, dtype=F32), jnp.full((HEAD_DIM, HEAD_DIM), 1.0 / HEAD_DIM, F32)).astype(BF16)
    c, s_lo, s_hi = _rope_tables(seq)
    tps = seq // tm
    const = lambda shape: pl.BlockSpec(shape, lambda i: (0,) * len(shape))
    row = lambda width: pl.BlockSpec((tm, width), lambda i: (i, 0))
    tab = pl.BlockSpec((tm, w), lambda i: (i % tps, 0))
    per_kb = kb // tm
    out_shapes = (
        jax.ShapeDtypeStruct((n, w), BF16),
        jax.ShapeDtypeStruct((n, w), BF16),
        jax.ShapeDtypeStruct((batch, seq // kb, w, kb), BF16),
        jax.ShapeDtypeStruct((n, w), BF16),
        jax.ShapeDtypeStruct((n, IDX_DIM), BF16),
        jax.ShapeDtypeStruct((n, LANES), F32),
        jax.ShapeDtypeStruct((n, POOL_WIDTH), F32),
        jax.ShapeDtypeStruct((n, 2 * d), BF16),
    )
    out_specs = (
        row(w), row(w),
        pl.BlockSpec((1, 1, w, tm), lambda i: (i // tps, (i % tps) // per_kb, 0, (i % tps) % per_kb)),
        row(w), row(IDX_DIM), row(LANES), row(POOL_WIDTH), row(2 * d),
    )
    return pl.pallas_call(
        _proj_body,
        grid=(n // tm,),
        in_specs=[row(d), const((1, d)), const(wqkv.shape), const(wqi.shape), const(wki.shape), const(wwi.shape),
                  const(wp.shape), const(wgl.shape), const((1, w)), const((1, w)), const(bd.shape), tab, tab, tab],
        out_specs=out_specs,
        out_shape=out_shapes,
        compiler_params=_params(("parallel",)),
        name="input_projection",
    )(x2, norm1_g.reshape(1, d), wqkv, wqi, wki, wwi, wp, wgl,
      jnp.tile(q_norm_g, N_HEADS).reshape(1, w), jnp.tile(k_norm_g, N_HEADS).reshape(1, w), bd, c, s_lo, s_hi)


def _sortable(v):
    b = lax.bitcast_convert_type(v, I32)
    b = jnp.where(b == INT_MIN, 0, b)
    return jnp.where(b < 0, b ^ 0x7FFFFFFF, b)


def _dsa_body(q_ref, qi_ref, wi_ref, k_ref, vt_ref, ki_ref, o_ref, key_s, bias_s, acc_s, s_s, hi_s, lo_s,
              *, seq, tq, kb, topk):
    j = pl.program_id(1)
    nblk = ((j + 1) * tq + kb - 1) // kb
    lane = lax.broadcasted_iota(I32, (1, tq), 1)
    lim = j * tq + (lane // CHUNK + 1) * CHUNK
    row_iota = lax.broadcasted_iota(I32, (kb, tq), 0)
    wi_t = wi_ref[...].T[:IDX_HEADS, :]
    qi = qi_ref[...]

    def lanes_to_queries(t):
        return t.astype(F32).T.astype(BF16)

    qi_t = [lanes_to_queries(qi[:, h * IDX_DIM:(h + 1) * IDX_DIM]) for h in range(IDX_HEADS)]

    def rows(i):
        return pl.ds(pl.multiple_of(i * kb, kb), kb)

    def score_block(i, carry):
        kib = ki_ref[rows(i), :]
        acc = jnp.zeros((kb, tq), F32)
        for h in range(IDX_HEADS):
            lg = jnp.dot(kib, qi_t[h], preferred_element_type=F32)
            acc = acc + jnp.maximum(lg, 0.0) * wi_t[h:h + 1, :]
        key = jnp.where(i * kb + row_iota < lim, _sortable(acc), INT_MIN)
        key_s[rows(i), :] = key
        hi_s[rows(i), :] = (key >> HALF_BITS).astype(I16)
        lo_s[rows(i), :] = ((key & HALF_MASK) - HALF_BIAS).astype(I16)
        return carry

    lax.fori_loop(0, nblk, score_block, 0)

    def count(pred):
        def body(i, c):
            m = pred(key_s[rows(i), :], i * kb + row_iota)
            return c + jnp.sum(m.astype(I32).reshape(kb // SUBLANES, SUBLANES, tq), axis=0)
        c8 = lax.fori_loop(0, nblk, body, jnp.zeros((SUBLANES, tq), I32))
        return jnp.sum(c8, axis=0, keepdims=True)

    pack = 2 * SUBLANES
    one16, zero16 = jnp.ones((), I16), jnp.zeros((), I16)

    def spread16(v):
        return jnp.broadcast_to(v.astype(I16), (kb, tq))

    def count16(ref, pred):
        def body(i, c):
            m = pred(ref[rows(i), :])
            hit = jnp.where(m, one16, zero16)
            parts = [hit[r:r + pack, :] for r in range(0, kb, pack)]
            while len(parts) > 1:
                parts = [parts[r] + parts[r + 1] for r in range(0, len(parts), 2)]
            return c + parts[0]
        c16 = lax.fori_loop(0, nblk, body, jnp.zeros((pack, tq), I16))
        return jnp.sum(c16.astype(I32), axis=0, keepdims=True)

    def search16(ref, need):
        def bit(it, tu):
            cand_u = tu | lax.shift_left(jnp.int32(1), HALF_BITS - 1 - it)
            cand = spread16(cand_u - HALF_BIAS)
            return jnp.where(count16(ref, lambda blk: blk >= cand) >= need, cand_u, tu)
        return lax.fori_loop(0, HALF_BITS, bit, jnp.zeros((1, tq), I32))

    hi_u = search16(hi_s, topk)
    thr_hi = spread16(hi_u - HALF_BIAS)
    above = count16(hi_s, lambda blk: blk > thr_hi)

    def mask_low(i, carry):
        lo_s[rows(i), :] = jnp.where(hi_s[rows(i), :] == thr_hi, lo_s[rows(i), :], jnp.full((), -HALF_BIAS, I16))
        return carry

    lax.fori_loop(0, nblk, mask_low, 0)
    lo_u = search16(lo_s, topk - above)
    thr = lax.shift_left(hi_u - HALF_BIAS, HALF_BITS) | lo_u
    idx_bits = int(seq).bit_length()
    surplus = (count(lambda blk, idx: blk >= thr) != topk) & (thr != INT_MIN)
    has_tie = jnp.max(jnp.where(surplus, 1.0, 0.0)) > 0.5

    def resolve_ties():
        need = topk - count(lambda blk, idx: blk > thr)

        def index_bit(it, jj):
            cand = jj | lax.shift_left(jnp.int32(1), idx_bits - 1 - it)
            return jnp.where(count(lambda blk, idx: (blk == thr) & (idx < cand)) <= need, cand, jj)

        return lax.fori_loop(0, idx_bits, index_bit, jnp.zeros((1, tq), I32))

    tie_end = lax.cond(has_tie, resolve_ties, lambda: jnp.full((1, tq), (1 << idx_bits) - 1, I32))

    def bias_block(i, carry):
        blk = key_s[rows(i), :]
        idx = i * kb + row_iota
        sel = ((blk > thr) | ((blk == thr) & (idx < tie_end))) & (idx < lim)
        bias_s[rows(i), :] = jnp.where(sel, 0.0, NEG)
        return carry

    lax.fori_loop(0, nblk, bias_block, 0)

    q = q_ref[...]
    pair_lane = lax.broadcasted_iota(I32, (tq, 2 * HEAD_DIM), 1)
    qm = []
    for h in range(N_HEADS):
        pair = q[:, (h // 2) * 2 * HEAD_DIM:(h // 2 + 1) * 2 * HEAD_DIM]
        qm.append(lanes_to_queries(jnp.where((pair_lane // HEAD_DIM) == (h % 2), pair, jnp.zeros_like(pair))))
    acc_s[...] = jnp.zeros_like(acc_s)
    group = s_s.shape[0]

    def fold(t):
        return t.reshape(kb // SUBLANES, SUBLANES, tq)

    for g0 in range(0, N_HEADS, group):
        heads = range(g0, g0 + group)

        def score_pass(i, ms):
            bias = bias_s[rows(i), :]
            out = []
            for hh, h in enumerate(heads):
                kblk = k_ref[rows(i), (h // 2) * 2 * HEAD_DIM:(h // 2 + 1) * 2 * HEAD_DIM]
                s = jnp.dot(kblk, qm[h], preferred_element_type=F32) + bias
                s_s[hh, rows(i), :] = s
                out.append(jnp.maximum(ms[hh], jnp.max(fold(s), axis=0)))
            return tuple(out)

        ms = lax.fori_loop(0, nblk, score_pass, tuple(jnp.full((SUBLANES, tq), NEG, F32) for _ in heads))
        mx = [jnp.max(m, axis=0, keepdims=True) for m in ms]

        def value_pass(i, ls):
            out = []
            for hh, h in enumerate(heads):
                hs = slice(h * HEAD_DIM, (h + 1) * HEAD_DIM)
                p = jnp.exp2(s_s[hh, rows(i), :] - mx[hh])
                out.append(ls[hh] + jnp.sum(fold(p), axis=0))
                acc_s[hs, :] += jnp.dot(vt_ref[0, i, hs, :], p.astype(BF16), preferred_element_type=F32)
            return tuple(out)

        ls = lax.fori_loop(0, nblk, value_pass, tuple(jnp.zeros((SUBLANES, tq), F32) for _ in heads))
        for hh, h in enumerate(heads):
            hs = slice(h * HEAD_DIM, (h + 1) * HEAD_DIM)
            acc_s[hs, :] = acc_s[hs, :] / jnp.sum(ls[hh], axis=0, keepdims=True)
    o_ref[...] = acc_s[...].T.astype(BF16)


def _dsa_attention(q, qi, wi, k, vt, ki, batch, seq, tq, kb):
    n, w = q.shape
    topk = min(TOPK_MAX, seq // 4)
    nq = seq // tq
    tile = lambda width: pl.BlockSpec((tq, width), lambda b, j: (b * nq + j, 0))
    whole = lambda width: pl.BlockSpec((seq, width), lambda b, j: (b, 0))
    return pl.pallas_call(
        functools.partial(_dsa_body, seq=seq, tq=tq, kb=kb, topk=topk),
        grid=(batch, nq),
        in_specs=[tile(w), tile(w), tile(LANES), whole(w),
                  pl.BlockSpec((1, seq // kb, w, kb), lambda b, j: (b, 0, 0, 0)), whole(IDX_DIM)],
        out_specs=tile(w),
        out_shape=jax.ShapeDtypeStruct((n, w), BF16),
        scratch_shapes=[pltpu.VMEM((seq, tq), I32), pltpu.VMEM((seq, tq), F32), pltpu.VMEM((w, tq), F32),
                        pltpu.VMEM((ATTN_HEAD_GROUP, seq, tq), F32),
                        pltpu.VMEM((seq, tq), I16), pltpu.VMEM((seq, tq), I16)],
        compiler_params=_params(("parallel", "arbitrary")),
        name="dsa_attention",
    )(q, qi, wi, k, vt, ki)


def _mix_body(attn_ref, p_ref, halo_ref, gate_ref, x_ref, wa_ref, wpb_ref, wo_ref, pw_ref, ps_ref, g2_ref,
              h_ref, hn_ref, ext_s, *, tm, tps):
    st = pl.program_id(0) % tps
    ext_s[0:POOL_HALO, :] = jnp.where(st == 0, 0.0, halo_ref[...])
    ext_s[POOL_HALO:POOL_HALO + tm, :] = p_ref[...]
    t1 = (st * tm + 1 + lax.broadcasted_iota(I32, (tm, POOL_GROUP_DIM), 0)).astype(F32)
    mixed = []
    for g, win in enumerate(POOL_WINDOWS):
        ls = slice(g * POOL_GROUP_DIM, (g + 1) * POOL_GROUP_DIM)
        frame = ext_s[POOL_HALO:POOL_HALO + tm, ls]
        tot = frame
        for dlt in range(1, win):
            tot = tot + ext_s[POOL_HALO - dlt:POOL_HALO - dlt + tm, ls]
        pooled = tot / jnp.minimum(t1, float(win)) - frame
        mixed.append(jnp.dot(pooled.astype(BF16), pw_ref[g], preferred_element_type=F32))
    mixed = jnp.concatenate(mixed, axis=1) * ps_ref[...]
    y_pool = jnp.dot(mixed.astype(BF16), wpb_ref[...], preferred_element_type=F32)
    y_attn = jnp.dot(attn_ref[...], wa_ref[...], preferred_element_type=F32)
    d = y_attn.shape[1]
    gate = gate_ref[...].astype(F32)
    z = gate[:, :d] * y_attn + gate[:, d:] * y_pool
    h = x_ref[...] + jnp.dot(z.astype(BF16), wo_ref[...], preferred_element_type=F32)
    h_ref[...] = h
    hn = h * lax.rsqrt(jnp.mean(h * h, axis=-1, keepdims=True) + EPS) * g2_ref[...]
    hn_ref[...] = hn.astype(BF16)


def _mixer_output(attn, p, gate, x2, w_branch_attn, w_branch_pool, w_out, pool_w, pool_scale, norm2_g, seq, tm):
    n, d = x2.shape
    tps = seq // tm
    hb = tm // POOL_HALO
    const = lambda shape: pl.BlockSpec(shape, lambda i: (0,) * len(shape))
    row = lambda width: pl.BlockSpec((tm, width), lambda i: (i, 0))
    return pl.pallas_call(
        functools.partial(_mix_body, tm=tm, tps=tps),
        grid=(n // tm,),
        in_specs=[row(ATTN_WIDTH), row(POOL_WIDTH),
                  pl.BlockSpec((POOL_HALO, POOL_WIDTH), lambda i: (jnp.maximum(i * hb - 1, 0), 0)),
                  row(2 * d), row(d), const((ATTN_WIDTH, d)), const((POOL_WIDTH, d)), const((d, d)),
                  const(pool_w.shape), const((1, POOL_WIDTH)), const((1, d))],
        out_specs=(row(d), row(d)),
        out_shape=(jax.ShapeDtypeStruct((n, d), F32), jax.ShapeDtypeStruct((n, d), BF16)),
        scratch_shapes=[pltpu.VMEM((POOL_HALO + tm, POOL_WIDTH), F32)],
        compiler_params=_params(("parallel",)),
        name="mixer_output",
    )(attn, p, p, gate, x2, w_branch_attn.astype(BF16), w_branch_pool.astype(BF16), w_out.astype(BF16),
      pool_w.astype(BF16), pool_scale.reshape(1, POOL_WIDTH), norm2_g.reshape(1, d))


def _candidate_pairs():
    return [(a, b) for a in range(PEER_TOPK) for b in range(PEER_TOPK) if (a + 1) * (b + 1) <= PEER_TOPK]


def _sort_desc(v):
    v = list(v)
    n = len(v)
    k = 2
    while k <= n:
        j = k // 2
        while j >= 1:
            for i in range(n):
                m = i ^ j
                if m > i:
                    hi, lo = jnp.maximum(v[i], v[m]), jnp.minimum(v[i], v[m])
                    v[i], v[m] = (hi, lo) if (i & k) == 0 else (lo, hi)
            j //= 2
        k *= 2
    return v


def _merge_top(a, b):
    n = len(a)
    c = [jnp.maximum(a[i], b[n - 1 - i]) for i in range(n)]
    j = n // 2
    while j >= 1:
        for i in range(n):
            m = i ^ j
            if m > i:
                c[i], c[m] = jnp.maximum(c[i], c[m]), jnp.minimum(c[i], c[m])
        j //= 2
    return c


def _route_body(hn_ref, wqt_ref, kbig_ref, a0_ref, l0_ref, br_ref, vals_s, rank_s, ex_s, b1_s, r1_s, *, tr):
    nk, nh, kt = PEER_KEYS, PEER_HEADS, PEER_TOPK
    half_rows = nh * PEER_KEY_DIM
    qt = lax.dot_general(wqt_ref[...], hn_ref[...], NT_DIMS, preferred_element_type=F32).astype(BF16)
    for p in range(2):
        sub = jnp.dot(kbig_ref[p], qt[p * half_rows:(p + 1) * half_rows], preferred_element_type=F32)
        vals_s[p] = sub.reshape(nk, nh, tr)

    def best(p, lo, hi):
        if hi - lo == kt:
            return _sort_desc([vals_s[p, i] for i in range(lo, hi)])
        mid = (lo + hi) // 2
        return _merge_top(best(p, lo, mid), best(p, mid, hi))

    tops = [best(p, 0, nk) for p in range(2)]
    v0, v1 = tops
    tied = jnp.zeros((nh, tr), F32)
    for p in range(2):
        for a in range(kt - 1):
            tied = jnp.maximum(tied, jnp.where(tops[p][a] == tops[p][a + 1], 1.0, 0.0))
        above = [jnp.where(vals_s[p, i] >= tops[p][kt - 1], 1.0, 0.0) for i in range(nk)]
        while len(above) > 1:
            above = [above[i] + above[i + 1] for i in range(0, len(above), 2)]
        tied = jnp.maximum(tied, jnp.where(above[0] != float(kt), 1.0, 0.0))
    has_tie = jnp.max(tied) > 0.5

    pairs = _candidate_pairs()
    cand = [v0[a] + v1[b] for a, b in pairs]
    rank = [jnp.zeros((nh, tr), F32) for _ in pairs]
    for ia, (a0, a1) in enumerate(pairs):
        for ib in range(ia + 1, len(pairs)):
            b0, b1 = pairs[ib]
            if a0 <= b0 and a1 <= b1:
                rank[ib] = rank[ib] + 1.0
            else:
                wins = jnp.where(cand[ia] >= cand[ib], 1.0, 0.0)
                rank[ib] = rank[ib] + wins
                rank[ia] = rank[ia] + (1.0 - wins)
    e0 = [jnp.exp(v0[a] - v0[0]) for a in range(kt)]
    e1 = [jnp.exp(v1[b] - v1[0]) for b in range(kt)]
    width = [jnp.zeros((nh, tr), F32) for _ in range(kt)]
    z = jnp.zeros((nh, tr), F32)
    for ic, (a, b) in enumerate(pairs):
        sel = jnp.where(rank[ic] < float(kt), 1.0, 0.0)
        width[a] = width[a] + sel
        z = z + sel * (e0[a] * e1[b])
    inv_z = 1.0 / z

    def key_rows(i):
        return slice(i * nh, (i + 1) * nh)

    @pl.when(jnp.logical_not(has_tie))
    def _():
        for i in range(nk):
            x0, x1 = vals_s[0, i], vals_s[1, i]
            width_i = jnp.zeros((nh, tr), F32)
            for a in range(kt):
                width_i = jnp.where(x0 == v0[a], width[a], width_i)
            a0_ref[0, key_rows(i), :] = jnp.where(x0 >= v0[kt - 1], jnp.exp(x0 - v0[0]) * inv_z, 0.0)
            l0_ref[0, key_rows(i), :] = width_i
            above8 = v1[7] > x1
            piv = jnp.where(above8, v1[11], v1[3])
            above4 = piv > x1
            piv = jnp.where(above8, jnp.where(above4, v1[13], v1[9]), jnp.where(above4, v1[5], v1[1]))
            above2 = piv > x1
            piv = jnp.where(
                above8,
                jnp.where(above4, jnp.where(above2, v1[14], v1[12]), jnp.where(above2, v1[10], v1[8])),
                jnp.where(above4, jnp.where(above2, v1[6], v1[4]), jnp.where(above2, v1[2], v1[0])))
            pos = (jnp.where(above8, 8.0, 0.0) + jnp.where(above4, 4.0, 0.0) + jnp.where(above2, 2.0, 0.0)
                   + jnp.where(piv > x1, 1.0, 0.0))
            chosen = x1 >= v1[kt - 1]
            b1_s[key_rows(i), :] = jnp.where(chosen, jnp.exp(x1 - v1[0]), 0.0)
            r1_s[key_rows(i), :] = jnp.where(chosen, pos, float(kt))

    @pl.when(has_tie)
    def _():
        rank_s[...] = jnp.full(rank_s.shape, float(kt), F32)
        ex_s[...] = jnp.zeros(ex_s.shape, F32)
        key_iota = lax.broadcasted_iota(I32, (nk, nh, tr), 0)

        def extract(kk, carry):
            for p in range(2):
                v = vals_s[p]
                m = jnp.max(v, axis=0)
                idx = jnp.min(jnp.where(v == m[None], key_iota, nk), axis=0)
                hit = key_iota == idx[None]
                vals_s[p] = jnp.where(hit, -jnp.inf, v)
                rank_s[p] = jnp.where(hit, lax.convert_element_type(kk, F32), rank_s[p])
                ex_s[p] = jnp.where(hit, jnp.exp(m - tops[p][0])[None], ex_s[p])
            return carry

        lax.fori_loop(0, kt, extract, 0)
        r0 = rank_s[0]
        l0 = jnp.zeros((nk, nh, tr), F32)
        for a in range(kt):
            l0 = jnp.where(r0 == float(a), width[a][None], l0)
        a0_ref[0] = (ex_s[0] * inv_z[None]).reshape(nk * nh, tr)
        l0_ref[0] = l0.reshape(nk * nh, tr)
        b1_s[...] = ex_s[1].reshape(nk * nh, tr)
        r1_s[...] = rank_s[1].reshape(nk * nh, tr)

    pack = 2 * SUBLANES
    for h in range(nh):
        b1 = b1_s[pl.ds(h, nk, stride=nh), :].astype(BF16)
        r1 = r1_s[pl.ds(h, nk, stride=nh), :].astype(BF16)
        br_ref[0, h, :, 0] = b1.reshape(nk // pack, pack, tr)
        br_ref[0, h, :, 1] = r1.reshape(nk // pack, pack, tr)


def _peer_routing(hn, peer_wq, peer_subkeys, tr):
    n, d = hn.shape
    nk, nh, kd = PEER_KEYS, PEER_HEADS, PEER_KEY_DIM
    wqt = peer_wq.reshape(d, nh, 2, kd).transpose(2, 1, 3, 0).reshape(2 * nh * kd, d).astype(BF16)
    eye = jnp.eye(nh, dtype=peer_subkeys.dtype)
    kbig = jnp.einsum("hpnd,hg->pnhgd", peer_subkeys, eye).reshape(2, nk * nh, nh * kd).astype(BF16)
    rows = nk * nh
    assert tr == LANES
    out = jax.ShapeDtypeStruct((n // tr, rows, tr), F32)
    spec = pl.BlockSpec((1, rows, tr), lambda i: (i, 0, 0))
    pack = 2 * SUBLANES
    pair_shape = (nh, nk // pack, 2, pack, tr)
    return pl.pallas_call(
        functools.partial(_route_body, tr=tr),
        grid=(n // tr,),
        in_specs=[pl.BlockSpec((tr, d), lambda i: (i, 0)),
                  pl.BlockSpec(wqt.shape, lambda i: (0, 0)),
                  pl.BlockSpec(kbig.shape, lambda i: (0, 0, 0))],
        out_specs=(spec, spec, pl.BlockSpec((1,) + pair_shape, lambda i: (i, 0, 0, 0, 0, 0))),
        out_shape=(out, out, jax.ShapeDtypeStruct((n // tr,) + pair_shape, BF16)),
        scratch_shapes=[pltpu.VMEM((2, nk, nh, tr), F32), pltpu.VMEM((2, nk, nh, tr), F32),
                        pltpu.VMEM((2, nk, nh, tr), F32), pltpu.VMEM((rows, tr), F32), pltpu.VMEM((rows, tr), F32)],
        compiler_params=_params(("parallel",)),
        name="peer_routing",
    )(hn, wqt, kbig)


def _expert_body(*refs, tm, te, n_eb):
    n_slices = te // EXPERT_SLICE
    hn_ref, h_ref = refs[:2]
    u_refs = refs[2:2 + n_slices]
    vt_refs = refs[2 + n_slices:2 + 2 * n_slices]
    a0_ref, l0_ref, br_ref, y_ref, acc_s, g0_s, g1_s, ga0_s, ga1_s, hnt_s, br_s = refs[2 + 2 * n_slices:]
    step = pl.program_id(1)
    nk, nh = PEER_KEYS, PEER_HEADS
    pack = 2 * SUBLANES

    def gt(slot):
        return (g0_s, g1_s)[slot]

    def ga(slot):
        return (ga0_s, ga1_s)[slot]

    every = slice(0, te)
    slices = [slice(k * EXPERT_SLICE, (k + 1) * EXPERT_SLICE) for k in range(n_slices)]

    def project(k):
        return jnp.dot(u_refs[k][...], hnt_s[...], preferred_element_type=F32)

    def finish(act, src, dst, rs):
        act = 0.5 * act * (1.0 + lax.erf(act * (2.0 ** -0.5)))
        dst[rs, :] = src[rs, :] * act.astype(BF16)

    def value_part(src, k):
        return jnp.dot(vt_refs[k][0], src[slices[k], :], preferred_element_type=F32)

    def gate_piece(dst, il, c):
        i = step * (te // nk) + il
        g = jnp.zeros((nk, LANES), BF16)
        for h in range(nh):
            row = pl.ds(i * nh + h, 1)
            a_row = jnp.broadcast_to(a0_ref[c, row, :], (pack, LANES)).astype(BF16)
            l_row = jnp.broadcast_to(l0_ref[c, row, :], (pack, LANES)).astype(BF16)
            a_row = jnp.tile(a_row, (nk // pack, 1))
            l_row = jnp.tile(l_row, (nk // pack, 1))
            b1 = br_s[c, h, :, 0].reshape(nk, LANES)
            r1 = br_s[c, h, :, 1].reshape(nk, LANES)
            g = g + a_row * jnp.where(r1 < l_row, b1, jnp.zeros((), BF16))
        dst[il * nk:(il + 1) * nk, c * LANES:(c + 1) * LANES] = g

    def gate(dst, rs=every):
        for il in range(rs.start // nk, rs.stop // nk):
            for c in range(tm // LANES):
                gate_piece(dst, il, c)

    def apply(src):
        total = value_part(src, 0)
        for k in range(1, n_slices):
            total = total + value_part(src, k)
        acc_s[...] += total

    def activate(src, dst):
        for k in range(n_slices):
            finish(project(k), src, dst, slices[k])

    @pl.when(step == 0)
    def _():
        acc_s[...] = jnp.zeros_like(acc_s)
        hnt_s[...] = hn_ref[...].astype(F32).T.astype(BF16)
        br_s[...] = br_ref[...]
        gate(gt(0))

    @pl.when(step == 1)
    def _():
        activate(gt(0), ga(0))
        gate(gt(1))

    for p in range(2):
        @pl.when((step >= 2) & (step < n_eb) & (step % 2 == p))
        def _():
            pieces = [(il, c) for il in range(te // nk) for c in range(tm // LANES)]
            per_pass = len(pieces) // (n_slices * n_slices)
            assert per_pass * n_slices * n_slices == len(pieces) and acc_s.shape[0] == te
            for mb, blk in enumerate(slices):
                act = val = None
                for ks, cols in enumerate(slices):
                    first = (mb * n_slices + ks) * per_pass
                    for piece in pieces[first:first + per_pass]:
                        gate_piece(gt(p), *piece)
                    pu = jnp.dot(u_refs[mb][:, cols], hnt_s[cols, :], preferred_element_type=F32)
                    pv = jnp.dot(vt_refs[ks][0, blk, :], ga(p)[cols, :], preferred_element_type=F32)
                    act = pu if act is None else act + pu
                    val = pv if val is None else val + pv
                finish(act, gt(1 - p), ga(1 - p), blk)
                acc_s[blk, :] += val

    @pl.when(step == n_eb)
    def _():
        p = n_eb % 2
        apply(ga(p))
        activate(gt(1 - p), ga(1 - p))

    @pl.when(step == n_eb + 1)
    def _():
        apply(ga((n_eb + 1) % 2))
        y_ref[...] = h_ref[...] + acc_s[...].T


def _peer_experts(hn, h, peer_u, peer_v, a0, l0, br, tm, te):
    n, d = hn.shape
    ne = peer_u.shape[0]
    rows = a0.shape[1]
    tok = pl.BlockSpec((tm // LANES, rows, LANES), lambda t, e: (t, 0, 0))
    pair = pl.BlockSpec((tm // LANES,) + br.shape[1:], lambda t, e: (t, 0, 0, 0, 0, 0))
    n_eb = ne // te
    n_slices = te // EXPERT_SLICE
    u_b = peer_u.astype(BF16)
    vt_slabs = peer_v.astype(BF16).reshape(ne // EXPERT_SLICE, EXPERT_SLICE, d).transpose(0, 2, 1)
    u_specs = [pl.BlockSpec((EXPERT_SLICE, d), lambda t, s, k=k: (n_slices * jnp.clip(s - 1, 0, n_eb - 1) + k, 0))
               for k in range(n_slices)]
    vt_specs = [pl.BlockSpec((1, d, EXPERT_SLICE),
                             lambda t, s, k=k: (n_slices * jnp.clip(s - 2, 0, n_eb - 1) + k, 0, 0))
                for k in range(n_slices)]
    return pl.pallas_call(
        functools.partial(_expert_body, tm=tm, te=te, n_eb=n_eb),
        grid=(n // tm, n_eb + 2),
        in_specs=[pl.BlockSpec((tm, d), lambda t, s: (t, 0)), pl.BlockSpec((tm, d), lambda t, s: (t, 0)),
                  *u_specs, *vt_specs, tok, tok, pair],
        out_specs=pl.BlockSpec((tm, d), lambda t, s: (t, 0)),
        out_shape=jax.ShapeDtypeStruct((n, d), F32),
        scratch_shapes=[pltpu.VMEM((d, tm), F32), pltpu.VMEM((te, tm), BF16), pltpu.VMEM((te, tm), BF16),
                        pltpu.VMEM((te, tm), BF16), pltpu.VMEM((te, tm), BF16), pltpu.VMEM((d, tm), BF16),
                        pltpu.VMEM((tm // LANES,) + br.shape[1:], BF16)],
        compiler_params=_params(("parallel", "arbitrary")),
        name="peer_experts",
    )(hn, h, *([u_b] * n_slices), *([vt_slabs] * n_slices), a0, l0, br)


def _tiles(batch, seq):
    return dict(tm=256, tq=256, kb=512, tr=128, te_tm=512, te=1024)


def kernel(x, norm1_g, w_in, q_norm_g, k_norm_g, pool_w, pool_scale, w_branch_attn, w_branch_pool, w_out, norm2_g,
           peer_wq, peer_subkeys, peer_u, peer_v):
    batch, seq, d = x.shape
    t = _tiles(batch, seq)
    x2 = x.reshape(batch * seq, d)
    for l in range(norm1_g.shape[0]):
        q, k, vt, qi, ki, wi, p, gate = _input_projection(
            x2, norm1_g[l], w_in[l], q_norm_g[l], k_norm_g[l], batch, seq, t["tm"], t["kb"])
        attn = _dsa_attention(q, qi, wi, k, vt, ki, batch, seq, t["tq"], t["kb"])
        h, hn = _mixer_output(attn, p, gate, x2, w_branch_attn[l], w_branch_pool[l], w_out[l], pool_w[l],
                              pool_scale[l], norm2_g[l], seq, t["tm"])
        a0, l0, br = _peer_routing(hn, peer_wq[l], peer_subkeys[l], t["tr"])
        x2 = _peer_experts(hn, h, peer_u[l], peer_v[l], a0, l0, br, t["te_tm"], t["te"])
    return x2.reshape(batch, seq, d)
```

```python
import functools

import jax
import jax.numpy as jnp
import numpy as np
from jax import lax
from jax.experimental import pallas as pl
from jax.experimental.pallas import tpu as pltpu

CHUNK = 64
EPS = 1e-6
N_HEADS = 8
HEAD_DIM = 64
ATTN_WIDTH = N_HEADS * HEAD_DIM
ROT_HALF = HEAD_DIM // 8
ROPE_THETA = 500000.0
IDX_HEADS = 8
IDX_DIM = 64
TOPK_MAX = 256
POOL_WINDOWS = (2, 4, 8, 16)
POOL_WIDTH = 512
POOL_GROUP_DIM = POOL_WIDTH // len(POOL_WINDOWS)
POOL_HALO = 16
PEER_HEADS = 8
PEER_KEYS = 128
PEER_KEY_DIM = 64
PEER_TOPK = 16

LANES = 128
SUBLANES = 8
VMEM_LIMIT = 56 * 1024 * 1024

F32 = jnp.float32
BF16 = jnp.bfloat16
I32 = jnp.int32
I16 = jnp.int16
INT_MIN = -2147483648
HALF_BITS = 16
HALF_MASK = 0xFFFF
HALF_BIAS = 32768
NEG = -1e30
LOG2_E = 1.4426950408889634
ATTN_HEAD_GROUP = 4
EXPERT_SLICE = 256
NT_DIMS = (((1,), (1,)), ((), ()))


def _params(sem):
    return pltpu.CompilerParams(dimension_semantics=sem, vmem_limit_bytes=VMEM_LIMIT)


def _rope(t, c, s_lo, s_hi):
    w = t.shape[-1]
    return t * c + pltpu.roll(t, w - ROT_HALF, 1) * s_lo + pltpu.roll(t, ROT_HALF, 1) * s_hi


def _proj_body(x_ref, g1_ref, wqkv_ref, wqi_ref, wki_ref, wwi_ref, wp_ref, wgl_ref, qg_ref, kg_ref, bd_ref,
               c_ref, slo_ref, shi_ref,
               q_ref, k_ref, vt_ref, qi_ref, ki_ref, wi_ref, p_ref, gate_ref):
    x = x_ref[...]
    xn = x * lax.rsqrt(jnp.mean(x * x, axis=-1, keepdims=True) + EPS) * g1_ref[...]
    xb = xn.astype(BF16)
    c, s_lo, s_hi = c_ref[...], slo_ref[...], shi_ref[...]

    def head_norm(t, g):
        ms = jnp.dot((t * t).astype(BF16), bd_ref[...], preferred_element_type=F32)
        return t * lax.rsqrt(ms + EPS) * g

    qkv = jnp.dot(xb, wqkv_ref[...], preferred_element_type=F32)
    w = ATTN_WIDTH
    q = _rope(head_norm(qkv[:, :w], qg_ref[...]), c, s_lo, s_hi) * (HEAD_DIM ** -0.5 * LOG2_E)
    k = _rope(head_norm(qkv[:, w:2 * w], kg_ref[...]), c, s_lo, s_hi)
    q_ref[...] = q.astype(BF16)
    k_ref[...] = k.astype(BF16)
    vt_ref[0, 0] = qkv[:, 2 * w:].T.astype(BF16)
    qi = jnp.dot(xb, wqi_ref[...], preferred_element_type=F32)
    qi_ref[...] = (_rope(qi, c, s_lo, s_hi) * (IDX_DIM ** -0.5)).astype(BF16)
    ki = jnp.dot(xb, wki_ref[...], preferred_element_type=F32)
    ki = _rope(ki, c[:, :LANES], s_lo[:, :LANES], s_hi[:, :LANES])
    ki_ref[...] = ki[:, :IDX_DIM].astype(BF16)
    wi_ref[...] = jnp.dot(xb, wwi_ref[...], preferred_element_type=F32) * (IDX_HEADS ** -0.5)
    p_ref[...] = jnp.dot(xb, wp_ref[...], preferred_element_type=F32)
    gate_ref[...] = jax.nn.sigmoid(jnp.dot(xb, wgl_ref[...], preferred_element_type=F32)).astype(BF16)


def _rope_tables(seq):
    inv_freq = ROPE_THETA ** (-jnp.arange(ROT_HALF, dtype=F32) / ROT_HALF)
    ang = jnp.arange(seq, dtype=F32)[:, None] * inv_freq[None, :]
    cos, sin = jnp.cos(ang), jnp.sin(ang)
    rest = HEAD_DIM - 2 * ROT_HALF
    ones = jnp.ones((seq, rest), F32)
    zeros = jnp.zeros((seq, rest), F32)
    zh = jnp.zeros((seq, ROT_HALF), F32)
    c = jnp.concatenate([cos, cos, ones], axis=1)
    s_lo = jnp.concatenate([-sin, zh, zeros], axis=1)
    s_hi = jnp.concatenate([zh, sin, zeros], axis=1)
    tile = lambda t: jnp.tile(t, (1, N_HEADS))
    return tile(c), tile(s_lo), tile(s_hi)


def _input_projection(x2, norm1_g, w_in, q_norm_g, k_norm_g, batch, seq, tm, kb):
    n, d = x2.shape
    w = ATTN_WIDTH
    o = np.cumsum([0, w, w, w, IDX_HEADS * IDX_DIM, IDX_DIM, IDX_HEADS, POOL_WIDTH, 2 * d])
    wb = w_in.astype(BF16)
    wqkv = wb[:, o[0]:o[3]]
    wqi = wb[:, o[3]:o[4]]
    wki = jnp.pad(wb[:, o[4]:o[5]], ((0, 0), (0, LANES - IDX_DIM)))
    wwi = jnp.pad(wb[:, o[5]:o[6]], ((0, 0), (0, LANES - IDX_HEADS)))
    wp = wb[:, o[6]:o[7]]
    wgl = wb[:, o[7]:o[8]]
    bd = jnp.kron(jnp.eye(N_HEADS, dtype=F32), jnp.full((HEAD_DIM, HEAD_DIM), 1.0 / HEAD_DIM, F32)).astype(BF16)
    c, s_lo, s_hi = _rope_tables(seq)
    tps = seq // tm
    const = lambda shape: pl.BlockSpec(shape, lambda i: (0,) * len(shape))
    row = lambda width: pl.BlockSpec((tm, width), lambda i: (i, 0))
    tab = pl.BlockSpec((tm, w), lambda i: (i % tps, 0))
    per_kb = kb // tm
    out_shapes = (
        jax.ShapeDtypeStruct((n, w), BF16),
        jax.ShapeDtypeStruct((n, w), BF16),
        jax.ShapeDtypeStruct((batch, seq // kb, w, kb), BF16),
        jax.ShapeDtypeStruct((n, w), BF16),
        jax.ShapeDtypeStruct((n, IDX_DIM), BF16),
        jax.ShapeDtypeStruct((n, LANES), F32),
        jax.ShapeDtypeStruct((n, POOL_WIDTH), F32),
        jax.ShapeDtypeStruct((n, 2 * d), BF16),
    )
    out_specs = (
        row(w), row(w),
        pl.BlockSpec((1, 1, w, tm), lambda i: (i // tps, (i % tps) // per_kb, 0, (i % tps) % per_kb)),
        row(w), row(IDX_DIM), row(LANES), row(POOL_WIDTH), row(2 * d),
    )
    return pl.pallas_call(
        _proj_body,
        grid=(n // tm,),
        in_specs=[row(d), const((1, d)), const(wqkv.shape), const(wqi.shape), const(wki.shape), const(wwi.shape),
                  const(wp.shape), const(wgl.shape), const((1, w)), const((1, w)), const(bd.shape), tab, tab, tab],
        out_specs=out_specs,
        out_shape=out_shapes,
        compiler_params=_params(("parallel",)),
        name="input_projection",
    )(x2, norm1_g.reshape(1, d), wqkv, wqi, wki, wwi, wp, wgl,
      jnp.tile(q_norm_g, N_HEADS).reshape(1, w), jnp.tile(k_norm_g, N_HEADS).reshape(1, w), bd, c, s_lo, s_hi)


def _sortable(v):
    b = lax.bitcast_convert_type(v, I32)
    b = jnp.where(b == INT_MIN, 0, b)
    return jnp.where(b < 0, b ^ 0x7FFFFFFF, b)


def _dsa_body(q_ref, qi_ref, wi_ref, k_ref, vt_ref, ki_ref, o_ref, key_s, bias_s, acc_s, s_s, hi_s, lo_s,
              *, seq, tq, kb, topk):
    j = pl.program_id(1)
    nblk = ((j + 1) * tq + kb - 1) // kb
    lane = lax.broadcasted_iota(I32, (1, tq), 1)
    lim = j * tq + (lane // CHUNK + 1) * CHUNK
    row_iota = lax.broadcasted_iota(I32, (kb, tq), 0)
    wi_t = wi_ref[...].T[:IDX_HEADS, :]
    qi = qi_ref[...]

    def lanes_to_queries(t):
        return t.astype(F32).T.astype(BF16)

    qi_t = [lanes_to_queries(qi[:, h * IDX_DIM:(h + 1) * IDX_DIM]) for h in range(IDX_HEADS)]

    def rows(i):
        return pl.ds(pl.multiple_of(i * kb, kb), kb)

    def score_block(i, carry):
        kib = ki_ref[rows(i), :]
        acc = jnp.zeros((kb, tq), F32)
        for h in range(IDX_HEADS):
            lg = jnp.dot(kib, qi_t[h], preferred_element_type=F32)
            acc = acc + jnp.maximum(lg, 0.0) * wi_t[h:h + 1, :]
        key = jnp.where(i * kb + row_iota < lim, _sortable(acc), INT_MIN)
        key_s[rows(i), :] = key
        hi_s[rows(i), :] = (key >> HALF_BITS).astype(I16)
        lo_s[rows(i), :] = ((key & HALF_MASK) - HALF_BIAS).astype(I16)
        return carry

    lax.fori_loop(0, nblk, score_block, 0)

    def count(pred):
        def body(i, c):
            m = pred(key_s[rows(i), :], i * kb + row_iota)
            return c + jnp.sum(m.astype(I32).reshape(kb // SUBLANES, SUBLANES, tq), axis=0)
        c8 = lax.fori_loop(0, nblk, body, jnp.zeros((SUBLANES, tq), I32))
        return jnp.sum(c8, axis=0, keepdims=True)

    pack = 2 * SUBLANES
    one16, zero16 = jnp.ones((), I16), jnp.zeros((), I16)

    def spread16(v):
        return jnp.broadcast_to(v.astype(I16), (kb, tq))

    def count16(ref, pred):
        def body(i, c):
            m = pred(ref[rows(i), :])
            hit = jnp.where(m, one16, zero16)
            parts = [hit[r:r + pack, :] for r in range(0, kb, pack)]
            while len(parts) > 1:
                parts = [parts[r] + parts[r + 1] for r in range(0, len(parts), 2)]
            return c + parts[0]
        c16 = lax.fori_loop(0, nblk, body, jnp.zeros((pack, tq), I16))
        return jnp.sum(c16.astype(I32), axis=0, keepdims=True)

    def search16(ref, need):
        def bit(it, tu):
            cand_u = tu | lax.shift_left(jnp.int32(1), HALF_BITS - 1 - it)
            cand = spread16(cand_u - HALF_BIAS)
            return jnp.where(count16(ref, lambda blk: blk >= cand) >= need, cand_u, tu)
        return lax.fori_loop(0, HALF_BITS, bit, jnp.zeros((1, tq), I32))

    hi_u = search16(hi_s, topk)
    thr_hi = spread16(hi_u - HALF_BIAS)
    above = count16(hi_s, lambda blk: blk > thr_hi)

    def mask_low(i, carry):
        lo_s[rows(i), :] = jnp.where(hi_s[rows(i), :] == thr_hi, lo_s[rows(i), :], jnp.full((), -HALF_BIAS, I16))
        return carry

    lax.fori_loop(0, nblk, mask_low, 0)
    lo_u = search16(lo_s, topk - above)
    thr = lax.shift_left(hi_u - HALF_BIAS, HALF_BITS) | lo_u
    idx_bits = int(seq).bit_length()
    surplus = (count(lambda blk, idx: blk >= thr) != topk) & (thr != INT_MIN)
    has_tie = jnp.max(jnp.where(surplus, 1.0, 0.0)) > 0.5

    def resolve_ties():
        need = topk - count(lambda blk, idx: blk > thr)

        def index_bit(it, jj):
            cand = jj | lax.shift_left(jnp.int32(1), idx_bits - 1 - it)
            return jnp.where(count(lambda blk, idx: (blk == thr) & (idx < cand)) <= need, cand, jj)

        return lax.fori_loop(0, idx_bits, index_bit, jnp.zeros((1, tq), I32))

    tie_end = lax.cond(has_tie, resolve_ties, lambda: jnp.full((1, tq), (1 << idx_bits) - 1, I32))

    def bias_block(i, carry):
        blk = key_s[rows(i), :]
        idx = i * kb + row_iota
        sel = ((blk > thr) | ((blk == thr) & (idx < tie_end))) & (idx < lim)
        bias_s[rows(i), :] = jnp.where(sel, 0.0, NEG)
        return carry

    lax.fori_loop(0, nblk, bias_block, 0)

    q = q_ref[...]
    pair_lane = lax.broadcasted_iota(I32, (tq, 2 * HEAD_DIM), 1)
    qm = []
    for h in range(N_HEADS):
        pair = q[:, (h // 2) * 2 * HEAD_DIM:(h // 2 + 1) * 2 * HEAD_DIM]
        qm.append(lanes_to_queries(jnp.where((pair_lane // HEAD_DIM) == (h % 2), pair, jnp.zeros_like(pair))))
    acc_s[...] = jnp.zeros_like(acc_s)
    group = s_s.shape[0]

    def fold(t):
        return t.reshape(kb // SUBLANES, SUBLANES, tq)

    for g0 in range(0, N_HEADS, group):
        heads = range(g0, g0 + group)

        def score_pass(i, ms):
            bias = bias_s[rows(i), :]
            out = []
            for hh, h in enumerate(heads):
                kblk = k_ref[rows(i), (h // 2) * 2 * HEAD_DIM:(h // 2 + 1) * 2 * HEAD_DIM]
                s = jnp.dot(kblk, qm[h], preferred_element_type=F32) + bias
                s_s[hh, rows(i), :] = s
                out.append(jnp.maximum(ms[hh], jnp.max(fold(s), axis=0)))
            return tuple(out)

        ms = lax.fori_loop(0, nblk, score_pass, tuple(jnp.full((SUBLANES, tq), NEG, F32) for _ in heads))
        mx = [jnp.max(m, axis=0, keepdims=True) for m in ms]

        def value_pass(i, ls):
            out = []
            for hh, h in enumerate(heads):
                hs = slice(h * HEAD_DIM, (h + 1) * HEAD_DIM)
                p = jnp.exp2(s_s[hh, rows(i), :] - mx[hh])
                out.append(ls[hh] + jnp.sum(fold(p), axis=0))
                acc_s[hs, :] += jnp.dot(vt_ref[0, i, hs, :], p.astype(BF16), preferred_element_type=F32)
            return tuple(out)

        ls = lax.fori_loop(0, nblk, value_pass, tuple(jnp.zeros((SUBLANES, tq), F32) for _ in heads))
        for hh, h in enumerate(heads):
            hs = slice(h * HEAD_DIM, (h + 1) * HEAD_DIM)
            acc_s[hs, :] = acc_s[hs, :] / jnp.sum(ls[hh], axis=0, keepdims=True)
    o_ref[...] = acc_s[...].T.astype(BF16)


def _dsa_attention(q, qi, wi, k, vt, ki, batch, seq, tq, kb):
    n, w = q.shape
    topk = min(TOPK_MAX, seq // 4)
    nq = seq // tq
    tile = lambda width: pl.BlockSpec((tq, width), lambda b, j: (b * nq + j, 0))
    whole = lambda width: pl.BlockSpec((seq, width), lambda b, j: (b, 0))
    return pl.pallas_call(
        functools.partial(_dsa_body, seq=seq, tq=tq, kb=kb, topk=topk),
        grid=(batch, nq),
        in_specs=[tile(w), tile(w), tile(LANES), whole(w),
                  pl.BlockSpec((1, seq // kb, w, kb), lambda b, j: (b, 0, 0, 0)), whole(IDX_DIM)],
        out_specs=tile(w),
        out_shape=jax.ShapeDtypeStruct((n, w), BF16),
        scratch_shapes=[pltpu.VMEM((seq, tq), I32), pltpu.VMEM((seq, tq), F32), pltpu.VMEM((w, tq), F32),
                        pltpu.VMEM((ATTN_HEAD_GROUP, seq, tq), F32),
                        pltpu.VMEM((seq, tq), I16), pltpu.VMEM((seq, tq), I16)],
        compiler_params=_params(("parallel", "arbitrary")),
        name="dsa_attention",
    )(q, qi, wi, k, vt, ki)


def _mix_body(attn_ref, p_ref, halo_ref, gate_ref, x_ref, wa_ref, wpb_ref, wo_ref, pw_ref, ps_ref, g2_ref,
              h_ref, hn_ref, ext_s, *, tm, tps):
    st = pl.program_id(0) % tps
    ext_s[0:POOL_HALO, :] = jnp.where(st == 0, 0.0, halo_ref[...])
    ext_s[POOL_HALO:POOL_HALO + tm, :] = p_ref[...]
    t1 = (st * tm + 1 + lax.broadcasted_iota(I32, (tm, POOL_GROUP_DIM), 0)).astype(F32)
    mixed = []
    for g, win in enumerate(POOL_WINDOWS):
        ls = slice(g * POOL_GROUP_DIM, (g + 1) * POOL_GROUP_DIM)
        frame = ext_s[POOL_HALO:POOL_HALO + tm, ls]
        tot = frame
        for dlt in range(1, win):
            tot = tot + ext_s[POOL_HALO - dlt:POOL_HALO - dlt + tm, ls]
        pooled = tot / jnp.minimum(t1, float(win)) - frame
        mixed.append(jnp.dot(pooled.astype(BF16), pw_ref[g], preferred_element_type=F32))
    mixed = jnp.concatenate(mixed, axis=1) * ps_ref[...]
    y_pool = jnp.dot(mixed.astype(BF16), wpb_ref[...], preferred_element_type=F32)
    y_attn = jnp.dot(attn_ref[...], wa_ref[...], preferred_element_type=F32)
    d = y_attn.shape[1]
    gate = gate_ref[...].astype(F32)
    z = gate[:, :d] * y_attn + gate[:, d:] * y_pool
    h = x_ref[...] + jnp.dot(z.astype(BF16), wo_ref[...], preferred_element_type=F32)
    h_ref[...] = h
    hn = h * lax.rsqrt(jnp.mean(h * h, axis=-1, keepdims=True) + EPS) * g2_ref[...]
    hn_ref[...] = hn.astype(BF16)


def _mixer_output(attn, p, gate, x2, w_branch_attn, w_branch_pool, w_out, pool_w, pool_scale, norm2_g, seq, tm):
    n, d = x2.shape
    tps = seq // tm
    hb = tm // POOL_HALO
    const = lambda shape: pl.BlockSpec(shape, lambda i: (0,) * len(shape))
    row = lambda width: pl.BlockSpec((tm, width), lambda i: (i, 0))
    return pl.pallas_call(
        functools.partial(_mix_body, tm=tm, tps=tps),
        grid=(n // tm,),
        in_specs=[row(ATTN_WIDTH), row(POOL_WIDTH),
                  pl.BlockSpec((POOL_HALO, POOL_WIDTH), lambda i: (jnp.maximum(i * hb - 1, 0), 0)),
                  row(2 * d), row(d), const((ATTN_WIDTH, d)), const((POOL_WIDTH, d)), const((d, d)),
                  const(pool_w.shape), const((1, POOL_WIDTH)), const((1, d))],
        out_specs=(row(d), row(d)),
        out_shape=(jax.ShapeDtypeStruct((n, d), F32), jax.ShapeDtypeStruct((n, d), BF16)),
        scratch_shapes=[pltpu.VMEM((POOL_HALO + tm, POOL_WIDTH), F32)],
        compiler_params=_params(("parallel",)),
        name="mixer_output",
    )(attn, p, p, gate, x2, w_branch_attn.astype(BF16), w_branch_pool.astype(BF16), w_out.astype(BF16),
      pool_w.astype(BF16), pool_scale.reshape(1, POOL_WIDTH), norm2_g.reshape(1, d))


def _candidate_pairs():
    return [(a, b) for a in range(PEER_TOPK) for b in range(PEER_TOPK) if (a + 1) * (b + 1) <= PEER_TOPK]


def _sort_desc(v):
    v = list(v)
    n = len(v)
    k = 2
    while k <= n:
        j = k // 2
        while j >= 1:
            for i in range(n):
                m = i ^ j
                if m > i:
                    hi, lo = jnp.maximum(v[i], v[m]), jnp.minimum(v[i], v[m])
                    v[i], v[m] = (hi, lo) if (i & k) == 0 else (lo, hi)
            j //= 2
        k *= 2
    return v


def _merge_top(a, b):
    n = len(a)
    c = [jnp.maximum(a[i], b[n - 1 - i]) for i in range(n)]
    j = n // 2
    while j >= 1:
        for i in range(n):
            m = i ^ j
            if m > i:
                c[i], c[m] = jnp.maximum(c[i], c[m]), jnp.minimum(c[i], c[m])
        j //= 2
    return c


def _route_body(hn_ref, wqt_ref, kbig_ref, a0_ref, l0_ref, br_ref, vals_s, rank_s, ex_s, b1_s, r1_s, *, tr):
    nk, nh, kt = PEER_KEYS, PEER_HEADS, PEER_TOPK
    half_rows = nh * PEER_KEY_DIM
    qt = lax.dot_general(wqt_ref[...], hn_ref[...], NT_DIMS, preferred_element_type=F32).astype(BF16)
    for p in range(2):
        sub = jnp.dot(kbig_ref[p], qt[p * half_rows:(p + 1) * half_rows], preferred_element_type=F32)
        vals_s[p] = sub.reshape(nk, nh, tr)

    def best(p, lo, hi):
        if hi - lo == kt:
            return _sort_desc([vals_s[p, i] for i in range(lo, hi)])
        mid = (lo + hi) // 2
        return _merge_top(best(p, lo, mid), best(p, mid, hi))

    tops = [best(p, 0, nk) for p in range(2)]
    v0, v1 = tops
    tied = jnp.zeros((nh, tr), F32)
    for p in range(2):
        for a in range(kt - 1):
            tied = jnp.maximum(tied, jnp.where(tops[p][a] == tops[p][a + 1], 1.0, 0.0))
        above = [jnp.where(vals_s[p, i] >= tops[p][kt - 1], 1.0, 0.0) for i in range(nk)]
        while len(above) > 1:
            above = [above[i] + above[i + 1] for i in range(0, len(above), 2)]
        tied = jnp.maximum(tied, jnp.where(above[0] != float(kt), 1.0, 0.0))
    has_tie = jnp.max(tied) > 0.5

    pairs = _candidate_pairs()
    cand = [v0[a] + v1[b] for a, b in pairs]
    rank = [jnp.zeros((nh, tr), F32) for _ in pairs]
    for ia, (a0, a1) in enumerate(pairs):
        for ib in range(ia + 1, len(pairs)):
            b0, b1 = pairs[ib]
            if a0 <= b0 and a1 <= b1:
                rank[ib] = rank[ib] + 1.0
            else:
                wins = jnp.where(cand[ia] >= cand[ib], 1.0, 0.0)
                rank[ib] = rank[ib] + wins
                rank[ia] = rank[ia] + (1.0 - wins)
    e0 = [jnp.exp(v0[a] - v0[0]) for a in range(kt)]
    e1 = [jnp.exp(v1[b] - v1[0]) for b in range(kt)]
    width = [jnp.zeros((nh, tr), F32) for _ in range(kt)]
    z = jnp.zeros((nh, tr), F32)
    for ic, (a, b) in enumerate(pairs):
        sel = jnp.where(rank[ic] < float(kt), 1.0, 0.0)
        width[a] = width[a] + sel
        z = z + sel * (e0[a] * e1[b])
    inv_z = 1.0 / z

    def key_rows(i):
        return slice(i * nh, (i + 1) * nh)

    @pl.when(jnp.logical_not(has_tie))
    def _():
        for i in range(nk):
            x0, x1 = vals_s[0, i], vals_s[1, i]
            width_i = jnp.zeros((nh, tr), F32)
            for a in range(kt):
                width_i = jnp.where(x0 == v0[a], width[a], width_i)
            a0_ref[0, key_rows(i), :] = jnp.where(x0 >= v0[kt - 1], jnp.exp(x0 - v0[0]) * inv_z, 0.0)
            l0_ref[0, key_rows(i), :] = width_i
            above8 = v1[7] > x1
            piv = jnp.where(above8, v1[11], v1[3])
            above4 = piv > x1
            piv = jnp.where(above8, jnp.where(above4, v1[13], v1[9]), jnp.where(above4, v1[5], v1[1]))
            above2 = piv > x1
            piv = jnp.where(
                above8,
                jnp.where(above4, jnp.where(above2, v1[14], v1[12]), jnp.where(above2, v1[10], v1[8])),
                jnp.where(above4, jnp.where(above2, v1[6], v1[4]), jnp.where(above2, v1[2], v1[0])))
            pos = (jnp.where(above8, 8.0, 0.0) + jnp.where(above4, 4.0, 0.0) + jnp.where(above2, 2.0, 0.0)
                   + jnp.where(piv > x1, 1.0, 0.0))
            chosen = x1 >= v1[kt - 1]
            b1_s[key_rows(i), :] = jnp.where(chosen, jnp.exp(x1 - v1[0]), 0.0)
            r1_s[key_rows(i), :] = jnp.where(chosen, pos, float(kt))

    @pl.when(has_tie)
    def _():
        rank_s[...] = jnp.full(rank_s.shape, float(kt), F32)
        ex_s[...] = jnp.zeros(ex_s.shape, F32)
        key_iota = lax.broadcasted_iota(I32, (nk, nh, tr), 0)

        def extract(kk, carry):
            for p in range(2):
                v = vals_s[p]
                m = jnp.max(v, axis=0)
                idx = jnp.min(jnp.where(v == m[None], key_iota, nk), axis=0)
                hit = key_iota == idx[None]
                vals_s[p] = jnp.where(hit, -jnp.inf, v)
                rank_s[p] = jnp.where(hit, lax.convert_element_type(kk, F32), rank_s[p])
                ex_s[p] = jnp.where(hit, jnp.exp(m - tops[p][0])[None], ex_s[p])
            return carry

        lax.fori_loop(0, kt, extract, 0)
        r0 = rank_s[0]
        l0 = jnp.zeros((nk, nh, tr), F32)
        for a in range(kt):
            l0 = jnp.where(r0 == float(a), width[a][None], l0)
        a0_ref[0] = (ex_s[0] * inv_z[None]).reshape(nk * nh, tr)
        l0_ref[0] = l0.reshape(nk * nh, tr)
        b1_s[...] = ex_s[1].reshape(nk * nh, tr)
        r1_s[...] = rank_s[1].reshape(nk * nh, tr)

    pack = 2 * SUBLANES
    for h in range(nh):
        b1 = b1_s[pl.ds(h, nk, stride=nh), :].astype(BF16)
        r1 = r1_s[pl.ds(h, nk, stride=nh), :].astype(BF16)
        br_ref[0, h, :, 0] = b1.reshape(nk // pack, pack, tr)
        br_ref[0, h, :, 1] = r1.reshape(nk // pack, pack, tr)


def _peer_routing(hn, peer_wq, peer_subkeys, tr):
    n, d = hn.shape
    nk, nh, kd = PEER_KEYS, PEER_HEADS, PEER_KEY_DIM
    wqt = peer_wq.reshape(d, nh, 2, kd).transpose(2, 1, 3, 0).reshape(2 * nh * kd, d).astype(BF16)
    eye = jnp.eye(nh, dtype=peer_subkeys.dtype)
    kbig = jnp.einsum("hpnd,hg->pnhgd", peer_subkeys, eye).reshape(2, nk * nh, nh * kd).astype(BF16)
    rows = nk * nh
    assert tr == LANES
    out = jax.ShapeDtypeStruct((n // tr, rows, tr), F32)
    spec = pl.BlockSpec((1, rows, tr), lambda i: (i, 0, 0))
    pack = 2 * SUBLANES
    pair_shape = (nh, nk // pack, 2, pack, tr)
    return pl.pallas_call(
        functools.partial(_route_body, tr=tr),
        grid=(n // tr,),
        in_specs=[pl.BlockSpec((tr, d), lambda i: (i, 0)),
                  pl.BlockSpec(wqt.shape, lambda i: (0, 0)),
                  pl.BlockSpec(kbig.shape, lambda i: (0, 0, 0))],
        out_specs=(spec, spec, pl.BlockSpec((1,) + pair_shape, lambda i: (i, 0, 0, 0, 0, 0))),
        out_shape=(out, out, jax.ShapeDtypeStruct((n // tr,) + pair_shape, BF16)),
        scratch_shapes=[pltpu.VMEM((2, nk, nh, tr), F32), pltpu.VMEM((2, nk, nh, tr), F32),
                        pltpu.VMEM((2, nk, nh, tr), F32), pltpu.VMEM((rows, tr), F32), pltpu.VMEM((rows, tr), F32)],
        compiler_params=_params(("parallel",)),
        name="peer_routing",
    )(hn, wqt, kbig)


def _expert_body(*refs, tm, te, n_eb):
    n_slices = te // EXPERT_SLICE
    hn_ref, h_ref = refs[:2]
    u_refs = refs[2:2 + n_slices]
    vt_refs = refs[2 + n_slices:2 + 2 * n_slices]
    a0_ref, l0_ref, br_ref, y_ref, acc_s, g0_s, g1_s, ga0_s, ga1_s, hnt_s, br_s = refs[2 + 2 * n_slices:]
    step = pl.program_id(1)
    nk, nh = PEER_KEYS, PEER_HEADS
    pack = 2 * SUBLANES

    def gt(slot):
        return (g0_s, g1_s)[slot]

    def ga(slot):
        return (ga0_s, ga1_s)[slot]

    every = slice(0, te)
    slices = [slice(k * EXPERT_SLICE, (k + 1) * EXPERT_SLICE) for k in range(n_slices)]

    def project(k):
        return jnp.dot(u_refs[k][...], hnt_s[...], preferred_element_type=F32)

    def finish(act, src, dst, rs):
        act = 0.5 * act * (1.0 + lax.erf(act * (2.0 ** -0.5)))
        dst[rs, :] = src[rs, :] * act.astype(BF16)

    def value_part(src, k):
        return jnp.dot(vt_refs[k][0], src[slices[k], :], preferred_element_type=F32)

    def gate_piece(dst, il, c):
        i = step * (te // nk) + il
        g = jnp.zeros((nk, LANES), BF16)
        for h in range(nh):
            row = pl.ds(i * nh + h, 1)
            a_row = jnp.broadcast_to(a0_ref[c, row, :], (pack, LANES)).astype(BF16)
            l_row = jnp.broadcast_to(l0_ref[c, row, :], (pack, LANES)).astype(BF16)
            a_row = jnp.tile(a_row, (nk // pack, 1))
            l_row = jnp.tile(l_row, (nk // pack, 1))
            b1 = br_s[c, h, :, 0].reshape(nk, LANES)
            r1 = br_s[c, h, :, 1].reshape(nk, LANES)
            g = g + a_row * jnp.where(r1 < l_row, b1, jnp.zeros((), BF16))
        dst[il * nk:(il + 1) * nk, c * LANES:(c + 1) * LANES] = g

    def gate(dst, rs=every):
        for il in range(rs.start // nk, rs.stop // nk):
            for c in range(tm // LANES):
                gate_piece(dst, il, c)

    def apply(src):
        total = value_part(src, 0)
        for k in range(1, n_slices):
            total = total + value_part(src, k)
        acc_s[...] += total

    def activate(src, dst):
        for k in range(n_slices):
            finish(project(k), src, dst, slices[k])

    @pl.when(step == 0)
    def _():
        acc_s[...] = jnp.zeros_like(acc_s)
        hnt_s[...] = hn_ref[...].astype(F32).T.astype(BF16)
        br_s[...] = br_ref[...]
        gate(gt(0))

    @pl.when(step == 1)
    def _():
        activate(gt(0), ga(0))
        gate(gt(1))

    for p in range(2):
        @pl.when((step >= 2) & (step < n_eb) & (step % 2 == p))
        def _():
            pieces = [(il, c) for il in range(te // nk) for c in range(tm // LANES)]
            d_model = acc_s.shape[0]
            d_slices = [slice(k * EXPERT_SLICE, (k + 1) * EXPERT_SLICE) for k in range(d_model // EXPERT_SLICE)]
            act_tasks = [("act", mb, ks) for mb in range(n_slices) for ks in range(len(d_slices))]
            val_tasks = [("val", db, ks) for db in range(len(d_slices)) for ks in range(n_slices)]
            tasks = [t for pair in zip(act_tasks, val_tasks) for t in pair]
            per_task = len(pieces) // len(tasks)
            assert per_task * len(tasks) == len(pieces)
            sums = {}
            for ti, (kind, blk_i, ks) in enumerate(tasks):
                for piece in pieces[ti * per_task:(ti + 1) * per_task]:
                    gate_piece(gt(p), *piece)
                if kind == "act":
                    part = jnp.dot(u_refs[blk_i][:, d_slices[ks]], hnt_s[d_slices[ks], :],
                                   preferred_element_type=F32)
                    last = ks == len(d_slices) - 1
                else:
                    part = jnp.dot(vt_refs[ks][0, d_slices[blk_i], :], ga(p)[slices[ks], :],
                                   preferred_element_type=F32)
                    last = ks == n_slices - 1
                key = (kind, blk_i)
                sums[key] = part if key not in sums else sums[key] + part
                if last and kind == "act":
                    finish(sums.pop(key), gt(1 - p), ga(1 - p), slices[blk_i])
                elif last:
                    acc_s[d_slices[blk_i], :] += sums.pop(key)

    @pl.when(step == n_eb)
    def _():
        p = n_eb % 2
        apply(ga(p))
        activate(gt(1 - p), ga(1 - p))

    @pl.when(step == n_eb + 1)
    def _():
        apply(ga((n_eb + 1) % 2))
        y_ref[...] = h_ref[...] + acc_s[...].T


def _peer_experts(hn, h, peer_u, peer_v, a0, l0, br, tm, te):
    n, d = hn.shape
    ne = peer_u.shape[0]
    rows = a0.shape[1]
    once = pl.Buffered(1)
    tok = pl.BlockSpec((tm // LANES, rows, LANES), lambda t, e: (t, 0, 0), pipeline_mode=once)
    pair = pl.BlockSpec((tm // LANES,) + br.shape[1:], lambda t, e: (t, 0, 0, 0, 0, 0), pipeline_mode=once)
    n_eb = ne // te
    n_slices = te // EXPERT_SLICE
    u_b = peer_u.astype(BF16)
    vt_slabs = peer_v.astype(BF16).reshape(ne // EXPERT_SLICE, EXPERT_SLICE, d).transpose(0, 2, 1)
    u_specs = [pl.BlockSpec((EXPERT_SLICE, d), lambda t, s, k=k: (n_slices * jnp.clip(s - 1, 0, n_eb - 1) + k, 0))
               for k in range(n_slices)]
    vt_specs = [pl.BlockSpec((1, d, EXPERT_SLICE),
                             lambda t, s, k=k: (n_slices * jnp.clip(s - 2, 0, n_eb - 1) + k, 0, 0))
                for k in range(n_slices)]
    return pl.pallas_call(
        functools.partial(_expert_body, tm=tm, te=te, n_eb=n_eb),
        grid=(n // tm, n_eb + 2),
        in_specs=[pl.BlockSpec((tm, d), lambda t, s: (t, 0)),
                  pl.BlockSpec((tm, d), lambda t, s: (t, 0), pipeline_mode=once),
                  *u_specs, *vt_specs, tok, tok, pair],
        out_specs=pl.BlockSpec((tm, d), lambda t, s: (t, 0)),
        out_shape=jax.ShapeDtypeStruct((n, d), F32),
        scratch_shapes=[pltpu.VMEM((d, tm), F32), pltpu.VMEM((te, tm), BF16), pltpu.VMEM((te, tm), BF16),
                        pltpu.VMEM((te, tm), BF16), pltpu.VMEM((te, tm), BF16), pltpu.VMEM((d, tm), BF16),
                        pltpu.VMEM((tm // LANES,) + br.shape[1:], BF16)],
        compiler_params=_params(("parallel", "arbitrary")),
        name="peer_experts",
    )(hn, h, *([u_b] * n_slices), *([vt_slabs] * n_slices), a0, l0, br)


def _tiles(batch, seq):
    return dict(tm=256, tq=256, kb=512, tr=128, te_tm=1024, te=512)


def kernel(x, norm1_g, w_in, q_norm_g, k_norm_g, pool_w, pool_scale, w_branch_attn, w_branch_pool, w_out, norm2_g,
           peer_wq, peer_subkeys, peer_u, peer_v):
    batch, seq, d = x.shape
    t = _tiles(batch, seq)
    x2 = x.reshape(batch * seq, d)
    for l in range(norm1_g.shape[0]):
        q, k, vt, qi, ki, wi, p, gate = _input_projection(
            x2, norm1_g[l], w_in[l], q_norm_g[l], k_norm_g[l], batch, seq, t["tm"], t["kb"])
        attn = _dsa_attention(q, qi, wi, k, vt, ki, batch, seq, t["tq"], t["kb"])
        h, hn = _mixer_output(attn, p, gate, x2, w_branch_attn[l], w_branch_pool[l], w_out[l], pool_w[l],
                              pool_scale[l], norm2_g[l], seq, t["tm"])
        a0, l0, br = _peer_routing(hn, peer_wq[l], peer_subkeys[l], t["tr"])
        x2 = _peer_experts(hn, h, peer_u[l], peer_v[l], a0, l0, br, t["te_tm"], t["te"])
    return x2.reshape(batch, seq, d)
```

```python
import functools

import jax
import jax.numpy as jnp
import numpy as np
from jax import lax
from jax.experimental import pallas as pl
from jax.experimental.pallas import tpu as pltpu

CHUNK = 64
EPS = 1e-6
N_HEADS = 8
HEAD_DIM = 64
ATTN_WIDTH = N_HEADS * HEAD_DIM
ROT_HALF = HEAD_DIM // 8
ROPE_THETA = 500000.0
IDX_HEADS = 8
IDX_DIM = 64
TOPK_MAX = 256
POOL_WINDOWS = (2, 4, 8, 16)
POOL_WIDTH = 512
POOL_GROUP_DIM = POOL_WIDTH // len(POOL_WINDOWS)
POOL_HALO = 16
PEER_HEADS = 8
PEER_KEYS = 128
PEER_KEY_DIM = 64
PEER_TOPK = 16

LANES = 128
SUBLANES = 8
VMEM_LIMIT = 56 * 1024 * 1024

F32 = jnp.float32
BF16 = jnp.bfloat16
I32 = jnp.int32
I16 = jnp.int16
INT_MIN = -2147483648
HALF_BITS = 16
HALF_MASK = 0xFFFF
HALF_BIAS = 32768
NEG = -1e30
LOG2_E = 1.4426950408889634
ATTN_HEAD_GROUP = 4
EXPERT_SLICE = 256
MATMUL_ROWS = 512
NT_DIMS = (((1,), (1,)), ((), ()))


def _params(sem):
    return pltpu.CompilerParams(dimension_semantics=sem, vmem_limit_bytes=VMEM_LIMIT)


def _rope(t, c, s_lo, s_hi):
    w = t.shape[-1]
    return t * c + pltpu.roll(t, w - ROT_HALF, 1) * s_lo + pltpu.roll(t, ROT_HALF, 1) * s_hi


def _proj_body(x_ref, g1_ref, wqkv_ref, wqi_ref, wki_ref, wwi_ref, wp_ref, wgl_ref, qg_ref, kg_ref, bd_ref,
               c_ref, slo_ref, shi_ref,
               q_ref, k_ref, vt_ref, qi_ref, ki_ref, wi_ref, p_ref, gate_ref):
    x = x_ref[...]
    xn = x * lax.rsqrt(jnp.mean(x * x, axis=-1, keepdims=True) + EPS) * g1_ref[...]
    xb = xn.astype(BF16)
    c, s_lo, s_hi = c_ref[...], slo_ref[...], shi_ref[...]

    def head_norm(t, g):
        ms = jnp.dot((t * t).astype(BF16), bd_ref[...], preferred_element_type=F32)
        return t * lax.rsqrt(ms + EPS) * g

    qkv = jnp.dot(xb, wqkv_ref[...], preferred_element_type=F32)
    w = ATTN_WIDTH
    q = _rope(head_norm(qkv[:, :w], qg_ref[...]), c, s_lo, s_hi) * (HEAD_DIM ** -0.5 * LOG2_E)
    k = _rope(head_norm(qkv[:, w:2 * w], kg_ref[...]), c, s_lo, s_hi)
    q_ref[...] = q.astype(BF16)
    k_ref[...] = k.astype(BF16)
    vt_ref[0, 0] = qkv[:, 2 * w:].T.astype(BF16)
    qi = jnp.dot(xb, wqi_ref[...], preferred_element_type=F32)
    qi_ref[...] = (_rope(qi, c, s_lo, s_hi) * (IDX_DIM ** -0.5)).astype(BF16)
    ki = jnp.dot(xb, wki_ref[...], preferred_element_type=F32)
    ki = _rope(ki, c[:, :LANES], s_lo[:, :LANES], s_hi[:, :LANES])
    ki_ref[...] = ki[:, :IDX_DIM].astype(BF16)
    wi_ref[...] = jnp.dot(xb, wwi_ref[...], preferred_element_type=F32) * (IDX_HEADS ** -0.5)
    p_ref[...] = jnp.dot(xb, wp_ref[...], preferred_element_type=F32)
    gate_ref[...] = jax.nn.sigmoid(jnp.dot(xb, wgl_ref[...], preferred_element_type=F32)).astype(BF16)


def _rope_tables(seq):
    inv_freq = ROPE_THETA ** (-jnp.arange(ROT_HALF, dtype=F32) / ROT_HALF)
    ang = jnp.arange(seq, dtype=F32)[:, None] * inv_freq[None, :]
    cos, sin = jnp.cos(ang), jnp.sin(ang)
    rest = HEAD_DIM - 2 * ROT_HALF
    ones = jnp.ones((seq, rest), F32)
    zeros = jnp.zeros((seq, rest), F32)
    zh = jnp.zeros((seq, ROT_HALF), F32)
    c = jnp.concatenate([cos, cos, ones], axis=1)
    s_lo = jnp.concatenate([-sin, zh, zeros], axis=1)
    s_hi = jnp.concatenate([zh, sin, zeros], axis=1)
    tile = lambda t: jnp.tile(t, (1, N_HEADS))
    return tile(c), tile(s_lo), tile(s_hi)


def _input_projection(x2, norm1_g, w_in, q_norm_g, k_norm_g, batch, seq, tm, kb):
    n, d = x2.shape
    w = ATTN_WIDTH
    o = np.cumsum([0, w, w, w, IDX_HEADS * IDX_DIM, IDX_DIM, IDX_HEADS, POOL_WIDTH, 2 * d])
    wb = w_in.astype(BF16)
    wqkv = wb[:, o[0]:o[3]]
    wqi = wb[:, o[3]:o[4]]
    wki = jnp.pad(wb[:, o[4]:o[5]], ((0, 0), (0, LANES - IDX_DIM)))
    wwi = jnp.pad(wb[:, o[5]:o[6]], ((0, 0), (0, LANES - IDX_HEADS)))
    wp = wb[:, o[6]:o[7]]
    wgl = wb[:, o[7]:o[8]]
    bd = jnp.kron(jnp.eye(N_HEADS, dtype=F32), jnp.full((HEAD_DIM, HEAD_DIM), 1.0 / HEAD_DIM, F32)).astype(BF16)
    c, s_lo, s_hi = _rope_tables(seq)
    tps = seq // tm
    const = lambda shape: pl.BlockSpec(shape, lambda i: (0,) * len(shape))
    row = lambda width: pl.BlockSpec((tm, width), lambda i: (i, 0))
    tab = pl.BlockSpec((tm, w), lambda i: (i % tps, 0))
    per_kb = kb // tm
    out_shapes = (
        jax.ShapeDtypeStruct((n, w), BF16),
        jax.ShapeDtypeStruct((n, w), BF16),
        jax.ShapeDtypeStruct((batch, seq // kb, w, kb), BF16),
        jax.ShapeDtypeStruct((n, w), BF16),
        jax.ShapeDtypeStruct((n, IDX_DIM), BF16),
        jax.ShapeDtypeStruct((n, LANES), F32),
        jax.ShapeDtypeStruct((n, POOL_WIDTH), F32),
        jax.ShapeDtypeStruct((n, 2 * d), BF16),
    )
    out_specs = (
        row(w), row(w),
        pl.BlockSpec((1, 1, w, tm), lambda i: (i // tps, (i % tps) // per_kb, 0, (i % tps) % per_kb)),
        row(w), row(IDX_DIM), row(LANES), row(POOL_WIDTH), row(2 * d),
    )
    return pl.pallas_call(
        _proj_body,
        grid=(n // tm,),
        in_specs=[row(d), const((1, d)), const(wqkv.shape), const(wqi.shape), const(wki.shape), const(wwi.shape),
                  const(wp.shape), const(wgl.shape), const((1, w)), const((1, w)), const(bd.shape), tab, tab, tab],
        out_specs=out_specs,
        out_shape=out_shapes,
        compiler_params=_params(("parallel",)),
        name="input_projection",
    )(x2, norm1_g.reshape(1, d), wqkv, wqi, wki, wwi, wp, wgl,
      jnp.tile(q_norm_g, N_HEADS).reshape(1, w), jnp.tile(k_norm_g, N_HEADS).reshape(1, w), bd, c, s_lo, s_hi)


def _sortable(v):
    b = lax.bitcast_convert_type(v, I32)
    b = jnp.where(b == INT_MIN, 0, b)
    return jnp.where(b < 0, b ^ 0x7FFFFFFF, b)


def _dsa_body(q_ref, qi_ref, wi_ref, k_ref, vt_ref, ki_ref, o_ref, key_s, bias_s, acc_s, s_s, hi_s, lo_s,
              *, seq, tq, kb, topk):
    j = pl.program_id(1)
    nblk = ((j + 1) * tq + kb - 1) // kb
    lane = lax.broadcasted_iota(I32, (1, tq), 1)
    lim = j * tq + (lane // CHUNK + 1) * CHUNK
    row_iota = lax.broadcasted_iota(I32, (kb, tq), 0)
    wi_t = wi_ref[...].T[:IDX_HEADS, :]
    qi = qi_ref[...]

    def lanes_to_queries(t):
        return t.astype(F32).T.astype(BF16)

    qi_t = [lanes_to_queries(qi[:, h * IDX_DIM:(h + 1) * IDX_DIM]) for h in range(IDX_HEADS)]

    def rows(i):
        return pl.ds(pl.multiple_of(i * kb, kb), kb)

    def score_block(i, carry):
        kib = ki_ref[rows(i), :]
        acc = jnp.zeros((kb, tq), F32)
        for h in range(IDX_HEADS):
            lg = jnp.dot(kib, qi_t[h], preferred_element_type=F32)
            acc = acc + jnp.maximum(lg, 0.0) * wi_t[h:h + 1, :]
        key = jnp.where(i * kb + row_iota < lim, _sortable(acc), INT_MIN)
        key_s[rows(i), :] = key
        hi_s[rows(i), :] = (key >> HALF_BITS).astype(I16)
        lo_s[rows(i), :] = ((key & HALF_MASK) - HALF_BIAS).astype(I16)
        return carry

    lax.fori_loop(0, nblk, score_block, 0)

    def count(pred):
        def body(i, c):
            m = pred(key_s[rows(i), :], i * kb + row_iota)
            return c + jnp.sum(m.astype(I32).reshape(kb // SUBLANES, SUBLANES, tq), axis=0)
        c8 = lax.fori_loop(0, nblk, body, jnp.zeros((SUBLANES, tq), I32))
        return jnp.sum(c8, axis=0, keepdims=True)

    pack = 2 * SUBLANES
    one16, zero16 = jnp.ones((), I16), jnp.zeros((), I16)

    def spread16(v):
        return jnp.broadcast_to(v.astype(I16), (kb, tq))

    def count16(ref, pred):
        def body(i, c):
            m = pred(ref[rows(i), :])
            hit = jnp.where(m, one16, zero16)
            parts = [hit[r:r + pack, :] for r in range(0, kb, pack)]
            while len(parts) > 1:
                parts = [parts[r] + parts[r + 1] for r in range(0, len(parts), 2)]
            return c + parts[0]
        c16 = lax.fori_loop(0, nblk, body, jnp.zeros((pack, tq), I16))
        return jnp.sum(c16.astype(I32), axis=0, keepdims=True)

    def search16(ref, need):
        def bit(it, tu):
            cand_u = tu | lax.shift_left(jnp.int32(1), HALF_BITS - 1 - it)
            cand = spread16(cand_u - HALF_BIAS)
            return jnp.where(count16(ref, lambda blk: blk >= cand) >= need, cand_u, tu)
        return lax.fori_loop(0, HALF_BITS, bit, jnp.zeros((1, tq), I32))

    hi_u = search16(hi_s, topk)
    thr_hi = spread16(hi_u - HALF_BIAS)
    above = count16(hi_s, lambda blk: blk > thr_hi)

    def mask_low(i, carry):
        lo_s[rows(i), :] = jnp.where(hi_s[rows(i), :] == thr_hi, lo_s[rows(i), :], jnp.full((), -HALF_BIAS, I16))
        return carry

    lax.fori_loop(0, nblk, mask_low, 0)
    lo_u = search16(lo_s, topk - above)
    thr = lax.shift_left(hi_u - HALF_BIAS, HALF_BITS) | lo_u
    idx_bits = int(seq).bit_length()
    surplus = (count(lambda blk, idx: blk >= thr) != topk) & (thr != INT_MIN)
    has_tie = jnp.max(jnp.where(surplus, 1.0, 0.0)) > 0.5

    def resolve_ties():
        need = topk - count(lambda blk, idx: blk > thr)

        def index_bit(it, jj):
            cand = jj | lax.shift_left(jnp.int32(1), idx_bits - 1 - it)
            return jnp.where(count(lambda blk, idx: (blk == thr) & (idx < cand)) <= need, cand, jj)

        return lax.fori_loop(0, idx_bits, index_bit, jnp.zeros((1, tq), I32))

    tie_end = lax.cond(has_tie, resolve_ties, lambda: jnp.full((1, tq), (1 << idx_bits) - 1, I32))

    def bias_block(i, carry):
        blk = key_s[rows(i), :]
        idx = i * kb + row_iota
        sel = ((blk > thr) | ((blk == thr) & (idx < tie_end))) & (idx < lim)
        bias_s[rows(i), :] = jnp.where(sel, 0.0, NEG)
        return carry

    lax.fori_loop(0, nblk, bias_block, 0)

    q = q_ref[...]
    pair_lane = lax.broadcasted_iota(I32, (tq, 2 * HEAD_DIM), 1)
    qm = []
    for h in range(N_HEADS):
        pair = q[:, (h // 2) * 2 * HEAD_DIM:(h // 2 + 1) * 2 * HEAD_DIM]
        qm.append(lanes_to_queries(jnp.where((pair_lane // HEAD_DIM) == (h % 2), pair, jnp.zeros_like(pair))))
    acc_s[...] = jnp.zeros_like(acc_s)
    group = s_s.shape[0]

    def fold(t):
        return t.reshape(kb // SUBLANES, SUBLANES, tq)

    for g0 in range(0, N_HEADS, group):
        heads = range(g0, g0 + group)

        def score_pass(i, ms):
            bias = bias_s[rows(i), :]
            out = []
            for hh, h in enumerate(heads):
                kblk = k_ref[rows(i), (h // 2) * 2 * HEAD_DIM:(h // 2 + 1) * 2 * HEAD_DIM]
                s = jnp.dot(kblk, qm[h], preferred_element_type=F32) + bias
                s_s[hh, rows(i), :] = s
                out.append(jnp.maximum(ms[hh], jnp.max(fold(s), axis=0)))
            return tuple(out)

        ms = lax.fori_loop(0, nblk, score_pass, tuple(jnp.full((SUBLANES, tq), NEG, F32) for _ in heads))
        mx = [jnp.max(m, axis=0, keepdims=True) for m in ms]

        def value_pass(i, ls):
            out = []
            for hh, h in enumerate(heads):
                hs = slice(h * HEAD_DIM, (h + 1) * HEAD_DIM)
                p = jnp.exp2(s_s[hh, rows(i), :] - mx[hh])
                out.append(ls[hh] + jnp.sum(fold(p), axis=0))
                acc_s[hs, :] += jnp.dot(vt_ref[0, i, hs, :], p.astype(BF16), preferred_element_type=F32)
            return tuple(out)

        ls = lax.fori_loop(0, nblk, value_pass, tuple(jnp.zeros((SUBLANES, tq), F32) for _ in heads))
        for hh, h in enumerate(heads):
            hs = slice(h * HEAD_DIM, (h + 1) * HEAD_DIM)
            acc_s[hs, :] = acc_s[hs, :] / jnp.sum(ls[hh], axis=0, keepdims=True)
    o_ref[...] = acc_s[...].T.astype(BF16)


def _dsa_attention(q, qi, wi, k, vt, ki, batch, seq, tq, kb):
    n, w = q.shape
    topk = min(TOPK_MAX, seq // 4)
    nq = seq // tq
    tile = lambda width: pl.BlockSpec((tq, width), lambda b, j: (b * nq + j, 0))
    whole = lambda width: pl.BlockSpec((seq, width), lambda b, j: (b, 0))
    return pl.pallas_call(
        functools.partial(_dsa_body, seq=seq, tq=tq, kb=kb, topk=topk),
        grid=(batch, nq),
        in_specs=[tile(w), tile(w), tile(LANES), whole(w),
                  pl.BlockSpec((1, seq // kb, w, kb), lambda b, j: (b, 0, 0, 0)), whole(IDX_DIM)],
        out_specs=tile(w),
        out_shape=jax.ShapeDtypeStruct((n, w), BF16),
        scratch_shapes=[pltpu.VMEM((seq, tq), I32), pltpu.VMEM((seq, tq), F32), pltpu.VMEM((w, tq), F32),
                        pltpu.VMEM((ATTN_HEAD_GROUP, seq, tq), F32),
                        pltpu.VMEM((seq, tq), I16), pltpu.VMEM((seq, tq), I16)],
        compiler_params=_params(("parallel", "arbitrary")),
        name="dsa_attention",
    )(q, qi, wi, k, vt, ki)


def _mix_body(attn_ref, p_ref, halo_ref, gate_ref, x_ref, wa_ref, wpb_ref, wo_ref, pw_ref, ps_ref, g2_ref,
              h_ref, hn_ref, ext_s, *, tm, tps):
    st = pl.program_id(0) % tps
    ext_s[0:POOL_HALO, :] = jnp.where(st == 0, 0.0, halo_ref[...])
    ext_s[POOL_HALO:POOL_HALO + tm, :] = p_ref[...]
    t1 = (st * tm + 1 + lax.broadcasted_iota(I32, (tm, POOL_GROUP_DIM), 0)).astype(F32)
    mixed = []
    for g, win in enumerate(POOL_WINDOWS):
        ls = slice(g * POOL_GROUP_DIM, (g + 1) * POOL_GROUP_DIM)
        frame = ext_s[POOL_HALO:POOL_HALO + tm, ls]
        tot = frame
        for dlt in range(1, win):
            tot = tot + ext_s[POOL_HALO - dlt:POOL_HALO - dlt + tm, ls]
        pooled = tot / jnp.minimum(t1, float(win)) - frame
        mixed.append(jnp.dot(pooled.astype(BF16), pw_ref[g], preferred_element_type=F32))
    mixed = jnp.concatenate(mixed, axis=1) * ps_ref[...]
    y_pool = jnp.dot(mixed.astype(BF16), wpb_ref[...], preferred_element_type=F32)
    y_attn = jnp.dot(attn_ref[...], wa_ref[...], preferred_element_type=F32)
    d = y_attn.shape[1]
    gate = gate_ref[...].astype(F32)
    z = gate[:, :d] * y_attn + gate[:, d:] * y_pool
    h = x_ref[...] + jnp.dot(z.astype(BF16), wo_ref[...], preferred_element_type=F32)
    h_ref[...] = h
    hn = h * lax.rsqrt(jnp.mean(h * h, axis=-1, keepdims=True) + EPS) * g2_ref[...]
    hn_ref[...] = hn.astype(BF16)


def _mixer_output(attn, p, gate, x2, w_branch_attn, w_branch_pool, w_out, pool_w, pool_scale, norm2_g, seq, tm):
    n, d = x2.shape
    tps = seq // tm
    hb = tm // POOL_HALO
    const = lambda shape: pl.BlockSpec(shape, lambda i: (0,) * len(shape))
    row = lambda width: pl.BlockSpec((tm, width), lambda i: (i, 0))
    return pl.pallas_call(
        functools.partial(_mix_body, tm=tm, tps=tps),
        grid=(n // tm,),
        in_specs=[row(ATTN_WIDTH), row(POOL_WIDTH),
                  pl.BlockSpec((POOL_HALO, POOL_WIDTH), lambda i: (jnp.maximum(i * hb - 1, 0), 0)),
                  row(2 * d), row(d), const((ATTN_WIDTH, d)), const((POOL_WIDTH, d)), const((d, d)),
                  const(pool_w.shape), const((1, POOL_WIDTH)), const((1, d))],
        out_specs=(row(d), row(d)),
        out_shape=(jax.ShapeDtypeStruct((n, d), F32), jax.ShapeDtypeStruct((n, d), BF16)),
        scratch_shapes=[pltpu.VMEM((POOL_HALO + tm, POOL_WIDTH), F32)],
        compiler_params=_params(("parallel",)),
        name="mixer_output",
    )(attn, p, p, gate, x2, w_branch_attn.astype(BF16), w_branch_pool.astype(BF16), w_out.astype(BF16),
      pool_w.astype(BF16), pool_scale.reshape(1, POOL_WIDTH), norm2_g.reshape(1, d))


def _candidate_pairs():
    return [(a, b) for a in range(PEER_TOPK) for b in range(PEER_TOPK) if (a + 1) * (b + 1) <= PEER_TOPK]


def _sort_desc(v):
    v = list(v)
    n = len(v)
    k = 2
    while k <= n:
        j = k // 2
        while j >= 1:
            for i in range(n):
                m = i ^ j
                if m > i:
                    hi, lo = jnp.maximum(v[i], v[m]), jnp.minimum(v[i], v[m])
                    v[i], v[m] = (hi, lo) if (i & k) == 0 else (lo, hi)
            j //= 2
        k *= 2
    return v


def _merge_top(a, b):
    n = len(a)
    c = [jnp.maximum(a[i], b[n - 1 - i]) for i in range(n)]
    j = n // 2
    while j >= 1:
        for i in range(n):
            m = i ^ j
            if m > i:
                c[i], c[m] = jnp.maximum(c[i], c[m]), jnp.minimum(c[i], c[m])
        j //= 2
    return c


def _route_body(hn_ref, wqt_ref, kbig_ref, a0_ref, l0_ref, br_ref, vals_s, rank_s, ex_s, b1_s, r1_s, *, tr):
    nk, nh, kt = PEER_KEYS, PEER_HEADS, PEER_TOPK
    half_rows = nh * PEER_KEY_DIM
    qt = lax.dot_general(wqt_ref[...], hn_ref[...], NT_DIMS, preferred_element_type=F32).astype(BF16)
    for p in range(2):
        sub = jnp.dot(kbig_ref[p], qt[p * half_rows:(p + 1) * half_rows], preferred_element_type=F32)
        vals_s[p] = sub.reshape(nk, nh, tr)

    def best(p, lo, hi):
        if hi - lo == kt:
            return _sort_desc([vals_s[p, i] for i in range(lo, hi)])
        mid = (lo + hi) // 2
        return _merge_top(best(p, lo, mid), best(p, mid, hi))

    tops = [best(p, 0, nk) for p in range(2)]
    v0, v1 = tops
    tied = jnp.zeros((nh, tr), F32)
    for p in range(2):
        for a in range(kt - 1):
            tied = jnp.maximum(tied, jnp.where(tops[p][a] == tops[p][a + 1], 1.0, 0.0))
        above = [jnp.where(vals_s[p, i] >= tops[p][kt - 1], 1.0, 0.0) for i in range(nk)]
        while len(above) > 1:
            above = [above[i] + above[i + 1] for i in range(0, len(above), 2)]
        tied = jnp.maximum(tied, jnp.where(above[0] != float(kt), 1.0, 0.0))
    has_tie = jnp.max(tied) > 0.5

    pairs = _candidate_pairs()
    cand = [v0[a] + v1[b] for a, b in pairs]
    rank = [jnp.zeros((nh, tr), F32) for _ in pairs]
    for ia, (a0, a1) in enumerate(pairs):
        for ib in range(ia + 1, len(pairs)):
            b0, b1 = pairs[ib]
            if a0 <= b0 and a1 <= b1:
                rank[ib] = rank[ib] + 1.0
            else:
                wins = jnp.where(cand[ia] >= cand[ib], 1.0, 0.0)
                rank[ib] = rank[ib] + wins
                rank[ia] = rank[ia] + (1.0 - wins)
    e0 = [jnp.exp(v0[a] - v0[0]) for a in range(kt)]
    e1 = [jnp.exp(v1[b] - v1[0]) for b in range(kt)]
    width = [jnp.zeros((nh, tr), F32) for _ in range(kt)]
    z = jnp.zeros((nh, tr), F32)
    for ic, (a, b) in enumerate(pairs):
        sel = jnp.where(rank[ic] < float(kt), 1.0, 0.0)
        width[a] = width[a] + sel
        z = z + sel * (e0[a] * e1[b])
    inv_z = 1.0 / z

    def key_rows(i):
        return slice(i * nh, (i + 1) * nh)

    @pl.when(jnp.logical_not(has_tie))
    def _():
        for i in range(nk):
            x0, x1 = vals_s[0, i], vals_s[1, i]
            width_i = jnp.zeros((nh, tr), F32)
            for a in range(kt):
                width_i = jnp.where(x0 == v0[a], width[a], width_i)
            a0_ref[0, key_rows(i), :] = jnp.where(x0 >= v0[kt - 1], jnp.exp(x0 - v0[0]) * inv_z, 0.0)
            l0_ref[0, key_rows(i), :] = width_i
            above8 = v1[7] > x1
            piv = jnp.where(above8, v1[11], v1[3])
            above4 = piv > x1
            piv = jnp.where(above8, jnp.where(above4, v1[13], v1[9]), jnp.where(above4, v1[5], v1[1]))
            above2 = piv > x1
            piv = jnp.where(
                above8,
                jnp.where(above4, jnp.where(above2, v1[14], v1[12]), jnp.where(above2, v1[10], v1[8])),
                jnp.where(above4, jnp.where(above2, v1[6], v1[4]), jnp.where(above2, v1[2], v1[0])))
            pos = (jnp.where(above8, 8.0, 0.0) + jnp.where(above4, 4.0, 0.0) + jnp.where(above2, 2.0, 0.0)
                   + jnp.where(piv > x1, 1.0, 0.0))
            chosen = x1 >= v1[kt - 1]
            b1_s[key_rows(i), :] = jnp.where(chosen, jnp.exp(x1 - v1[0]), 0.0)
            r1_s[key_rows(i), :] = jnp.where(chosen, pos, float(kt))

    @pl.when(has_tie)
    def _():
        rank_s[...] = jnp.full(rank_s.shape, float(kt), F32)
        ex_s[...] = jnp.zeros(ex_s.shape, F32)
        key_iota = lax.broadcasted_iota(I32, (nk, nh, tr), 0)

        def extract(kk, carry):
            for p in range(2):
                v = vals_s[p]
                m = jnp.max(v, axis=0)
                idx = jnp.min(jnp.where(v == m[None], key_iota, nk), axis=0)
                hit = key_iota == idx[None]
                vals_s[p] = jnp.where(hit, -jnp.inf, v)
                rank_s[p] = jnp.where(hit, lax.convert_element_type(kk, F32), rank_s[p])
                ex_s[p] = jnp.where(hit, jnp.exp(m - tops[p][0])[None], ex_s[p])
            return carry

        lax.fori_loop(0, kt, extract, 0)
        r0 = rank_s[0]
        l0 = jnp.zeros((nk, nh, tr), F32)
        for a in range(kt):
            l0 = jnp.where(r0 == float(a), width[a][None], l0)
        a0_ref[0] = (ex_s[0] * inv_z[None]).reshape(nk * nh, tr)
        l0_ref[0] = l0.reshape(nk * nh, tr)
        b1_s[...] = ex_s[1].reshape(nk * nh, tr)
        r1_s[...] = rank_s[1].reshape(nk * nh, tr)

    pack = 2 * SUBLANES
    for h in range(nh):
        b1 = b1_s[pl.ds(h, nk, stride=nh), :].astype(BF16)
        r1 = r1_s[pl.ds(h, nk, stride=nh), :].astype(BF16)
        br_ref[0, h, :, 0] = b1.reshape(nk // pack, pack, tr)
        br_ref[0, h, :, 1] = r1.reshape(nk // pack, pack, tr)


def _peer_routing(hn, peer_wq, peer_subkeys, tr):
    n, d = hn.shape
    nk, nh, kd = PEER_KEYS, PEER_HEADS, PEER_KEY_DIM
    wqt = peer_wq.reshape(d, nh, 2, kd).transpose(2, 1, 3, 0).reshape(2 * nh * kd, d).astype(BF16)
    eye = jnp.eye(nh, dtype=peer_subkeys.dtype)
    kbig = jnp.einsum("hpnd,hg->pnhgd", peer_subkeys, eye).reshape(2, nk * nh, nh * kd).astype(BF16)
    rows = nk * nh
    assert tr == LANES
    out = jax.ShapeDtypeStruct((n // tr, rows, tr), F32)
    spec = pl.BlockSpec((1, rows, tr), lambda i: (i, 0, 0))
    pack = 2 * SUBLANES
    pair_shape = (nh, nk // pack, 2, pack, tr)
    return pl.pallas_call(
        functools.partial(_route_body, tr=tr),
        grid=(n // tr,),
        in_specs=[pl.BlockSpec((tr, d), lambda i: (i, 0)),
                  pl.BlockSpec(wqt.shape, lambda i: (0, 0)),
                  pl.BlockSpec(kbig.shape, lambda i: (0, 0, 0))],
        out_specs=(spec, spec, pl.BlockSpec((1,) + pair_shape, lambda i: (i, 0, 0, 0, 0, 0))),
        out_shape=(out, out, jax.ShapeDtypeStruct((n // tr,) + pair_shape, BF16)),
        scratch_shapes=[pltpu.VMEM((2, nk, nh, tr), F32), pltpu.VMEM((2, nk, nh, tr), F32),
                        pltpu.VMEM((2, nk, nh, tr), F32), pltpu.VMEM((rows, tr), F32), pltpu.VMEM((rows, tr), F32)],
        compiler_params=_params(("parallel",)),
        name="peer_routing",
    )(hn, wqt, kbig)


def _expert_body(*refs, tm, te, n_eb):
    n_slices = te // EXPERT_SLICE
    hn_ref, h_ref = refs[:2]
    u_refs = refs[2:2 + n_slices]
    vt_refs = refs[2 + n_slices:2 + 2 * n_slices]
    a0_ref, l0_ref, br_ref, y_ref, acc_s, g0_s, g1_s, ga0_s, ga1_s, hnt_s, br_s = refs[2 + 2 * n_slices:]
    step = pl.program_id(1)
    nk, nh = PEER_KEYS, PEER_HEADS
    pack = 2 * SUBLANES

    def gt(slot):
        return (g0_s, g1_s)[slot]

    def ga(slot):
        return (ga0_s, ga1_s)[slot]

    every = slice(0, te)
    slices = [slice(k * EXPERT_SLICE, (k + 1) * EXPERT_SLICE) for k in range(n_slices)]

    def project(k):
        return jnp.dot(u_refs[k][...], hnt_s[...], preferred_element_type=F32)

    def finish(act, src, dst, rs):
        act = 0.5 * act * (1.0 + lax.erf(act * (2.0 ** -0.5)))
        dst[rs, :] = src[rs, :] * act.astype(BF16)

    def value_part(src, k):
        return jnp.dot(vt_refs[k][0], src[slices[k], :], preferred_element_type=F32)

    def gate_piece(dst, il, c):
        i = step * (te // nk) + il
        g = jnp.zeros((nk, LANES), BF16)
        for h in range(nh):
            row = pl.ds(i * nh + h, 1)
            a_row = jnp.broadcast_to(a0_ref[c, row, :], (pack, LANES)).astype(BF16)
            l_row = jnp.broadcast_to(l0_ref[c, row, :], (pack, LANES)).astype(BF16)
            a_row = jnp.tile(a_row, (nk // pack, 1))
            l_row = jnp.tile(l_row, (nk // pack, 1))
            b1 = br_s[c, h, :, 0].reshape(nk, LANES)
            r1 = br_s[c, h, :, 1].reshape(nk, LANES)
            g = g + a_row * jnp.where(r1 < l_row, b1, jnp.zeros((), BF16))
        dst[il * nk:(il + 1) * nk, c * LANES:(c + 1) * LANES] = g

    def gate(dst, rs=every):
        for il in range(rs.start // nk, rs.stop // nk):
            for c in range(tm // LANES):
                gate_piece(dst, il, c)

    def apply(src):
        total = value_part(src, 0)
        for k in range(1, n_slices):
            total = total + value_part(src, k)
        acc_s[...] += total

    def activate(src, dst):
        for k in range(n_slices):
            finish(project(k), src, dst, slices[k])

    @pl.when(step == 0)
    def _():
        acc_s[...] = jnp.zeros_like(acc_s)
        hnt_s[...] = hn_ref[...].astype(F32).T.astype(BF16)
        br_s[...] = br_ref[...]
        gate(gt(0))

    @pl.when(step == 1)
    def _():
        activate(gt(0), ga(0))
        gate(gt(1))

    for p in range(2):
        @pl.when((step >= 2) & (step < n_eb) & (step % 2 == p))
        def _():
            pieces = [(il, c) for il in range(te // nk) for c in range(tm // LANES)]
            join = MATMUL_ROWS // EXPERT_SLICE
            n_big = n_slices // join
            per_pass = len(pieces) // (n_big * n_slices)
            assert per_pass * n_big * n_slices == len(pieces) and acc_s.shape[0] == te
            for mb in range(n_big):
                blk = slice(mb * MATMUL_ROWS, (mb + 1) * MATMUL_ROWS)
                act = val = None
                for ks, cols in enumerate(slices):
                    first = (mb * n_slices + ks) * per_pass
                    for piece in pieces[first:first + per_pass]:
                        gate_piece(gt(p), *piece)
                    u_rows = jnp.concatenate([u_refs[mb * join + r][:, cols] for r in range(join)], axis=0)
                    pu = jnp.dot(u_rows, hnt_s[cols, :], preferred_element_type=F32)
                    pv = jnp.dot(vt_refs[ks][0, blk, :], ga(p)[cols, :], preferred_element_type=F32)
                    act = pu if act is None else act + pu
                    val = pv if val is None else val + pv
                finish(act, gt(1 - p), ga(1 - p), blk)
                acc_s[blk, :] += val

    @pl.when(step == n_eb)
    def _():
        p = n_eb % 2
        apply(ga(p))
        activate(gt(1 - p), ga(1 - p))

    @pl.when(step == n_eb + 1)
    def _():
        apply(ga((n_eb + 1) % 2))
        y_ref[...] = h_ref[...] + acc_s[...].T


def _peer_experts(hn, h, peer_u, peer_v, a0, l0, br, tm, te):
    n, d = hn.shape
    ne = peer_u.shape[0]
    rows = a0.shape[1]
    tok = pl.BlockSpec((tm // LANES, rows, LANES), lambda t, e: (t, 0, 0))
    pair = pl.BlockSpec((tm // LANES,) + br.shape[1:], lambda t, e: (t, 0, 0, 0, 0, 0))
    n_eb = ne // te
    n_slices = te // EXPERT_SLICE
    u_b = peer_u.astype(BF16)
    vt_slabs = peer_v.astype(BF16).reshape(ne // EXPERT_SLICE, EXPERT_SLICE, d).transpose(0, 2, 1)
    u_specs = [pl.BlockSpec((EXPERT_SLICE, d), lambda t, s, k=k: (n_slices * jnp.clip(s - 1, 0, n_eb - 1) + k, 0))
               for k in range(n_slices)]
    vt_specs = [pl.BlockSpec((1, d, EXPERT_SLICE),
                             lambda t, s, k=k: (n_slices * jnp.clip(s - 2, 0, n_eb - 1) + k, 0, 0))
                for k in range(n_slices)]
    return pl.pallas_call(
        functools.partial(_expert_body, tm=tm, te=te, n_eb=n_eb),
        grid=(n // tm, n_eb + 2),
        in_specs=[pl.BlockSpec((tm, d), lambda t, s: (t, 0)), pl.BlockSpec((tm, d), lambda t, s: (t, 0)),
                  *u_specs, *vt_specs, tok, tok, pair],
        out_specs=pl.BlockSpec((tm, d), lambda t, s: (t, 0)),
        out_shape=jax.ShapeDtypeStruct((n, d), F32),
        scratch_shapes=[pltpu.VMEM((d, tm), F32), pltpu.VMEM((te, tm), BF16), pltpu.VMEM((te, tm), BF16),
                        pltpu.VMEM((te, tm), BF16), pltpu.VMEM((te, tm), BF16), pltpu.VMEM((d, tm), BF16),
                        pltpu.VMEM((tm // LANES,) + br.shape[1:], BF16)],
        compiler_params=_params(("parallel", "arbitrary")),
        name="peer_experts",
    )(hn, h, *([u_b] * n_slices), *([vt_slabs] * n_slices), a0, l0, br)


def _tiles(batch, seq):
    return dict(tm=256, tq=256, kb=512, tr=128, te_tm=512, te=1024)


def kernel(x, norm1_g, w_in, q_norm_g, k_norm_g, pool_w, pool_scale, w_branch_attn, w_branch_pool, w_out, norm2_g,
           peer_wq, peer_subkeys, peer_u, peer_v):
    batch, seq, d = x.shape
    t = _tiles(batch, seq)
    x2 = x.reshape(batch * seq, d)
    for l in range(norm1_g.shape[0]):
        q, k, vt, qi, ki, wi, p, gate = _input_projection(
            x2, norm1_g[l], w_in[l], q_norm_g[l], k_norm_g[l], batch, seq, t["tm"], t["kb"])
        attn = _dsa_attention(q, qi, wi, k, vt, ki, batch, seq, t["tq"], t["kb"])
        h, hn = _mixer_output(attn, p, gate, x2, w_branch_attn[l], w_branch_pool[l], w_out[l], pool_w[l],
                              pool_scale[l], norm2_g[l], seq, t["tm"])
        a0, l0, br = _peer_routing(hn, peer_wq[l], peer_subkeys[l], t["tr"])
        x2 = _peer_experts(hn, h, peer_u[l], peer_v[l], a0, l0, br, t["te_tm"], t["te"])
    return x2.reshape(batch, seq, d)
```

```python
import functools

import jax
import jax.numpy as jnp
import numpy as np
from jax import lax
from jax.experimental import pallas as pl
from jax.experimental.pallas import tpu as pltpu

CHUNK = 64
EPS = 1e-6
N_HEADS = 8
HEAD_DIM = 64
ATTN_WIDTH = N_HEADS * HEAD_DIM
ROT_HALF = HEAD_DIM // 8
ROPE_THETA = 500000.0
IDX_HEADS = 8
IDX_DIM = 64
TOPK_MAX = 256
POOL_WINDOWS = (2, 4, 8, 16)
POOL_WIDTH = 512
POOL_GROUP_DIM = POOL_WIDTH // len(POOL_WINDOWS)
POOL_HALO = 16
PEER_HEADS = 8
PEER_KEYS = 128
PEER_KEY_DIM = 64
PEER_TOPK = 16

LANES = 128
SUBLANES = 8
VMEM_LIMIT = 56 * 1024 * 1024

F32 = jnp.float32
BF16 = jnp.bfloat16
I32 = jnp.int32
I16 = jnp.int16
INT_MIN = -2147483648
HALF_BITS = 16
HALF_MASK = 0xFFFF
HALF_BIAS = 32768
NEG = -1e30
LOG2_E = 1.4426950408889634
ATTN_HEAD_GROUP = 4
EXPERT_SLICE = 256
NT_DIMS = (((1,), (1,)), ((), ()))


def _params(sem):
    return pltpu.CompilerParams(dimension_semantics=sem, vmem_limit_bytes=VMEM_LIMIT)


def _rope(t, c, s_lo, s_hi):
    w = t.shape[-1]
    return t * c + pltpu.roll(t, w - ROT_HALF, 1) * s_lo + pltpu.roll(t, ROT_HALF, 1) * s_hi


def _proj_body(x_ref, g1_ref, wqkv_ref, wqi_ref, wki_ref, wwi_ref, wp_ref, wgl_ref, qg_ref, kg_ref, bd_ref,
               c_ref, slo_ref, shi_ref,
               q_ref, k_ref, vt_ref, qi_ref, ki_ref, wi_ref, p_ref, gate_ref):
    x = x_ref[...]
    xn = x * lax.rsqrt(jnp.mean(x * x, axis=-1, keepdims=True) + EPS) * g1_ref[...]
    xb = xn.astype(BF16)
    c, s_lo, s_hi = c_ref[...], slo_ref[...], shi_ref[...]

    def head_norm(t, g):
        ms = jnp.dot((t * t).astype(BF16), bd_ref[...], preferred_element_type=F32)
        return t * lax.rsqrt(ms + EPS) * g

    qkv = jnp.dot(xb, wqkv_ref[...], preferred_element_type=F32)
    w = ATTN_WIDTH
    q = _rope(head_norm(qkv[:, :w], qg_ref[...]), c, s_lo, s_hi) * (HEAD_DIM ** -0.5 * LOG2_E)
    k = _rope(head_norm(qkv[:, w:2 * w], kg_ref[...]), c, s_lo, s_hi)
    q_ref[...] = q.astype(BF16)
    k_ref[...] = k.astype(BF16)
    vt_ref[0, 0] = qkv[:, 2 * w:].T.astype(BF16)
    qi = jnp.dot(xb, wqi_ref[...], preferred_element_type=F32)
    qi_ref[...] = (_rope(qi, c, s_lo, s_hi) * (IDX_DIM ** -0.5)).astype(BF16)
    ki = jnp.dot(xb, wki_ref[...], preferred_element_type=F32)
    ki = _rope(ki, c[:, :LANES], s_lo[:, :LANES], s_hi[:, :LANES])
    ki_ref[...] = ki[:, :IDX_DIM].astype(BF16)
    wi_ref[...] = jnp.dot(xb, wwi_ref[...], preferred_element_type=F32) * (IDX_HEADS ** -0.5)
    p_ref[...] = jnp.dot(xb, wp_ref[...], preferred_element_type=F32)
    gate_ref[...] = jax.nn.sigmoid(jnp.dot(xb, wgl_ref[...], preferred_element_type=F32)).astype(BF16)


def _rope_tables(seq):
    inv_freq = ROPE_THETA ** (-jnp.arange(ROT_HALF, dtype=F32) / ROT_HALF)
    ang = jnp.arange(seq, dtype=F32)[:, None] * inv_freq[None, :]
    cos, sin = jnp.cos(ang), jnp.sin(ang)
    rest = HEAD_DIM - 2 * ROT_HALF
    ones = jnp.ones((seq, rest), F32)
    zeros = jnp.zeros((seq, rest), F32)
    zh = jnp.zeros((seq, ROT_HALF), F32)
    c = jnp.concatenate([cos, cos, ones], axis=1)
    s_lo = jnp.concatenate([-sin, zh, zeros], axis=1)
    s_hi = jnp.concatenate([zh, sin, zeros], axis=1)
    tile = lambda t: jnp.tile(t, (1, N_HEADS))
    return tile(c), tile(s_lo), tile(s_hi)


def _input_projection(x2, norm1_g, w_in, q_norm_g, k_norm_g, batch, seq, tm, kb):
    n, d = x2.shape
    w = ATTN_WIDTH
    o = np.cumsum([0, w, w, w, IDX_HEADS * IDX_DIM, IDX_DIM, IDX_HEADS, POOL_WIDTH, 2 * d])
    wb = w_in.astype(BF16)
    wqkv = wb[:, o[0]:o[3]]
    wqi = wb[:, o[3]:o[4]]
    wki = jnp.pad(wb[:, o[4]:o[5]], ((0, 0), (0, LANES - IDX_DIM)))
    wwi = jnp.pad(wb[:, o[5]:o[6]], ((0, 0), (0, LANES - IDX_HEADS)))
    wp = wb[:, o[6]:o[7]]
    wgl = wb[:, o[7]:o[8]]
    bd = jnp.kron(jnp.eye(N_HEADS, dtype=F32), jnp.full((HEAD_DIM, HEAD_DIM), 1.0 / HEAD_DIM, F32)).astype(BF16)
    c, s_lo, s_hi = _rope_tables(seq)
    tps = seq // tm
    const = lambda shape: pl.BlockSpec(shape, lambda i: (0,) * len(shape))
    row = lambda width: pl.BlockSpec((tm, width), lambda i: (i, 0))
    tab = pl.BlockSpec((tm, w), lambda i: (i % tps, 0))
    per_kb = kb // tm
    out_shapes = (
        jax.ShapeDtypeStruct((n, w), BF16),
        jax.ShapeDtypeStruct((n, w), BF16),
        jax.ShapeDtypeStruct((batch, seq // kb, w, kb), BF16),
        jax.ShapeDtypeStruct((n, w), BF16),
        jax.ShapeDtypeStruct((n, IDX_DIM), BF16),
        jax.ShapeDtypeStruct((n, LANES), F32),
        jax.ShapeDtypeStruct((n, POOL_WIDTH), F32),
        jax.ShapeDtypeStruct((n, 2 * d), BF16),
    )
    out_specs = (
        row(w), row(w),
        pl.BlockSpec((1, 1, w, tm), lambda i: (i // tps, (i % tps) // per_kb, 0, (i % tps) % per_kb)),
        row(w), row(IDX_DIM), row(LANES), row(POOL_WIDTH), row(2 * d),
    )
    return pl.pallas_call(
        _proj_body,
        grid=(n // tm,),
        in_specs=[row(d), const((1, d)), const(wqkv.shape), const(wqi.shape), const(wki.shape), const(wwi.shape),
                  const(wp.shape), const(wgl.shape), const((1, w)), const((1, w)), const(bd.shape), tab, tab, tab],
        out_specs=out_specs,
        out_shape=out_shapes,
        compiler_params=_params(("parallel",)),
        name="input_projection",
    )(x2, norm1_g.reshape(1, d), wqkv, wqi, wki, wwi, wp, wgl,
      jnp.tile(q_norm_g, N_HEADS).reshape(1, w), jnp.tile(k_norm_g, N_HEADS).reshape(1, w), bd, c, s_lo, s_hi)


def _sortable(v):
    b = lax.bitcast_convert_type(v, I32)
    b = jnp.where(b == INT_MIN, 0, b)
    return jnp.where(b < 0, b ^ 0x7FFFFFFF, b)


def _dsa_body(q_ref, qi_ref, wi_ref, k_ref, vt_ref, ki_ref, o_ref, key_s, bias_s, acc_s, s_s, hi_s, lo_s,
              *, seq, tq, kb, topk):
    j = pl.program_id(1)
    nblk = ((j + 1) * tq + kb - 1) // kb
    lane = lax.broadcasted_iota(I32, (1, tq), 1)
    lim = j * tq + (lane // CHUNK + 1) * CHUNK
    row_iota = lax.broadcasted_iota(I32, (kb, tq), 0)
    wi_t = wi_ref[...].T[:IDX_HEADS, :]
    qi = qi_ref[...]

    def lanes_to_queries(t):
        return t.astype(F32).T.astype(BF16)

    qi_t = [lanes_to_queries(qi[:, h * IDX_DIM:(h + 1) * IDX_DIM]) for h in range(IDX_HEADS)]

    def rows(i):
        return pl.ds(pl.multiple_of(i * kb, kb), kb)

    def score_block(i, carry):
        kib = ki_ref[rows(i), :]
        acc = jnp.zeros((kb, tq), F32)
        for h in range(IDX_HEADS):
            lg = jnp.dot(kib, qi_t[h], preferred_element_type=F32)
            acc = acc + jnp.maximum(lg, 0.0) * wi_t[h:h + 1, :]
        key = jnp.where(i * kb + row_iota < lim, _sortable(acc), INT_MIN)
        key_s[rows(i), :] = key
        hi_s[rows(i), :] = (key >> HALF_BITS).astype(I16)
        lo_s[rows(i), :] = ((key & HALF_MASK) - HALF_BIAS).astype(I16)
        return carry

    lax.fori_loop(0, nblk, score_block, 0)

    def count(pred):
        def body(i, c):
            m = pred(key_s[rows(i), :], i * kb + row_iota)
            return c + jnp.sum(m.astype(I32).reshape(kb // SUBLANES, SUBLANES, tq), axis=0)
        c8 = lax.fori_loop(0, nblk, body, jnp.zeros((SUBLANES, tq), I32))
        return jnp.sum(c8, axis=0, keepdims=True)

    pack = 2 * SUBLANES
    one16, zero16 = jnp.ones((), I16), jnp.zeros((), I16)

    def spread16(v):
        return jnp.broadcast_to(v.astype(I16), (kb, tq))

    def count16(ref, pred):
        def body(i, c):
            m = pred(ref[rows(i), :])
            hit = jnp.where(m, one16, zero16)
            parts = [hit[r:r + pack, :] for r in range(0, kb, pack)]
            while len(parts) > 1:
                parts = [parts[r] + parts[r + 1] for r in range(0, len(parts), 2)]
            return c + parts[0]
        c16 = lax.fori_loop(0, nblk, body, jnp.zeros((pack, tq), I16))
        return jnp.sum(c16.astype(I32), axis=0, keepdims=True)

    def search16(ref, need):
        def bit(it, tu):
            cand_u = tu | lax.shift_left(jnp.int32(1), HALF_BITS - 1 - it)
            cand = spread16(cand_u - HALF_BIAS)
            return jnp.where(count16(ref, lambda blk: blk >= cand) >= need, cand_u, tu)
        return lax.fori_loop(0, HALF_BITS, bit, jnp.zeros((1, tq), I32))

    hi_u = search16(hi_s, topk)
    thr_hi = spread16(hi_u - HALF_BIAS)
    above = count16(hi_s, lambda blk: blk > thr_hi)

    def mask_low(i, carry):
        lo_s[rows(i), :] = jnp.where(hi_s[rows(i), :] == thr_hi, lo_s[rows(i), :], jnp.full((), -HALF_BIAS, I16))
        return carry

    lax.fori_loop(0, nblk, mask_low, 0)
    lo_u = search16(lo_s, topk - above)
    thr = lax.shift_left(hi_u - HALF_BIAS, HALF_BITS) | lo_u
    idx_bits = int(seq).bit_length()
    surplus = (count(lambda blk, idx: blk >= thr) != topk) & (thr != INT_MIN)
    has_tie = jnp.max(jnp.where(surplus, 1.0, 0.0)) > 0.5

    def resolve_ties():
        need = topk - count(lambda blk, idx: blk > thr)

        def index_bit(it, jj):
            cand = jj | lax.shift_left(jnp.int32(1), idx_bits - 1 - it)
            return jnp.where(count(lambda blk, idx: (blk == thr) & (idx < cand)) <= need, cand, jj)

        return lax.fori_loop(0, idx_bits, index_bit, jnp.zeros((1, tq), I32))

    tie_end = lax.cond(has_tie, resolve_ties, lambda: jnp.full((1, tq), (1 << idx_bits) - 1, I32))

    def bias_block(i, carry):
        blk = key_s[rows(i), :]
        idx = i * kb + row_iota
        sel = ((blk > thr) | ((blk == thr) & (idx < tie_end))) & (idx < lim)
        bias_s[rows(i), :] = jnp.where(sel, 0.0, NEG)
        return carry

    lax.fori_loop(0, nblk, bias_block, 0)

    q = q_ref[...]
    pair_lane = lax.broadcasted_iota(I32, (tq, 2 * HEAD_DIM), 1)
    qm = []
    for h in range(N_HEADS):
        pair = q[:, (h // 2) * 2 * HEAD_DIM:(h // 2 + 1) * 2 * HEAD_DIM]
        qm.append(lanes_to_queries(jnp.where((pair_lane // HEAD_DIM) == (h % 2), pair, jnp.zeros_like(pair))))
    acc_s[...] = jnp.zeros_like(acc_s)
    group = s_s.shape[0]

    def fold(t):
        return t.reshape(kb // SUBLANES, SUBLANES, tq)

    for g0 in range(0, N_HEADS, group):
        heads = range(g0, g0 + group)

        def score_pass(i, ms):
            bias = bias_s[rows(i), :]
            out = []
            for hh, h in enumerate(heads):
                kblk = k_ref[rows(i), (h // 2) * 2 * HEAD_DIM:(h // 2 + 1) * 2 * HEAD_DIM]
                s = jnp.dot(kblk, qm[h], preferred_element_type=F32) + bias
                s_s[hh, rows(i), :] = s
                out.append(jnp.maximum(ms[hh], jnp.max(fold(s), axis=0)))
            return tuple(out)

        ms = lax.fori_loop(0, nblk, score_pass, tuple(jnp.full((SUBLANES, tq), NEG, F32) for _ in heads))
        mx = [jnp.max(m, axis=0, keepdims=True) for m in ms]

        def value_pass(i, ls):
            out = []
            for hh, h in enumerate(heads):
                hs = slice(h * HEAD_DIM, (h + 1) * HEAD_DIM)
                p = jnp.exp2(s_s[hh, rows(i), :] - mx[hh])
                out.append(ls[hh] + jnp.sum(fold(p), axis=0))
                acc_s[hs, :] += jnp.dot(vt_ref[0, i, hs, :], p.astype(BF16), preferred_element_type=F32)
            return tuple(out)

        ls = lax.fori_loop(0, nblk, value_pass, tuple(jnp.zeros((SUBLANES, tq), F32) for _ in heads))
        for hh, h in enumerate(heads):
            hs = slice(h * HEAD_DIM, (h + 1) * HEAD_DIM)
            acc_s[hs, :] = acc_s[hs, :] / jnp.sum(ls[hh], axis=0, keepdims=True)
    o_ref[...] = acc_s[...].T.astype(BF16)


def _dsa_attention(q, qi, wi, k, vt, ki, batch, seq, tq, kb):
    n, w = q.shape
    topk = min(TOPK_MAX, seq // 4)
    nq = seq // tq
    tile = lambda width: pl.BlockSpec((tq, width), lambda b, j: (b * nq + j, 0))
    whole = lambda width: pl.BlockSpec((seq, width), lambda b, j: (b, 0))
    return pl.pallas_call(
        functools.partial(_dsa_body, seq=seq, tq=tq, kb=kb, topk=topk),
        grid=(batch, nq),
        in_specs=[tile(w), tile(w), tile(LANES), whole(w),
                  pl.BlockSpec((1, seq // kb, w, kb), lambda b, j: (b, 0, 0, 0)), whole(IDX_DIM)],
        out_specs=tile(w),
        out_shape=jax.ShapeDtypeStruct((n, w), BF16),
        scratch_shapes=[pltpu.VMEM((seq, tq), I32), pltpu.VMEM((seq, tq), F32), pltpu.VMEM((w, tq), F32),
                        pltpu.VMEM((ATTN_HEAD_GROUP, seq, tq), F32),
                        pltpu.VMEM((seq, tq), I16), pltpu.VMEM((seq, tq), I16)],
        compiler_params=_params(("parallel", "arbitrary")),
        name="dsa_attention",
    )(q, qi, wi, k, vt, ki)


def _mix_body(attn_ref, p_ref, halo_ref, gate_ref, x_ref, wa_ref, wpb_ref, wo_ref, pw_ref, ps_ref, g2_ref,
              h_ref, hn_ref, ext_s, *, tm, tps):
    st = pl.program_id(0) % tps
    ext_s[0:POOL_HALO, :] = jnp.where(st == 0, 0.0, halo_ref[...])
    ext_s[POOL_HALO:POOL_HALO + tm, :] = p_ref[...]
    t1 = (st * tm + 1 + lax.broadcasted_iota(I32, (tm, POOL_GROUP_DIM), 0)).astype(F32)
    mixed = []
    for g, win in enumerate(POOL_WINDOWS):
        ls = slice(g * POOL_GROUP_DIM, (g + 1) * POOL_GROUP_DIM)
        frame = ext_s[POOL_HALO:POOL_HALO + tm, ls]
        tot = frame
        for dlt in range(1, win):
            tot = tot + ext_s[POOL_HALO - dlt:POOL_HALO - dlt + tm, ls]
        pooled = tot / jnp.minimum(t1, float(win)) - frame
        mixed.append(jnp.dot(pooled.astype(BF16), pw_ref[g], preferred_element_type=F32))
    mixed = jnp.concatenate(mixed, axis=1) * ps_ref[...]
    y_pool = jnp.dot(mixed.astype(BF16), wpb_ref[...], preferred_element_type=F32)
    y_attn = jnp.dot(attn_ref[...], wa_ref[...], preferred_element_type=F32)
    d = y_attn.shape[1]
    gate = gate_ref[...].astype(F32)
    z = gate[:, :d] * y_attn + gate[:, d:] * y_pool
    h = x_ref[...] + jnp.dot(z.astype(BF16), wo_ref[...], preferred_element_type=F32)
    h_ref[...] = h
    hn = h * lax.rsqrt(jnp.mean(h * h, axis=-1, keepdims=True) + EPS) * g2_ref[...]
    hn_ref[...] = hn.astype(BF16)


def _mixer_output(attn, p, gate, x2, w_branch_attn, w_branch_pool, w_out, pool_w, pool_scale, norm2_g, seq, tm):
    n, d = x2.shape
    tps = seq // tm
    hb = tm // POOL_HALO
    const = lambda shape: pl.BlockSpec(shape, lambda i: (0,) * len(shape))
    row = lambda width: pl.BlockSpec((tm, width), lambda i: (i, 0))
    return pl.pallas_call(
        functools.partial(_mix_body, tm=tm, tps=tps),
        grid=(n // tm,),
        in_specs=[row(ATTN_WIDTH), row(POOL_WIDTH),
                  pl.BlockSpec((POOL_HALO, POOL_WIDTH), lambda i: (jnp.maximum(i * hb - 1, 0), 0)),
                  row(2 * d), row(d), const((ATTN_WIDTH, d)), const((POOL_WIDTH, d)), const((d, d)),
                  const(pool_w.shape), const((1, POOL_WIDTH)), const((1, d))],
        out_specs=(row(d), row(d)),
        out_shape=(jax.ShapeDtypeStruct((n, d), F32), jax.ShapeDtypeStruct((n, d), BF16)),
        scratch_shapes=[pltpu.VMEM((POOL_HALO + tm, POOL_WIDTH), F32)],
        compiler_params=_params(("parallel",)),
        name="mixer_output",
    )(attn, p, p, gate, x2, w_branch_attn.astype(BF16), w_branch_pool.astype(BF16), w_out.astype(BF16),
      pool_w.astype(BF16), pool_scale.reshape(1, POOL_WIDTH), norm2_g.reshape(1, d))


def _candidate_pairs():
    return [(a, b) for a in range(PEER_TOPK) for b in range(PEER_TOPK) if (a + 1) * (b + 1) <= PEER_TOPK]


def _sort_desc(v):
    v = list(v)
    n = len(v)
    k = 2
    while k <= n:
        j = k // 2
        while j >= 1:
            for i in range(n):
                m = i ^ j
                if m > i:
                    hi, lo = jnp.maximum(v[i], v[m]), jnp.minimum(v[i], v[m])
                    v[i], v[m] = (hi, lo) if (i & k) == 0 else (lo, hi)
            j //= 2
        k *= 2
    return v


def _merge_top(a, b):
    n = len(a)
    c = [jnp.maximum(a[i], b[n - 1 - i]) for i in range(n)]
    j = n // 2
    while j >= 1:
        for i in range(n):
            m = i ^ j
            if m > i:
                c[i], c[m] = jnp.maximum(c[i], c[m]), jnp.minimum(c[i], c[m])
        j //= 2
    return c


def _route_body(hn_ref, wqt_ref, kbig_ref, a0_ref, l0_ref, br_ref, vals_s, rank_s, ex_s, b1_s, r1_s, *, tr):
    nk, nh, kt = PEER_KEYS, PEER_HEADS, PEER_TOPK
    half_rows = nh * PEER_KEY_DIM
    qt = lax.dot_general(wqt_ref[...], hn_ref[...], NT_DIMS, preferred_element_type=F32).astype(BF16)
    for p in range(2):
        sub = jnp.dot(kbig_ref[p], qt[p * half_rows:(p + 1) * half_rows], preferred_element_type=F32)
        vals_s[p] = sub.reshape(nk, nh, tr)

    def best(p, lo, hi):
        if hi - lo == kt:
            return _sort_desc([vals_s[p, i] for i in range(lo, hi)])
        mid = (lo + hi) // 2
        return _merge_top(best(p, lo, mid), best(p, mid, hi))

    tops = [best(p, 0, nk) for p in range(2)]
    v0, v1 = tops
    tied = jnp.zeros((nh, tr), F32)
    for p in range(2):
        for a in range(kt - 1):
            tied = jnp.maximum(tied, jnp.where(tops[p][a] == tops[p][a + 1], 1.0, 0.0))
        above = [jnp.where(vals_s[p, i] >= tops[p][kt - 1], 1.0, 0.0) for i in range(nk)]
        while len(above) > 1:
            above = [above[i] + above[i + 1] for i in range(0, len(above), 2)]
        tied = jnp.maximum(tied, jnp.where(above[0] != float(kt), 1.0, 0.0))
    has_tie = jnp.max(tied) > 0.5

    pairs = _candidate_pairs()
    cand = [v0[a] + v1[b] for a, b in pairs]
    rank = [jnp.zeros((nh, tr), F32) for _ in pairs]
    for ia, (a0, a1) in enumerate(pairs):
        for ib in range(ia + 1, len(pairs)):
            b0, b1 = pairs[ib]
            if a0 <= b0 and a1 <= b1:
                rank[ib] = rank[ib] + 1.0
            else:
                wins = jnp.where(cand[ia] >= cand[ib], 1.0, 0.0)
                rank[ib] = rank[ib] + wins
                rank[ia] = rank[ia] + (1.0 - wins)
    e0 = [jnp.exp(v0[a] - v0[0]) for a in range(kt)]
    e1 = [jnp.exp(v1[b] - v1[0]) for b in range(kt)]
    width = [jnp.zeros((nh, tr), F32) for _ in range(kt)]
    z = jnp.zeros((nh, tr), F32)
    for ic, (a, b) in enumerate(pairs):
        sel = jnp.where(rank[ic] < float(kt), 1.0, 0.0)
        width[a] = width[a] + sel
        z = z + sel * (e0[a] * e1[b])
    inv_z = 1.0 / z

    def key_rows(i):
        return slice(i * nh, (i + 1) * nh)

    @pl.when(jnp.logical_not(has_tie))
    def _():
        for i in range(nk):
            x0, x1 = vals_s[0, i], vals_s[1, i]
            width_i = jnp.zeros((nh, tr), F32)
            for a in range(kt):
                width_i = jnp.where(x0 == v0[a], width[a], width_i)
            a0_ref[0, key_rows(i), :] = jnp.where(x0 >= v0[kt - 1], jnp.exp(x0 - v0[0]) * inv_z, 0.0)
            l0_ref[0, key_rows(i), :] = width_i
            above8 = v1[7] > x1
            piv = jnp.where(above8, v1[11], v1[3])
            above4 = piv > x1
            piv = jnp.where(above8, jnp.where(above4, v1[13], v1[9]), jnp.where(above4, v1[5], v1[1]))
            above2 = piv > x1
            piv = jnp.where(
                above8,
                jnp.where(above4, jnp.where(above2, v1[14], v1[12]), jnp.where(above2, v1[10], v1[8])),
                jnp.where(above4, jnp.where(above2, v1[6], v1[4]), jnp.where(above2, v1[2], v1[0])))
            pos = (jnp.where(above8, 8.0, 0.0) + jnp.where(above4, 4.0, 0.0) + jnp.where(above2, 2.0, 0.0)
                   + jnp.where(piv > x1, 1.0, 0.0))
            chosen = x1 >= v1[kt - 1]
            b1_s[key_rows(i), :] = jnp.where(chosen, jnp.exp(x1 - v1[0]), 0.0)
            r1_s[key_rows(i), :] = jnp.where(chosen, pos, float(kt))

    @pl.when(has_tie)
    def _():
        rank_s[...] = jnp.full(rank_s.shape, float(kt), F32)
        ex_s[...] = jnp.zeros(ex_s.shape, F32)
        key_iota = lax.broadcasted_iota(I32, (nk, nh, tr), 0)

        def extract(kk, carry):
            for p in range(2):
                v = vals_s[p]
                m = jnp.max(v, axis=0)
                idx = jnp.min(jnp.where(v == m[None], key_iota, nk), axis=0)
                hit = key_iota == idx[None]
                vals_s[p] = jnp.where(hit, -jnp.inf, v)
                rank_s[p] = jnp.where(hit, lax.convert_element_type(kk, F32), rank_s[p])
                ex_s[p] = jnp.where(hit, jnp.exp(m - tops[p][0])[None], ex_s[p])
            return carry

        lax.fori_loop(0, kt, extract, 0)
        r0 = rank_s[0]
        l0 = jnp.zeros((nk, nh, tr), F32)
        for a in range(kt):
            l0 = jnp.where(r0 == float(a), width[a][None], l0)
        a0_ref[0] = (ex_s[0] * inv_z[None]).reshape(nk * nh, tr)
        l0_ref[0] = l0.reshape(nk * nh, tr)
        b1_s[...] = ex_s[1].reshape(nk * nh, tr)
        r1_s[...] = rank_s[1].reshape(nk * nh, tr)

    pack = 2 * SUBLANES
    for h in range(nh):
        b1 = b1_s[pl.ds(h, nk, stride=nh), :].astype(BF16)
        r1 = r1_s[pl.ds(h, nk, stride=nh), :].astype(BF16)
        br_ref[0, h, :, 0] = b1.reshape(nk // pack, pack, tr)
        br_ref[0, h, :, 1] = r1.reshape(nk // pack, pack, tr)


def _peer_routing(hn, peer_wq, peer_subkeys, tr):
    n, d = hn.shape
    nk, nh, kd = PEER_KEYS, PEER_HEADS, PEER_KEY_DIM
    wqt = peer_wq.reshape(d, nh, 2, kd).transpose(2, 1, 3, 0).reshape(2 * nh * kd, d).astype(BF16)
    eye = jnp.eye(nh, dtype=peer_subkeys.dtype)
    kbig = jnp.einsum("hpnd,hg->pnhgd", peer_subkeys, eye).reshape(2, nk * nh, nh * kd).astype(BF16)
    rows = nk * nh
    assert tr == LANES
    out = jax.ShapeDtypeStruct((n // tr, rows, tr), F32)
    spec = pl.BlockSpec((1, rows, tr), lambda i: (i, 0, 0))
    pack = 2 * SUBLANES
    pair_shape = (nh, nk // pack, 2, pack, tr)
    return pl.pallas_call(
        functools.partial(_route_body, tr=tr),
        grid=(n // tr,),
        in_specs=[pl.BlockSpec((tr, d), lambda i: (i, 0)),
                  pl.BlockSpec(wqt.shape, lambda i: (0, 0)),
                  pl.BlockSpec(kbig.shape, lambda i: (0, 0, 0))],
        out_specs=(spec, spec, pl.BlockSpec((1,) + pair_shape, lambda i: (i, 0, 0, 0, 0, 0))),
        out_shape=(out, out, jax.ShapeDtypeStruct((n // tr,) + pair_shape, BF16)),
        scratch_shapes=[pltpu.VMEM((2, nk, nh, tr), F32), pltpu.VMEM((2, nk, nh, tr), F32),
                        pltpu.VMEM((2, nk, nh, tr), F32), pltpu.VMEM((rows, tr), F32), pltpu.VMEM((rows, tr), F32)],
        compiler_params=_params(("parallel",)),
        name="peer_routing",
    )(hn, wqt, kbig)


def _expert_body(*refs, tm, te, n_eb):
    n_slices = te // EXPERT_SLICE
    hn_ref, h_ref = refs[:2]
    u_refs = refs[2:2 + n_slices]
    vt_refs = refs[2 + n_slices:2 + 2 * n_slices]
    a0_ref, l0_ref, br_ref, y_ref, acc_s, g0_s, ga0_s, ga1_s, hnt_s, br_s = refs[2 + 2 * n_slices:]
    step = pl.program_id(1)
    nk, nh = PEER_KEYS, PEER_HEADS
    pack = 2 * SUBLANES

    def gt(slot):
        return (g0_s,)[slot]

    def ga(slot):
        return (ga0_s, ga1_s)[slot]

    every = slice(0, te)
    slices = [slice(k * EXPERT_SLICE, (k + 1) * EXPERT_SLICE) for k in range(n_slices)]

    def project(k):
        return jnp.dot(u_refs[k][...], hnt_s[...], preferred_element_type=F32)

    def finish(act, src, dst, rs):
        act = 0.5 * act * (1.0 + lax.erf(act * (2.0 ** -0.5)))
        dst[rs, :] = src[rs, :] * act.astype(BF16)

    def value_part(src, k):
        return jnp.dot(vt_refs[k][0], src[slices[k], :], preferred_element_type=F32)

    def gate_piece(dst, il, c):
        i = step * (te // nk) + il
        g = jnp.zeros((nk, LANES), BF16)
        for h in range(nh):
            row = pl.ds(i * nh + h, 1)
            a_row = jnp.broadcast_to(a0_ref[c, row, :], (pack, LANES)).astype(BF16)
            l_row = jnp.broadcast_to(l0_ref[c, row, :], (pack, LANES)).astype(BF16)
            a_row = jnp.tile(a_row, (nk // pack, 1))
            l_row = jnp.tile(l_row, (nk // pack, 1))
            b1 = br_s[c, h, :, 0].reshape(nk, LANES)
            r1 = br_s[c, h, :, 1].reshape(nk, LANES)
            g = g + a_row * jnp.where(r1 < l_row, b1, jnp.zeros((), BF16))
        dst[il * nk:(il + 1) * nk, c * LANES:(c + 1) * LANES] = g

    def gate(dst, rs=every):
        for il in range(rs.start // nk, rs.stop // nk):
            for c in range(tm // LANES):
                gate_piece(dst, il, c)

    def apply(src):
        total = value_part(src, 0)
        for k in range(1, n_slices):
            total = total + value_part(src, k)
        acc_s[...] += total

    def activate(src, dst):
        for k in range(n_slices):
            finish(project(k), src, dst, slices[k])

    @pl.when(step == 0)
    def _():
        acc_s[...] = jnp.zeros_like(acc_s)
        hnt_s[...] = hn_ref[...].astype(F32).T.astype(BF16)
        br_s[...] = br_ref[...]
        gate(gt(0))
        activate(gt(0), ga(0))

    for p in range(2):
        @pl.when((step >= 1) & (step < n_eb) & (step % 2 == p))
        def _():
            pieces = [(il, c) for il in range(te // nk) for c in range(tm // LANES)]
            per_pass = len(pieces) // (n_slices * n_slices)
            assert per_pass * n_slices * n_slices == len(pieces) and acc_s.shape[0] == te
            for mb, blk in enumerate(slices):
                act = val = None
                for ks, cols in enumerate(slices):
                    first = (mb * n_slices + ks) * per_pass
                    for piece in pieces[first:first + per_pass]:
                        gate_piece(gt(0), *piece)
                    pu = jnp.dot(u_refs[mb][:, cols], hnt_s[cols, :], preferred_element_type=F32)
                    pv = jnp.dot(vt_refs[ks][0, blk, :], ga(1 - p)[cols, :], preferred_element_type=F32)
                    act = pu if act is None else act + pu
                    val = pv if val is None else val + pv
                finish(act, gt(0), ga(p), blk)
                acc_s[blk, :] += val

    @pl.when(step == n_eb)
    def _():
        apply(ga((n_eb - 1) % 2))
        y_ref[...] = h_ref[...] + acc_s[...].T


def _peer_experts(hn, h, peer_u, peer_v, a0, l0, br, tm, te):
    n, d = hn.shape
    ne = peer_u.shape[0]
    rows = a0.shape[1]
    tok = pl.BlockSpec((tm // LANES, rows, LANES), lambda t, e: (t, 0, 0))
    pair = pl.BlockSpec((tm // LANES,) + br.shape[1:], lambda t, e: (t, 0, 0, 0, 0, 0))
    n_eb = ne // te
    n_slices = te // EXPERT_SLICE
    u_b = peer_u.astype(BF16)
    vt_slabs = peer_v.astype(BF16).reshape(ne // EXPERT_SLICE, EXPERT_SLICE, d).transpose(0, 2, 1)
    u_specs = [pl.BlockSpec((EXPERT_SLICE, d), lambda t, s, k=k: (n_slices * jnp.clip(s, 0, n_eb - 1) + k, 0))
               for k in range(n_slices)]
    vt_specs = [pl.BlockSpec((1, d, EXPERT_SLICE),
                             lambda t, s, k=k: (n_slices * jnp.clip(s - 1, 0, n_eb - 1) + k, 0, 0))
                for k in range(n_slices)]
    return pl.pallas_call(
        functools.partial(_expert_body, tm=tm, te=te, n_eb=n_eb),
        grid=(n // tm, n_eb + 1),
        in_specs=[pl.BlockSpec((tm, d), lambda t, s: (t, 0)), pl.BlockSpec((tm, d), lambda t, s: (t, 0)),
                  *u_specs, *vt_specs, tok, tok, pair],
        out_specs=pl.BlockSpec((tm, d), lambda t, s: (t, 0)),
        out_shape=jax.ShapeDtypeStruct((n, d), F32),
        scratch_shapes=[pltpu.VMEM((d, tm), F32), pltpu.VMEM((te, tm), BF16), pltpu.VMEM((te, tm), BF16),
                        pltpu.VMEM((te, tm), BF16), pltpu.VMEM((d, tm), BF16),
                        pltpu.VMEM((tm // LANES,) + br.shape[1:], BF16)],
        compiler_params=_params(("parallel", "arbitrary")),
        name="peer_experts",
    )(hn, h, *([u_b] * n_slices), *([vt_slabs] * n_slices), a0, l0, br)


def _tiles(batch, seq):
    return dict(tm=256, tq=256, kb=512, tr=128, te_tm=512, te=1024)


def kernel(x, norm1_g, w_in, q_norm_g, k_norm_g, pool_w, pool_scale, w_branch_attn, w_branch_pool, w_out, norm2_g,
           peer_wq, peer_subkeys, peer_u, peer_v):
    batch, seq, d = x.shape
    t = _tiles(batch, seq)
    x2 = x.reshape(batch * seq, d)
    for l in range(norm1_g.shape[0]):
        q, k, vt, qi, ki, wi, p, gate = _input_projection(
            x2, norm1_g[l], w_in[l], q_norm_g[l], k_norm_g[l], batch, seq, t["tm"], t["kb"])
        attn = _dsa_attention(q, qi, wi, k, vt, ki, batch, seq, t["tq"], t["kb"])
        h, hn = _mixer_output(attn, p, gate, x2, w_branch_attn[l], w_branch_pool[l], w_out[l], pool_w[l],
                              pool_scale[l], norm2_g[l], seq, t["tm"])
        a0, l0, br = _peer_routing(hn, peer_wq[l], peer_subkeys[l], t["tr"])
        x2 = _peer_experts(hn, h, peer_u[l], peer_v[l], a0, l0, br, t["te_tm"], t["te"])
    return x2.reshape(batch, seq, d)
```
